```python
import math
import jax
import jax.numpy as jnp
from jax import lax
import numpy as np

D_MODEL = 2048
BATCH = 1
SEQ = 16384
DEPTH = 2

CHUNK = 64
D_MIX = D_MODEL
GROUP_W = D_MIX // 4
S5_GROUP_CH = 16
S5_GROUPS = GROUP_W // S5_GROUP_CH
S5_STATE = 64
CONV_W = 3
SGU_BLK = 128
SGU_HEADS = 4
SGU_HEAD_DIM = GROUP_W // SGU_HEADS
DIFF_HEADS = 4
DIFF_QK_DIM = GROUP_W // (2 * DIFF_HEADS)
DIFF_V_DIM = 2 * DIFF_QK_DIM
QBLK = 128
NUM_BUCKETS = 32
MAX_DISTANCE = 128
N_EXPERTS = 64
TOP_K = 8
N_EXPERT_GROUPS = 8
TOPK_GROUPS = 4
D_EXPERT = D_MODEL // 4
D_SHARED = D_EXPERT
ROUTED_SCALE = 2.5
MOE_BLK = 128
PROJ_W = 9 * GROUP_W
DN_ALPHA = (2 * DEPTH) ** 0.25
DN_BETA = (8 * DEPTH) ** -0.25
EPS = 1e-5
NEG_INF = -1e30

kernel_name = 'hybrid_chunk_causal_s5_conv_sgu_diffattn_moe'


def layer_norm(x, g, b):
    xf = x.astype(jnp.float32)
    mu = jnp.mean(xf, -1, keepdims=True)
    var = jnp.mean(jnp.square(xf - mu), -1, keepdims=True)
    return ((xf - mu) * lax.rsqrt(var + EPS) * g.astype(jnp.float32) + b.astype(jnp.float32)).astype(x.dtype)


def rms_norm(x, g):
    xf = x.astype(jnp.float32)
    return (xf * lax.rsqrt(jnp.mean(jnp.square(xf), -1, keepdims=True) + EPS) * g.astype(jnp.float32)).astype(x.dtype)


def _ssm_combine(left, right):
    a_l, b_l = left
    a_r, b_r = right
    return a_r * a_l, a_r * b_l + b_r


def s5_mixer(u, lam_re, lam_im, log_dt, b_re, b_im, c_re, c_im, d, w_glu):
    Bt, L, _ = u.shape
    f32 = jnp.float32
    uf = u.astype(f32).reshape(Bt, L, S5_GROUPS, S5_GROUP_CH)
    dt = jnp.exp(log_dt.astype(f32))[:, None]
    lam = lax.complex(lam_re.astype(f32), lam_im.astype(f32))
    lam_bar = jnp.exp(lam * dt)
    b = lax.complex(b_re.astype(f32), b_im.astype(f32))
    b_bar = ((lam_bar - 1.0) / lam)[..., None] * b
    bu = jnp.einsum('gph,blgh->blgp', b_bar, uf.astype(jnp.complex64))
    a = jnp.broadcast_to(lam_bar, bu.shape)
    _, states = lax.associative_scan(_ssm_combine, (a, bu), axis=1)
    c = lax.complex(c_re.astype(f32), c_im.astype(f32))
    y = jnp.real(jnp.einsum('ghp,blgp->blgh', c, states))
    y = y + d.astype(f32).reshape(S5_GROUPS, S5_GROUP_CH) * uf
    y = jax.nn.gelu(y.reshape(Bt, L, GROUP_W))
    y = y * jax.nn.sigmoid(y @ w_glu.astype(f32))
    return y.astype(u.dtype)


def short_conv_mixer(bg, cg, h, conv_w):
    L = h.shape[1]
    z = cg * h
    zp = jnp.pad(z, ((0, 0), (CONV_W - 1, 0), (0, 0)))
    conv = sum(conv_w[k] * zp[:, k:k + L] for k in range(CONV_W))
    return bg * conv


def sgu_mixer(u, v, ln_g, ln_b, w_s, b_s):
    Bt, L, _ = v.shape
    u = jax.nn.gelu(u)
    v = layer_norm(jax.nn.gelu(v), ln_g, ln_b)
    vb = v.reshape(Bt, L // SGU_BLK, SGU_BLK, SGU_HEADS, SGU_HEAD_DIM)
    pos = jnp.arange(SGU_BLK)
    mask = (pos[None, :] // CHUNK) <= (pos[:, None] // CHUNK)
    ws = jnp.where(mask[None], w_s, 0.0)
    mixed = jnp.einsum('hij,bnjhc->bnihc', ws, vb) + b_s.T[:, :, None]
    return u * mixed.reshape(Bt, L, GROUP_W)


def t5_bucket(rel):
    half = NUM_BUCKETS // 2
    ret = jnp.where(rel > 0, half, 0)
    n = jnp.abs(rel)
    max_exact = half // 2
    large = max_exact + (jnp.log(jnp.maximum(n, 1).astype(jnp.float32) / max_exact)
                         / math.log(MAX_DISTANCE / max_exact) * (half - max_exact)).astype(jnp.int32)
    large = jnp.minimum(large, half - 1)
    return ret + jnp.where(n < max_exact, n, large)


def diff_attention(q, k, v, lq1, lk1, lq2, lk2, subln_g, rel_bias, lambda_init):
    Bt, L = q.shape[0], q.shape[1]
    f32 = jnp.float32
    nqb = L // QBLK
    lam = (jnp.exp(jnp.sum(lq1.astype(f32) * lk1.astype(f32)))
           - jnp.exp(jnp.sum(lq2.astype(f32) * lk2.astype(f32))) + lambda_init)
    kpos = jnp.arange(L)
    k_chunk = kpos // CHUNK
    qb = jnp.moveaxis((q * DIFF_QK_DIM ** -0.5).reshape(Bt, nqb, QBLK, DIFF_HEADS, 2, DIFF_QK_DIM), 1, 0)

    def block(args):
        qblk, bi = args
        qpos = bi * QBLK + jnp.arange(QBLK)
        bias = jnp.transpose(rel_bias[t5_bucket(kpos[None, :] - qpos[:, None])], (2, 0, 1)).astype(f32)
        mask = k_chunk[None, :] <= (qpos // CHUNK)[:, None]
        s = jnp.einsum('bqhmd,bkhmd->bmhqk', qblk, k).astype(f32) + bias
        s = jnp.where(mask, s, NEG_INF)
        p = jax.nn.softmax(s, axis=-1)
        attn = p[:, 0] - lam * p[:, 1]
        return jnp.einsum('bhqk,bkhe->bqhe', attn.astype(v.dtype), v)

    out = lax.map(block, (qb, jnp.arange(nqb)))
    out = jnp.moveaxis(out, 0, 1).reshape(Bt, L, DIFF_HEADS, DIFF_V_DIM)
    out = rms_norm(out, subln_g) * (1.0 - lambda_init)
    return out.reshape(Bt, L, DIFF_HEADS * DIFF_V_DIM)


def moe_ffn(h, router_w, router_bias, w_gu, w_down, sh_gu, sh_down):
    Bt, L, D = h.shape
    T = Bt * L
    f32 = jnp.float32
    xt = h.reshape(T, D)
    scores = jax.nn.sigmoid((xt @ router_w).astype(f32))
    sel = scores + router_bias.astype(f32)
    grp = sel.reshape(T, N_EXPERT_GROUPS, N_EXPERTS // N_EXPERT_GROUPS)
    grp_score = jnp.sum(lax.top_k(grp, 2)[0], axis=-1)
    _, top_groups = lax.top_k(grp_score, TOPK_GROUPS)
    group_mask = jnp.any(top_groups[..., None] == jnp.arange(N_EXPERT_GROUPS), axis=1)
    expert_mask = jnp.repeat(group_mask, N_EXPERTS // N_EXPERT_GROUPS, axis=1)
    _, top_idx = lax.top_k(jnp.where(expert_mask, sel, -jnp.inf), TOP_K)
    top_w = jnp.take_along_axis(scores, top_idx, axis=1)
    top_w = top_w / (jnp.sum(top_w, -1, keepdims=True) + 1e-20) * ROUTED_SCALE

    A = T * TOP_K
    e_flat = top_idx.reshape(A)
    tok_flat = jnp.broadcast_to(jnp.arange(T, dtype=jnp.int32)[:, None], (T, TOP_K)).reshape(A)
    w_flat = top_w.reshape(A)
    order = jnp.argsort(e_flat)
    e_s, tok_s, w_s = e_flat[order], tok_flat[order], w_flat[order]
    counts = jnp.zeros((N_EXPERTS,), jnp.int32).at[e_flat].add(1)
    starts = jnp.cumsum(counts) - counts
    padded = (counts + MOE_BLK - 1) // MOE_BLK * MOE_BLK
    pends = jnp.cumsum(padded)
    pstarts = pends - padded
    dest = pstarts[e_s] + (jnp.arange(A) - starts[e_s])
    NB = -(-A // MOE_BLK) + N_EXPERTS
    row_tok = jnp.full((NB * MOE_BLK,), T, jnp.int32).at[dest].set(tok_s)
    row_w = jnp.zeros((NB * MOE_BLK,), f32).at[dest].set(w_s)
    block_e = jnp.minimum(jnp.searchsorted(pends, jnp.arange(NB) * MOE_BLK, side='right'), N_EXPERTS - 1)
    x_pad = jnp.concatenate([xt, jnp.zeros((1, D), xt.dtype)], axis=0)

    def body(acc, blk):
        tok, w, e = blk
        xs = x_pad[tok]
        g, u = jnp.split(xs @ w_gu[e], 2, axis=-1)
        y = (jax.nn.silu(g) * u) @ w_down[e]
        return acc.at[tok].add(y * w[:, None].astype(y.dtype)), None

    acc, _ = lax.scan(body, jnp.zeros((T + 1, D), xt.dtype),
                      (row_tok.reshape(NB, MOE_BLK), row_w.reshape(NB, MOE_BLK), block_e))
    g, u = jnp.split(xt @ sh_gu, 2, axis=-1)
    shared = (jax.nn.silu(g) * u) @ sh_down
    return (acc[:T] + shared).reshape(Bt, L, D)


def setup_inputs(seed: int = 0) -> dict:
    key = jax.random.key(seed)
    ks = iter(jax.random.split(key, 40))
    f32 = jnp.float32

    def nrm(shape, scale):
        return jax.random.normal(next(ks), shape, f32) * scale

    L, G, P, H = DEPTH, S5_GROUPS, S5_STATE, S5_GROUP_CH
    n = jnp.arange(P, dtype=f32)
    return {
        'x': nrm((BATCH, SEQ, D_MODEL), 1.0),
        'w_in': nrm((L, D_MODEL, PROJ_W), D_MODEL ** -0.5),
        'w_out': nrm((L, D_MIX, D_MODEL), D_MIX ** -0.5 * DN_BETA),
        'mix_norm_g': 1.0 + nrm((L, 3 * GROUP_W), 0.1),
        's5_lambda_re': -0.5 + nrm((L, G, P), 0.01),
        's5_lambda_im': math.pi * n + nrm((L, G, P), 0.01),
        's5_log_dt': jax.random.uniform(next(ks), (L, G), f32, math.log(1e-3), math.log(1e-1)),
        's5_b_re': nrm((L, G, P, H), (2 * H) ** -0.5),
        's5_b_im': nrm((L, G, P, H), (2 * H) ** -0.5),
        's5_c_re': nrm((L, G, H, P), (2 * P) ** -0.5),
        's5_c_im': nrm((L, G, H, P), (2 * P) ** -0.5),
        's5_d': nrm((L, GROUP_W), 1.0),
        's5_w_glu': nrm((L, GROUP_W, GROUP_W), GROUP_W ** -0.5),
        'conv_w': nrm((L, CONV_W, GROUP_W), CONV_W ** -0.5),
        'sgu_ln_g': 1.0 + nrm((L, GROUP_W), 0.1),
        'sgu_ln_b': nrm((L, GROUP_W), 0.02),
        'sgu_w': nrm((L, SGU_HEADS, SGU_BLK, SGU_BLK), SGU_BLK ** -0.5),
        'sgu_b': 1.0 + nrm((L, SGU_HEADS, SGU_BLK), 0.1),
        'diff_lq1': nrm((L, DIFF_QK_DIM), 0.1),
        'diff_lk1': nrm((L, DIFF_QK_DIM), 0.1),
        'diff_lq2': nrm((L, DIFF_QK_DIM), 0.1),
        'diff_lk2': nrm((L, DIFF_QK_DIM), 0.1),
        'diff_subln_g': 1.0 + nrm((L, DIFF_V_DIM), 0.1),
        'rel_bias': nrm((NUM_BUCKETS, DIFF_HEADS), 0.5),
        'ln1_g': 1.0 + nrm((L, D_MODEL), 0.1),
        'ln1_b': nrm((L, D_MODEL), 0.02),
        'router_w': nrm((L, D_MODEL, N_EXPERTS), D_MODEL ** -0.5),
        'router_bias': nrm((L, N_EXPERTS), 0.01),
        'moe_w_gu': nrm((L, N_EXPERTS, D_MODEL, 2 * D_EXPERT), D_MODEL ** -0.5),
        'moe_w_down': nrm((L, N_EXPERTS, D_EXPERT, D_MODEL), D_EXPERT ** -0.5 * DN_BETA),
        'shared_w_gu': nrm((L, D_MODEL, 2 * D_SHARED), D_MODEL ** -0.5),
        'shared_w_down': nrm((L, D_SHARED, D_MODEL), D_SHARED ** -0.5 * DN_BETA),
        'ln2_g': 1.0 + nrm((L, D_MODEL), 0.1),
        'ln2_b': nrm((L, D_MODEL), 0.02),
    }


def reference(x, w_in, w_out, mix_norm_g, s5_lambda_re, s5_lambda_im, s5_log_dt, s5_b_re, s5_b_im,
              s5_c_re, s5_c_im, s5_d, s5_w_glu, conv_w, sgu_ln_g, sgu_ln_b, sgu_w, sgu_b,
              diff_lq1, diff_lk1, diff_lq2, diff_lk2, diff_subln_g, rel_bias, ln1_g, ln1_b,
              router_w, router_bias, moe_w_gu, moe_w_down, shared_w_gu, shared_w_down, ln2_g, ln2_b):
    Bt, L, _ = x.shape
    split_points = [GROUP_W * i for i in range(1, 9)]
    for l in range(DEPTH):
        proj = x @ w_in[l]
        s5_u, cb, cc, ch, su, sv, q, k, v = jnp.split(proj, split_points, axis=-1)
        a_out = s5_mixer(s5_u, s5_lambda_re[l], s5_lambda_im[l], s5_log_dt[l], s5_b_re[l], s5_b_im[l],
                         s5_c_re[l], s5_c_im[l], s5_d[l], s5_w_glu[l])
        b_out = short_conv_mixer(cb, cc, ch, conv_w[l])
        c_out = sgu_mixer(su, sv, sgu_ln_g[l], sgu_ln_b[l], sgu_w[l], sgu_b[l])
        lambda_init = 0.8 - 0.6 * math.exp(-0.3 * l)
        d_out = diff_attention(q.reshape(Bt, L, DIFF_HEADS, 2, DIFF_QK_DIM),
                               k.reshape(Bt, L, DIFF_HEADS, 2, DIFF_QK_DIM),
                               v.reshape(Bt, L, DIFF_HEADS, DIFF_V_DIM),
                               diff_lq1[l], diff_lk1[l], diff_lq2[l], diff_lk2[l], diff_subln_g[l],
                               rel_bias, lambda_init)
        abc = jnp.concatenate([a_out, b_out, c_out], axis=-1).reshape(Bt, L, 3, GROUP_W)
        abc = rms_norm(abc, mix_norm_g[l].reshape(3, GROUP_W)).reshape(Bt, L, 3 * GROUP_W)
        mix = jnp.concatenate([abc, d_out], axis=-1) @ w_out[l]
        x = layer_norm(DN_ALPHA * x + mix, ln1_g[l], ln1_b[l])
        ffn = moe_ffn(x, router_w[l], router_bias[l], moe_w_gu[l], moe_w_down[l],
                      shared_w_gu[l], shared_w_down[l])
        x = layer_norm(DN_ALPHA * x + ffn, ln2_g[l], ln2_b[l])
    return x
```

```python
import functools
import math

import jax
import jax.numpy as jnp
from jax import lax
from jax.experimental import pallas as pl
from jax.experimental.pallas import tpu as pltpu

F32 = jnp.float32
BF16 = jnp.bfloat16

GROUP_W = 512
CHUNK = 64
S5_GROUP_CH = 16
S5_GROUPS = 32
S5_STATE = 64
S5_SUB = 16
SGU_BLK = 128
SGU_HEADS = 4
DIFF_HEADS = 4
DIFF_QK_DIM = 64
DIFF_V_DIM = 128
NUM_BUCKETS = 32
MAX_DISTANCE = 128
N_EXPERTS = 64
TOP_K = 8
N_EXPERT_GROUPS = 8
TOPK_GROUPS = 4
ROUTED_SCALE = 2.5
EPS = 1e-5
NEG_INF = -1e30

VMEM_LIMIT = 56 * 1024 * 1024


def _cparams(sem):
    return pltpu.CompilerParams(dimension_semantics=sem, vmem_limit_bytes=VMEM_LIMIT)


def _rms(x, g):
    return x * lax.rsqrt(jnp.mean(jnp.square(x), -1, keepdims=True) + EPS) * g


def _ln(x, g, b):
    mu = jnp.mean(x, -1, keepdims=True)
    var = jnp.mean(jnp.square(x - mu), -1, keepdims=True)
    return (x - mu) * lax.rsqrt(var + EPS) * g + b


def _matmul_kernel(x_ref, w_ref, o_ref):
    o_ref[...] = jnp.dot(x_ref[...], w_ref[...], preferred_element_type=F32).astype(o_ref.dtype)


def _matmul(x, w, tm, tn, out_dtype):
    M, K = x.shape
    N = w.shape[1]
    return pl.pallas_call(
        _matmul_kernel,
        grid=(M // tm, N // tn),
        in_specs=[pl.BlockSpec((tm, K), lambda i, j: (i, 0)),
                  pl.BlockSpec((K, tn), lambda i, j: (0, j))],
        out_specs=pl.BlockSpec((tm, tn), lambda i, j: (i, j)),
        out_shape=jax.ShapeDtypeStruct((M, N), out_dtype),
        compiler_params=_cparams(("parallel", "arbitrary")),
        name="proj_matmul",
    )(x, w)


def _s5_tables(lam_re, lam_im, log_dt, b_re, b_im, c_re, c_im, n_rows):
    G, P, H, S = S5_GROUPS, S5_STATE, S5_GROUP_CH, S5_SUB
    hp = lax.Precision.HIGHEST
    dt = jnp.exp(log_dt.astype(F32))[:, None]
    lam = lax.complex(lam_re.astype(F32), lam_im.astype(F32))
    ldt = lam * dt
    lam_bar = jnp.exp(ldt)
    b_bar = ((lam_bar - 1.0) / lam)[..., None] * lax.complex(b_re.astype(F32), b_im.astype(F32))
    c = lax.complex(c_re.astype(F32), c_im.astype(F32))
    tau = jnp.arange(S + 1, dtype=F32)
    pows = jnp.exp(ldt[None] * tau[:, None, None])
    w1 = pows[:S][::-1][:, :, None, :] * jnp.transpose(b_bar, (0, 2, 1))[None]
    w1 = jnp.transpose(w1, (1, 0, 2, 3)).reshape(G, S * H, P)
    w2 = jnp.transpose(c, (0, 2, 1))[:, :, None, :] * jnp.transpose(pows[1:], (1, 2, 0))[..., None]
    w2 = w2.reshape(G, P, S * H)
    kc = jnp.real(jnp.einsum('ghp,tgp,gpi->tghi', c, pows[:S], b_bar, precision=hp))
    jj = jnp.arange(S)[:, None]
    tt = jnp.arange(S)[None, :]
    kl = kc[jnp.clip(tt - jj, 0, S - 1)]
    kl = jnp.where((tt >= jj)[:, :, None, None, None], kl, 0.0)
    d0 = jnp.transpose(kl, (2, 0, 4, 1, 3)).reshape(G, S * H, S * H)

    def pair_rows(a):
        a = a.reshape(G // 2, 2, a.shape[1], a.shape[2])
        z = jnp.zeros_like(a[:, 0])
        top = jnp.concatenate([a[:, 0], z], axis=2)
        bot = jnp.concatenate([z, a[:, 1]], axis=2)
        return jnp.concatenate([top, bot], axis=1)

    nstep = max(1, (n_rows - 1).bit_length())
    kk = (S * (2 ** jnp.arange(nstep))).astype(F32)
    lp = jnp.exp(ldt[None] * kk[:, None, None])
    lp = jnp.transpose(lp, (1, 0, 2)).reshape(G // 2, 2, nstep, P)
    lp = jnp.concatenate([lp[:, 0], lp[:, 1]], axis=-1)
    lampow = jnp.stack([jnp.real(lp), jnp.imag(lp)], axis=2)
    return dict(
        d0=d0.reshape(G // 2, 2, S * H, S * H).astype(BF16),
        w1re=pair_rows(jnp.real(w1)).astype(BF16), w1im=pair_rows(jnp.imag(w1)).astype(BF16),
        w2re=pair_rows(jnp.real(w2)).astype(BF16), w2im=pair_rows(-jnp.imag(w2)).astype(BF16),
        lampow=lampow.astype(F32))


def _s5_kernel(u_ref, d0_ref, w1re_ref, w1im_ref, w2re_ref, w2im_ref, lp_ref, o_ref, *, nstep):
    u = u_ref[0]
    R = u.shape[0]
    half = S5_SUB * S5_GROUP_CH
    y0 = jnp.dot(u[:, :half], d0_ref[0, 0], preferred_element_type=F32)
    y1 = jnp.dot(u[:, half:], d0_ref[0, 1], preferred_element_type=F32)
    xre = jnp.dot(u, w1re_ref[0], preferred_element_type=F32)
    xim = jnp.dot(u, w1im_ref[0], preferred_element_type=F32)
    row = lax.broadcasted_iota(jnp.int32, xre.shape, 0)
    for k in range(nstep):
        sh = 1 << k
        if sh >= R:
            break
        pre = pltpu.roll(xre, sh, 0)
        pim = pltpu.roll(xim, sh, 0)
        lr = lp_ref[0, k, 0:1, :]
        li = lp_ref[0, k, 1:2, :]
        keep = row >= sh
        xre, xim = (xre + jnp.where(keep, lr * pre - li * pim, 0.0),
                    xim + jnp.where(keep, lr * pim + li * pre, 0.0))
    sre = jnp.where(row >= 1, pltpu.roll(xre, 1, 0), 0.0).astype(BF16)
    sim = jnp.where(row >= 1, pltpu.roll(xim, 1, 0), 0.0).astype(BF16)
    yc = (jnp.dot(sre, w2re_ref[0], preferred_element_type=F32)
          + jnp.dot(sim, w2im_ref[0], preferred_element_type=F32))
    o_ref[0, :, :half] = y0 + yc[:, :half]
    o_ref[0, :, half:] = y1 + yc[:, half:]


def _s5_scan(u, tabs):
    L = u.shape[0]
    G, H, S = S5_GROUPS, S5_GROUP_CH, S5_SUB
    R = L // S
    W = 2 * S * H
    u16 = jnp.transpose(u.astype(BF16).reshape(R, S, G // 2, 2, H), (2, 0, 3, 1, 4)).reshape(G // 2, R, W)
    nstep = tabs['lampow'].shape[1]
    P2 = 2 * S5_STATE
    y16 = pl.pallas_call(
        functools.partial(_s5_kernel, nstep=nstep),
        grid=(G // 2,),
        in_specs=[pl.BlockSpec((1, R, W), lambda g: (g, 0, 0)),
                  pl.BlockSpec((1, 2, S * H, S * H), lambda g: (g, 0, 0, 0)),
                  pl.BlockSpec((1, W, P2), lambda g: (g, 0, 0)),
                  pl.BlockSpec((1, W, P2), lambda g: (g, 0, 0)),
                  pl.BlockSpec((1, P2, W), lambda g: (g, 0, 0)),
                  pl.BlockSpec((1, P2, W), lambda g: (g, 0, 0)),
                  pl.BlockSpec((1, nstep, 2, P2), lambda g: (g, 0, 0, 0))],
        out_specs=pl.BlockSpec((1, R, W), lambda g: (g, 0, 0)),
        out_shape=jax.ShapeDtypeStruct((G // 2, R, W), F32),
        compiler_params=_cparams(("parallel",)),
        name="s5_scan",
    )(u16, tabs['d0'], tabs['w1re'], tabs['w1im'], tabs['w2re'], tabs['w2im'], tabs['lampow'])
    return jnp.transpose(y16.reshape(G // 2, R, 2, S, H), (1, 3, 0, 2, 4)).reshape(L, G * H)


def _mixers_kernel(s5u_ref, cb_ref, cc_ref, ch_ref, su_ref, sv_ref, cch_ref, chh_ref, ys_ref,
                   d_ref, wglu_ref, cw_ref, lng_ref, lnb_ref, ws_ref, bs_ref, g_ref, o_ref):
    i = pl.program_id(0)
    tm = o_ref.shape[0]
    gw = GROUP_W
    y = ys_ref[...] + d_ref[...] * s5u_ref[...]
    y = jax.nn.gelu(y)
    y = y * jax.nn.sigmoid(jnp.dot(y.astype(BF16), wglu_ref[...], preferred_element_type=F32))
    o_ref[:, 0:gw] = _rms(y, g_ref[0:1, :]).astype(o_ref.dtype)
    z = cc_ref[...] * ch_ref[...]
    zh = jnp.where(i > 0, cch_ref[...] * chh_ref[...], 0.0)
    row = lax.broadcasted_iota(jnp.int32, z.shape, 0)
    z1 = jnp.where(row == 0, zh[7:8, :], pltpu.roll(z, 1, 0))
    z2 = jnp.where(row == 0, zh[6:7, :], jnp.where(row == 1, zh[7:8, :], pltpu.roll(z, 2, 0)))
    conv = cw_ref[0:1, :] * z2 + cw_ref[1:2, :] * z1 + cw_ref[2:3, :] * z
    o_ref[:, gw:2 * gw] = _rms(cb_ref[...] * conv, g_ref[1:2, :]).astype(o_ref.dtype)
    uu = jax.nn.gelu(su_ref[...])
    vv = _ln(jax.nn.gelu(sv_ref[...]), lng_ref[...], lnb_ref[...]).astype(BF16)
    pi = lax.broadcasted_iota(jnp.int32, (SGU_BLK, SGU_BLK), 0)
    pj = lax.broadcasted_iota(jnp.int32, (SGU_BLK, SGU_BLK), 1)
    causal = (pj // CHUNK) <= (pi // CHUNK)
    hd = gw // SGU_HEADS
    ws = [jnp.where(causal, ws_ref[h], 0.0).astype(BF16) for h in range(SGU_HEADS)]
    blocks = []
    for n in range(tm // SGU_BLK):
        vb = vv[n * SGU_BLK:(n + 1) * SGU_BLK, :]
        blocks.append(jnp.concatenate(
            [jnp.dot(ws[h], vb[:, h * hd:(h + 1) * hd], preferred_element_type=F32) for h in range(SGU_HEADS)],
            axis=1) + bs_ref[...])
    mixed = jnp.concatenate(blocks, axis=0)
    o_ref[:, 2 * gw:3 * gw] = _rms(uu * mixed, g_ref[2:3, :]).astype(o_ref.dtype)


def _mixers(proj_a, ys5, s5_d, w_glu, conv_w, ln_g, ln_b, sgu_w, sgu_b, mix_g, tm):
    L = proj_a.shape[0]
    gw = GROUP_W
    hb = tm // 8
    col = lambda c: pl.BlockSpec((tm, gw), lambda i, c=c: (i, c))
    halo = lambda c: pl.BlockSpec((8, gw), lambda i, c=c: (jnp.maximum(i * hb - 1, 0), c))
    full = lambda a: pl.BlockSpec(a.shape, lambda i: (0,) * a.ndim)
    hd = gw // SGU_HEADS
    bs_full = jnp.repeat(sgu_b.astype(F32).T, hd, axis=1)
    consts = [s5_d.reshape(1, gw).astype(F32), w_glu.astype(BF16), conv_w.astype(F32),
              ln_g.reshape(1, gw).astype(F32), ln_b.reshape(1, gw).astype(F32), sgu_w.astype(F32),
              bs_full, mix_g.reshape(3, gw).astype(F32)]
    return pl.pallas_call(
        _mixers_kernel,
        grid=(L // tm,),
        in_specs=[col(0), col(1), col(2), col(3), col(4), col(5), halo(2), halo(3),
                  pl.BlockSpec((tm, gw), lambda i: (i, 0))] + [full(a) for a in consts],
        out_specs=pl.BlockSpec((tm, 3 * gw), lambda i: (i, 0)),
        out_shape=jax.ShapeDtypeStruct((L, 3 * gw), BF16),
        compiler_params=_cparams(("parallel",)),
        name="row_mixers",
    )(proj_a, proj_a, proj_a, proj_a, proj_a, proj_a, proj_a, proj_a, ys5, *consts)


def _t5_bucket(rel):
    half = NUM_BUCKETS // 2
    ret = jnp.where(rel > 0, half, 0)
    n = jnp.abs(rel)
    max_exact = half // 2
    large = max_exact + (jnp.log(jnp.maximum(n, 1).astype(F32) / max_exact)
                         / math.log(MAX_DISTANCE / max_exact) * (half - max_exact)).astype(jnp.int32)
    large = jnp.minimum(large, half - 1)
    return ret + jnp.where(n < max_exact, n, large)


def _attn_bias_tables(rel_bias, tq):
    assert tq >= MAX_DISTANCE
    far = rel_bias[NUM_BUCKETS // 2 - 1].astype(F32)
    qi = jnp.arange(tq)[:, None]
    kj = jnp.arange(tq)[None, :]
    diag = rel_bias[_t5_bucket(kj - qi)].astype(F32) - far
    diag = jnp.where(((kj // CHUNK) <= (qi // CHUNK))[..., None], diag, NEG_INF)
    prev = rel_bias[_t5_bucket(kj - tq - qi)].astype(F32) - far
    tabs = jnp.stack([jnp.transpose(diag, (2, 0, 1)), jnp.transpose(prev, (2, 0, 1))], axis=1)
    return jnp.concatenate([tabs, tabs], axis=2)


def _attn_kernel(q_ref, k_ref, v_ref, nb_ref, lq1_ref, lk1_ref, lq2_ref, lk2_ref, g_ref, o_ref, *, lambda_init):
    i = pl.program_id(1)
    tq = q_ref.shape[0]
    dq = DIFF_QK_DIM
    q = q_ref[...] * jnp.asarray(dq ** -0.5, q_ref.dtype)
    lane = lax.broadcasted_iota(jnp.int32, q.shape, 1)
    zero = jnp.zeros_like(q)
    qq = jnp.concatenate([jnp.where(lane < dq, q, zero), jnp.where(lane >= dq, q, zero)], axis=0)

    def scores(j):
        kb = k_ref[pl.ds(pl.multiple_of(j * tq, tq), tq), :]
        return lax.dot_general(qq, kb, (((1,), (1,)), ((), ())), preferred_element_type=F32)

    def update(carry, s, j):
        m, l, acc = carry
        vb = v_ref[pl.ds(pl.multiple_of(j * tq, tq), tq), :]
        m_new = jnp.maximum(m, jnp.max(s, axis=-1, keepdims=True))
        alpha = jnp.exp(m - m_new)
        p = jnp.exp(s - m_new)
        l = alpha * l + jnp.sum(p, axis=-1, keepdims=True)
        acc = alpha * acc + jnp.dot(p.astype(BF16), vb, preferred_element_type=F32)
        return m_new, l, acc

    carry = (jnp.full((2 * tq, 1), -jnp.inf, F32), jnp.zeros((2 * tq, 1), F32),
             jnp.zeros((2 * tq, DIFF_V_DIM), F32))
    carry = update(carry, scores(i) + nb_ref[0, 0], i)
    jp = jnp.maximum(i - 1, 0)
    carry = update(carry, scores(jp) + nb_ref[0, 1] + jnp.where(i > 0, 0.0, NEG_INF), jp)
    carry = lax.fori_loop(0, jnp.maximum(i - 1, 0), lambda j, c: update(c, scores(j), j), carry)
    m, l, acc = carry
    o = acc / l
    lam = (jnp.exp(jnp.sum(lq1_ref[...] * lk1_ref[...], keepdims=True))
           - jnp.exp(jnp.sum(lq2_ref[...] * lk2_ref[...], keepdims=True)) + lambda_init)
    out = o[:tq] - lam * o[tq:]
    o_ref[...] = (_rms(out, g_ref[...]) * (1.0 - lambda_init)).astype(o_ref.dtype)


def _diff_attention(qkv, nb, lq1, lk1, lq2, lk2, subln_g, lambda_init, tq):
    L = qkv.shape[0]
    H, dv = DIFF_HEADS, DIFF_V_DIM
    vec = lambda a: a.reshape(1, -1).astype(F32)
    small = lambda n: pl.BlockSpec((1, n), lambda h, i: (0, 0))
    return pl.pallas_call(
        functools.partial(_attn_kernel, lambda_init=lambda_init),
        grid=(H, L // tq),
        in_specs=[pl.BlockSpec((tq, dv), lambda h, i: (i, h)),
                  pl.BlockSpec((L, dv), lambda h, i: (0, H + h)),
                  pl.BlockSpec((L, dv), lambda h, i: (0, 2 * H + h)),
                  pl.BlockSpec((1, 2, 2 * tq, tq), lambda h, i: (h, 0, 0, 0)),
                  small(DIFF_QK_DIM), small(DIFF_QK_DIM), small(DIFF_QK_DIM), small(DIFF_QK_DIM), small(dv)],
        out_specs=pl.BlockSpec((tq, dv), lambda h, i: (i, h)),
        out_shape=jax.ShapeDtypeStruct((L, H * dv), BF16),
        compiler_params=_cparams(("parallel", "arbitrary")),
        name="diff_attention",
    )(qkv, qkv, qkv, nb, vec(lq1), vec(lk1), vec(lq2), vec(lk2), vec(subln_g))


def _outproj_kernel(abc_ref, d_ref, x_ref, wa_ref, wd_ref, g_ref, b_ref, rhi_ref, rlo_ref,
                    x1_ref, x1b_ref, lg_ref, *, alpha):
    mix = (jnp.dot(abc_ref[...], wa_ref[...], preferred_element_type=F32)
           + jnp.dot(d_ref[...], wd_ref[...], preferred_element_type=F32))
    x1 = _ln(alpha * x_ref[...] + mix, g_ref[...], b_ref[...])
    x1_ref[...] = x1
    hi = x1.astype(BF16)
    x1b_ref[...] = hi
    lo = (x1 - hi.astype(F32)).astype(BF16)
    nt = (((1,), (1,)), ((), ()))
    lg_ref[...] = (lax.dot_general(rhi_ref[...], hi, nt, preferred_element_type=F32)
                   + lax.dot_general(rhi_ref[...], lo, nt, preferred_element_type=F32)
                   + lax.dot_general(rlo_ref[...], hi, nt, preferred_element_type=F32))


def _outproj(abc, d_out, x, w_out, ln_g, ln_b, router_w, alpha, tm):
    L, D = x.shape
    E = router_w.shape[1]
    ka = abc.shape[1]
    kd = d_out.shape[1]
    rwt = router_w.astype(F32).T
    rhi = rwt.astype(BF16)
    rlo = (rwt - rhi.astype(F32)).astype(BF16)
    full = lambda shape: pl.BlockSpec(shape, lambda i: (0,) * len(shape))
    return pl.pallas_call(
        functools.partial(_outproj_kernel, alpha=alpha),
        grid=(L // tm,),
        in_specs=[pl.BlockSpec((tm, ka), lambda i: (i, 0)),
                  pl.BlockSpec((tm, kd), lambda i: (i, 0)),
                  pl.BlockSpec((tm, D), lambda i: (i, 0)),
                  pl.BlockSpec((ka, D), lambda i: (0, 0)),
                  pl.BlockSpec((kd, D), lambda i: (ka // kd, 0)),
                  full((1, D)), full((1, D)), full((E, D)), full((E, D))],
        out_specs=[pl.BlockSpec((tm, D), lambda i: (i, 0)),
                   pl.BlockSpec((tm, D), lambda i: (i, 0)),
                   pl.BlockSpec((E, tm), lambda i: (0, i))],
        out_shape=[jax.ShapeDtypeStruct((L, D), F32), jax.ShapeDtypeStruct((L, D), BF16),
                   jax.ShapeDtypeStruct((E, L), F32)],
        compiler_params=_cparams(("parallel",)),
        name="outproj_ln1",
    )(abc, d_out, x, w_out, w_out, ln_g.reshape(1, D).astype(F32), ln_b.reshape(1, D).astype(F32), rhi, rlo)


def _router_kernel(lg_ref, bias_ref, tri_ref, e_ref, r_ref, w_ref, cnt_ref, carry_ref):
    i = pl.program_id(0)
    E, tn = lg_ref.shape
    ng = N_EXPERT_GROUPS
    gs_ = E // ng

    @pl.when(i == 0)
    def _():
        carry_ref[...] = jnp.zeros_like(carry_ref)

    s = jax.nn.sigmoid(lg_ref[...])
    sel = s + bias_ref[...]
    midx = lax.broadcasted_iota(jnp.int32, (gs_, tn), 0).astype(F32)
    rows, gscore = [], []
    for g in range(ng):
        rg = sel[g * gs_:(g + 1) * gs_, :]
        m1 = jnp.max(rg, axis=0, keepdims=True)
        first = jnp.min(jnp.where(rg == m1, midx, float(gs_)), axis=0, keepdims=True)
        m2 = jnp.max(jnp.where(midx == first, -jnp.inf, rg), axis=0, keepdims=True)
        rows.append(rg)
        gscore.append(m1 + m2)
    vals = []
    for g in range(ng):
        rank = jnp.zeros((1, tn), F32)
        for o in range(ng):
            if o != g:
                beats = (gscore[o] >= gscore[g]) if o < g else (gscore[o] > gscore[g])
                rank = rank + jnp.where(beats, 1.0, 0.0)
        vals.append(jnp.where(rank < TOPK_GROUPS, rows[g], -jnp.inf))
    val = jnp.concatenate(vals, axis=0)
    eidx = lax.broadcasted_iota(jnp.int32, val.shape, 0)
    erank = jnp.zeros(val.shape, F32)
    for e in range(E):
        other = val[e:e + 1, :]
        erank = erank + jnp.where(eidx > e, jnp.where(other >= val, 1.0, 0.0), jnp.where(other > val, 1.0, 0.0))
    chosen = erank < TOP_K
    wsel = jnp.where(chosen, s, 0.0)
    wn = wsel / (jnp.sum(wsel, axis=0, keepdims=True) + 1e-20) * ROUTED_SCALE
    chf = jnp.where(chosen, 1.0, 0.0)
    incl = jnp.dot(chf.astype(BF16), tri_ref[...], preferred_element_type=F32)
    base = carry_ref[...]
    pos = base + incl - chf
    carry_ref[...] = base + incl[:, tn - 1:tn]
    cnt_ref[...] = (base + incl[:, tn - 1:tn]).astype(jnp.int32)
    eidf = eidx.astype(F32)
    cand = jnp.where(chosen, eidf, float(E))
    for k in range(TOP_K):
        ek = jnp.min(cand, axis=0, keepdims=True)
        hit = cand == ek
        e_ref[k:k + 1, :] = ek.astype(jnp.int32)
        r_ref[k:k + 1, :] = jnp.sum(jnp.where(hit, pos, 0.0), axis=0, keepdims=True).astype(jnp.int32)
        w_ref[k:k + 1, :] = jnp.sum(jnp.where(hit, wn, 0.0), axis=0, keepdims=True)
        cand = jnp.where(hit, float(E), cand)


def _router(logits_t, router_bias, tn):
    E, L = logits_t.shape
    tri = (jnp.arange(tn)[:, None] <= jnp.arange(tn)[None, :]).astype(BF16)
    slot = lambda dt: jax.ShapeDtypeStruct((TOP_K, L), dt)
    return pl.pallas_call(
        _router_kernel,
        grid=(L // tn,),
        in_specs=[pl.BlockSpec((E, tn), lambda i: (0, i)),
                  pl.BlockSpec((E, 1), lambda i: (0, 0)),
                  pl.BlockSpec((tn, tn), lambda i: (0, 0))],
        out_specs=[pl.BlockSpec((TOP_K, tn), lambda i: (0, i)),
                   pl.BlockSpec((TOP_K, tn), lambda i: (0, i)),
                   pl.BlockSpec((TOP_K, tn), lambda i: (0, i)),
                   pl.BlockSpec((E, 1), lambda i: (0, 0))],
        out_shape=[slot(jnp.int32), slot(jnp.int32), slot(F32), jax.ShapeDtypeStruct((E, 1), jnp.int32)],
        scratch_shapes=[pltpu.VMEM((E, 1), F32)],
        compiler_params=_cparams(("arbitrary",)),
        name="router_topk",
    )(logits_t, router_bias.reshape(E, 1).astype(F32), tri)


def _dispatch_kernel(dest_ref, pad_ref, x_ref, xs_ref, zero_ref, sem, zsem):
    i = pl.program_id(0)
    tm = x_ref.shape[0]

    def row_copy(r, k):
        return pltpu.make_async_copy(x_ref.at[pl.ds(r, 1), :], xs_ref.at[pl.ds(dest_ref[k, r], 1), :], sem)

    def zero_copy(dst):
        return pltpu.make_async_copy(zero_ref, xs_ref.at[pl.ds(dst, 1), :], zsem)

    @pl.when(i == 0)
    def _():
        zero_ref[...] = jnp.zeros_like(zero_ref)

        def per_expert(e, c):
            lo, hi = pad_ref[0, e], pad_ref[1, e]
            lax.fori_loop(lo, hi, lambda d, c2: (zero_copy(d).start(), c2)[1], 0)
            lax.fori_loop(lo, hi, lambda d, c2: (zero_copy(d).wait(), c2)[1], 0)
            return c
        lax.fori_loop(0, N_EXPERTS, per_expert, 0)

    def issue(r, c):
        for k in range(TOP_K):
            row_copy(r, k).start()
        return c
    lax.fori_loop(0, tm, issue, 0)

    def drain(r, c):
        for k in range(TOP_K):
            row_copy(r, k).wait()
        return c
    lax.fori_loop(0, tm, drain, 0)


def _dispatch(x1, dest, pad_lo_hi, n_slots, tm):
    L, D = x1.shape
    return pl.pallas_call(
        _dispatch_kernel,
        grid=(L // tm,),
        in_specs=[pl.BlockSpec((TOP_K, tm), lambda i: (0, i), memory_space=pltpu.SMEM),
                  pl.BlockSpec(memory_space=pltpu.SMEM),
                  pl.BlockSpec((tm, D), lambda i: (i, 0))],
        out_specs=pl.BlockSpec(memory_space=pl.ANY),
        out_shape=jax.ShapeDtypeStruct((n_slots, D), x1.dtype),
        scratch_shapes=[pltpu.VMEM((1, D), x1.dtype), pltpu.SemaphoreType.DMA(()), pltpu.SemaphoreType.DMA(())],
        compiler_params=_cparams(("arbitrary",)),
        name="moe_dispatch",
    )(dest, pad_lo_hi, x1)


def _experts_kernel(be_ref, nb_ref, xs_ref, wgu_ref, wd_ref, y_ref):
    b = pl.program_id(0)

    @pl.when(b < nb_ref[0])
    def _():
        de = wd_ref.shape[1]
        h = jnp.dot(xs_ref[...].astype(BF16), wgu_ref[0], preferred_element_type=F32)
        a = jax.nn.silu(h[:, :de]) * h[:, de:]
        y_ref[...] = jnp.dot(a.astype(BF16), wd_ref[0], preferred_element_type=F32).astype(y_ref.dtype)


def _experts(xs, block_e, n_used, w_gu, w_down, blk):
    n_slots, D = xs.shape
    nblk = n_slots // blk
    E, _, de2 = w_gu.shape
    de = w_down.shape[1]
    last = lambda b, be, nb: jnp.minimum(b, nb[0] - 1)
    return pl.pallas_call(
        _experts_kernel,
        grid_spec=pltpu.PrefetchScalarGridSpec(
            num_scalar_prefetch=2,
            grid=(nblk,),
            in_specs=[pl.BlockSpec((blk, D), lambda b, be, nb: (last(b, be, nb), 0)),
                      pl.BlockSpec((1, D, de2), lambda b, be, nb: (be[b], 0, 0)),
                      pl.BlockSpec((1, de, D), lambda b, be, nb: (be[b], 0, 0))],
            out_specs=pl.BlockSpec((blk, D), lambda b, be, nb: (last(b, be, nb), 0))),
        out_shape=jax.ShapeDtypeStruct((n_slots, D), F32),
        compiler_params=_cparams(("arbitrary",)),
        name="moe_experts",
    )(block_e, n_used, xs, w_gu, w_down)


def _combine_kernel(dcur_ref, dnxt_ref, y_ref, w_ref, x1_ref, x1b_ref, sgu_ref, sdn_ref, g_ref, b_ref,
                    x2_ref, x2b_ref, buf, sem, *, alpha):
    i = pl.program_id(0)
    n = pl.num_programs(0)
    tm = x1_ref.shape[0]
    slot = i % 2

    def row_copy(d_ref, s, r, k):
        return pltpu.make_async_copy(y_ref.at[pl.ds(d_ref[k, r], 1), :],
                                     buf.at[s, k, pl.ds(r, 1), :], sem.at[s])

    def issue(d_ref, s):
        def body(r, c):
            for k in range(TOP_K):
                row_copy(d_ref, s, r, k).start()
            return c
        lax.fori_loop(0, tm, body, 0)

    @pl.when(i == 0)
    def _():
        issue(dcur_ref, 0)

    @pl.when(i + 1 < n)
    def _():
        issue(dnxt_ref, 1 - slot)

    de = sdn_ref.shape[0]
    h = jnp.dot(x1b_ref[...], sgu_ref[...], preferred_element_type=F32)
    a = jax.nn.silu(h[:, :de]) * h[:, de:]
    ffn = jnp.dot(a.astype(BF16), sdn_ref[...], preferred_element_type=F32)

    def drain(r, c):
        for k in range(TOP_K):
            row_copy(dcur_ref, slot, r, k).wait()
        return c
    lax.fori_loop(0, tm, drain, 0)

    for k in range(TOP_K):
        ffn = ffn + buf[slot, k] * w_ref[:, k:k + 1]
    x2 = _ln(alpha * x1_ref[...] + ffn, g_ref[...], b_ref[...])
    x2_ref[...] = x2
    x2b_ref[...] = x2.astype(BF16)


def _combine(dest, y, w_t, x1, x1b, sh_gu, sh_down, ln_g, ln_b, alpha, tm):
    L, D = x1.shape
    n = L // tm
    full = lambda shape: pl.BlockSpec(shape, lambda i: (0,) * len(shape))
    return pl.pallas_call(
        functools.partial(_combine_kernel, alpha=alpha),
        grid=(n,),
        in_specs=[pl.BlockSpec((TOP_K, tm), lambda i: (0, i), memory_space=pltpu.SMEM),
                  pl.BlockSpec((TOP_K, tm), lambda i: (0, jnp.minimum(i + 1, n - 1)), memory_space=pltpu.SMEM),
                  pl.BlockSpec(memory_space=pl.ANY),
                  pl.BlockSpec((tm, TOP_K), lambda i: (i, 0)),
                  pl.BlockSpec((tm, D), lambda i: (i, 0)),
                  pl.BlockSpec((tm, D), lambda i: (i, 0)),
                  full(sh_gu.shape), full(sh_down.shape), full((1, D)), full((1, D))],
        out_specs=[pl.BlockSpec((tm, D), lambda i: (i, 0)), pl.BlockSpec((tm, D), lambda i: (i, 0))],
        out_shape=[jax.ShapeDtypeStruct((L, D), F32), jax.ShapeDtypeStruct((L, D), BF16)],
        scratch_shapes=[pltpu.VMEM((2, TOP_K, tm, D), y.dtype), pltpu.SemaphoreType.DMA((2,))],
        compiler_params=_cparams(("arbitrary",)),
        name="moe_combine_ln2",
    )(dest, dest, y, w_t, x1, x1b, sh_gu, sh_down, ln_g.reshape(1, D).astype(F32), ln_b.reshape(1, D).astype(F32))


MOE_BLK = 256


def _moe_layer(x1, x1b, logits_t, router_bias, w_gu, w_down, sh_gu, sh_down, ln_g, ln_b, alpha,
               router_tn, dispatch_tm, combine_tm):
    L, D = x1.shape
    E = N_EXPERTS
    e_k, r_k, w_k, counts = _router(logits_t, router_bias, router_tn)
    counts = counts.reshape(E)
    padded = (counts + MOE_BLK - 1) // MOE_BLK * MOE_BLK
    pends = jnp.cumsum(padded)
    pstarts = pends - padded
    dest = pstarts[e_k] + r_k
    nblk = -(-(L * TOP_K) // MOE_BLK) + E
    n_used = (pends[-1] // MOE_BLK).astype(jnp.int32)
    blocks = jnp.minimum(jnp.arange(nblk, dtype=jnp.int32), n_used - 1)
    block_e = jnp.minimum(jnp.searchsorted(pends, blocks * MOE_BLK, side='right'), E - 1).astype(jnp.int32)
    pad_lo_hi = jnp.stack([pstarts + counts, pends]).astype(jnp.int32)
    xs = _dispatch(x1, dest, pad_lo_hi, nblk * MOE_BLK, dispatch_tm)
    y = _experts(xs, block_e, n_used.reshape(1), w_gu, w_down, MOE_BLK)
    return _combine(dest, y, w_k.T, x1, x1b, sh_gu, sh_down, ln_g, ln_b, alpha, combine_tm)


def _pick(n, pref):
    t = min(n, pref)
    assert n % t == 0
    return t


def kernel(x, w_in, w_out, mix_norm_g, s5_lambda_re, s5_lambda_im, s5_log_dt, s5_b_re, s5_b_im, s5_c_re, s5_c_im, s5_d, s5_w_glu, conv_w, sgu_ln_g, sgu_ln_b, sgu_w, sgu_b, diff_lq1, diff_lk1, diff_lq2, diff_lk2, diff_subln_g, rel_bias, ln1_g, ln1_b, router_w, router_bias, moe_w_gu, moe_w_down, shared_w_gu, shared_w_down, ln2_g, ln2_b):
    Bt, L, D = x.shape
    assert Bt == 1
    depth = w_in.shape[0]
    alpha = (2 * depth) ** 0.25
    gw = GROUP_W
    tq = _pick(L, 256)
    nb = _attn_bias_tables(rel_bias, tq)
    xf = x.reshape(L, D)
    xb = xf.astype(BF16)
    for l in range(depth):
        w_in_b = w_in[l].astype(BF16)
        proj_a = _matmul(xb, w_in_b[:, :6 * gw], _pick(L, 1024), 512, F32)
        qkv = _matmul(xb, w_in_b[:, 6 * gw:], _pick(L, 1024), 512, BF16)
        tabs = _s5_tables(s5_lambda_re[l], s5_lambda_im[l], s5_log_dt[l], s5_b_re[l], s5_b_im[l],
                          s5_c_re[l], s5_c_im[l], L // S5_SUB)
        ys5 = _s5_scan(proj_a[:, :gw], tabs)
        abc = _mixers(proj_a, ys5, s5_d[l], s5_w_glu[l], conv_w[l], sgu_ln_g[l], sgu_ln_b[l], sgu_w[l], sgu_b[l],
                      mix_norm_g[l], _pick(L, 512))
        lambda_init = 0.8 - 0.6 * math.exp(-0.3 * l)
        d_out = _diff_attention(qkv, nb, diff_lq1[l], diff_lk1[l], diff_lq2[l], diff_lk2[l], diff_subln_g[l],
                                lambda_init, tq)
        x1, x1b, logits_t = _outproj(abc, d_out, xf, w_out[l].astype(BF16), ln1_g[l], ln1_b[l], router_w[l],
                                     alpha, _pick(L, 256))
        xf, xb = _moe_layer(x1, x1b, logits_t, router_bias[l], moe_w_gu[l].astype(BF16),
                            moe_w_down[l].astype(BF16), shared_w_gu[l].astype(BF16), shared_w_down[l].astype(BF16),
                            ln2_g[l], ln2_b[l], alpha, _pick(L, 512), _pick(L, 256), _pick(L, 128))
    return xf.reshape(Bt, L, D)
```

```python
import functools
import math

import jax
import jax.numpy as jnp
from jax import lax
from jax.experimental import pallas as pl
from jax.experimental.pallas import tpu as pltpu

F32 = jnp.float32
BF16 = jnp.bfloat16

GROUP_W = 512
CHUNK = 64
S5_GROUP_CH = 16
S5_GROUPS = 32
S5_STATE = 64
S5_SUB = 16
S5_KBLOCKS = 4
SGU_BLK = 128
SGU_HEADS = 4
DIFF_HEADS = 4
DIFF_QK_DIM = 64
DIFF_V_DIM = 128
NUM_BUCKETS = 32
MAX_DISTANCE = 128
N_EXPERTS = 64
TOP_K = 8
N_EXPERT_GROUPS = 8
TOPK_GROUPS = 4
ROUTED_SCALE = 2.5
EPS = 1e-5
NEG_INF = -1e30
LOG2E = math.log2(math.e)

VMEM_LIMIT = 56 * 1024 * 1024


def _cparams(sem):
    return pltpu.CompilerParams(dimension_semantics=sem, vmem_limit_bytes=VMEM_LIMIT)


def _rms(x, g):
    return x * lax.rsqrt(jnp.mean(jnp.square(x), -1, keepdims=True) + EPS) * g


def _ln(x, g, b):
    mu = jnp.mean(x, -1, keepdims=True)
    var = jnp.mean(jnp.square(x - mu), -1, keepdims=True)
    return (x - mu) * lax.rsqrt(var + EPS) * g + b


def _matmul_kernel(x_ref, w_ref, o_ref):
    o_ref[...] = jnp.dot(x_ref[...], w_ref[...], preferred_element_type=F32).astype(o_ref.dtype)


def _matmul(x, w, tm, tn, out_dtype):
    M, K = x.shape
    N = w.shape[1]
    return pl.pallas_call(
        _matmul_kernel,
        grid=(M // tm, N // tn),
        in_specs=[pl.BlockSpec((tm, K), lambda i, j: (i, 0)),
                  pl.BlockSpec((K, tn), lambda i, j: (0, j))],
        out_specs=pl.BlockSpec((tm, tn), lambda i, j: (i, j)),
        out_shape=jax.ShapeDtypeStruct((M, N), out_dtype),
        compiler_params=_cparams(("parallel", "arbitrary")),
        name="proj_matmul",
    )(x, w)


def _matmul_nt_kernel(w_ref, x_ref, o_ref):
    o_ref[0] = lax.dot_general(w_ref[...], x_ref[...], (((1,), (1,)), ((), ())),
                               preferred_element_type=F32).astype(o_ref.dtype)


def _matmul_nt(w_t, x, tm, out_dtype):
    M, K = x.shape
    N = w_t.shape[0]
    return pl.pallas_call(
        _matmul_nt_kernel,
        grid=(M // tm,),
        in_specs=[pl.BlockSpec((N, K), lambda i: (0, 0)),
                  pl.BlockSpec((tm, K), lambda i: (i, 0))],
        out_specs=pl.BlockSpec((1, N, tm), lambda i: (i, 0, 0)),
        out_shape=jax.ShapeDtypeStruct((M // tm, N, tm), out_dtype),
        compiler_params=_cparams(("parallel",)),
        name="proj_matmul_nt",
    )(w_t, x)


def _s5_tables(lam_re, lam_im, log_dt, b_re, b_im, c_re, c_im, n_rows):
    G, P, H, S = S5_GROUPS, S5_STATE, S5_GROUP_CH, S5_SUB
    hp = lax.Precision.HIGHEST
    dt = jnp.exp(log_dt.astype(F32))[:, None]
    lam = lax.complex(lam_re.astype(F32), lam_im.astype(F32))
    ldt = lam * dt
    lam_bar = jnp.exp(ldt)
    b_bar = ((lam_bar - 1.0) / lam)[..., None] * lax.complex(b_re.astype(F32), b_im.astype(F32))
    c = lax.complex(c_re.astype(F32), c_im.astype(F32))
    tau = jnp.arange(S + 1, dtype=F32)
    pows = jnp.exp(ldt[None] * tau[:, None, None])
    KB, GL = S5_KBLOCKS, S5_GROUPS // S5_KBLOCKS
    eye = jnp.eye(GL, dtype=bool)
    w1 = pows[:S][::-1][:, :, None, :] * jnp.transpose(b_bar, (0, 2, 1))[None]
    w1 = jnp.transpose(w1.reshape(S, KB, GL, H, P), (1, 0, 2, 3, 4))
    w1 = jnp.where(eye[None, None, :, None, :, None], w1[:, :, :, :, None, :], 0.0)
    w1 = w1.reshape(KB, S * GL * H, GL * P)
    w2 = jnp.transpose(c, (0, 2, 1))[:, :, None, :] * jnp.transpose(pows[1:], (1, 2, 0))[..., None]
    w2 = w2.reshape(KB, GL, P, S, H)
    w2 = jnp.where(eye[None, :, None, None, :, None], w2[:, :, :, :, None, :], 0.0)
    w2 = w2.reshape(KB, GL * P, S * GL * H)
    kc = jnp.real(jnp.einsum('ghp,tgp,gpi->tghi', c, pows[:S], b_bar, precision=hp))
    kc = jnp.transpose(kc.reshape(S, KB, GL, H, H), (1, 2, 4, 0, 3))
    kc = jnp.where(eye[None, :, None, None, :, None], kc[:, :, :, :, None, :], 0.0)
    kcat = kc.reshape(KB, GL * H, S * GL * H)

    nstep = max(1, (n_rows - 1).bit_length())
    kk = (S * (2 ** jnp.arange(nstep))).astype(F32)
    lp = jnp.exp(ldt[None] * kk[:, None, None])
    lp = jnp.transpose(lp.reshape(nstep, KB, GL * P), (1, 0, 2))
    lampow = jnp.stack([jnp.real(lp), jnp.imag(lp)], axis=2)
    return dict(
        kcat=kcat.astype(BF16),
        w1re=jnp.real(w1).astype(BF16), w1im=jnp.imag(w1).astype(BF16),
        w2re=jnp.real(w2).astype(BF16), w2im=(-jnp.imag(w2)).astype(BF16),
        lampow=lampow.astype(F32))


def _s5_kernel(u_ref, w1re_ref, w1im_ref, w2re_ref, w2im_ref, kcat_ref, lp_ref, o_ref,
               ucat_ref, yall_ref, carry_ref, *, nstep):
    t = pl.program_id(1)
    S = S5_SUB
    R = ucat_ref.shape[0]
    W = u_ref.shape[1]

    @pl.when(t == 0)
    def _():
        carry_ref[...] = jnp.zeros_like(carry_ref)

    for j in range(S):
        ucat_ref[:, j * W:(j + 1) * W] = u_ref[pl.ds(j, R, stride=S), :].astype(BF16)
    ucat = ucat_ref[...]
    xre = jnp.dot(ucat, w1re_ref[0], preferred_element_type=F32)
    xim = jnp.dot(ucat, w1im_ref[0], preferred_element_type=F32)
    row = lax.broadcasted_iota(jnp.int32, xre.shape, 0)
    cre, cim = carry_ref[0], carry_ref[1]
    lr, li = lp_ref[0, 0, 0:1, :], lp_ref[0, 0, 1:2, :]
    xre = xre + jnp.where(row == 0, lr * cre - li * cim, 0.0)
    xim = xim + jnp.where(row == 0, lr * cim + li * cre, 0.0)
    for k in range(nstep):
        sh = 1 << k
        pre = pltpu.roll(xre, sh, 0)
        pim = pltpu.roll(xim, sh, 0)
        lr, li = lp_ref[0, k, 0:1, :], lp_ref[0, k, 1:2, :]
        keep = row >= sh
        xre, xim = (xre + jnp.where(keep, lr * pre - li * pim, 0.0),
                    xim + jnp.where(keep, lr * pim + li * pre, 0.0))
    carry_ref[0] = xre[R - 1:R, :]
    carry_ref[1] = xim[R - 1:R, :]
    sre = jnp.where(row >= 1, pltpu.roll(xre, 1, 0), cre).astype(BF16)
    sim = jnp.where(row >= 1, pltpu.roll(xim, 1, 0), cim).astype(BF16)
    yall_ref[...] = (jnp.dot(sre, w2re_ref[0], preferred_element_type=F32)
                     + jnp.dot(sim, w2im_ref[0], preferred_element_type=F32))
    for j in range(S):
        yall_ref[:, j * W:] += jnp.dot(ucat_ref[:, j * W:(j + 1) * W], kcat_ref[0, :, :(S - j) * W],
                                       preferred_element_type=F32)
    for j in range(S):
        o_ref[pl.ds(j, R, stride=S), :] = yall_ref[:, j * W:(j + 1) * W]


def _s5_scan(proj_a, tabs, rt):
    L = proj_a.shape[0]
    S, KB = S5_SUB, S5_KBLOCKS
    W = GROUP_W // KB
    nstep = tabs['lampow'].shape[1]
    P2 = tabs['lampow'].shape[3]
    rows = rt * S
    kb3 = lambda a: pl.BlockSpec((1,) + a.shape[1:], lambda k, t: (k, 0, 0))
    return pl.pallas_call(
        functools.partial(_s5_kernel, nstep=nstep),
        grid=(KB, L // rows),
        in_specs=[pl.BlockSpec((rows, W), lambda k, t: (t, k)),
                  kb3(tabs['w1re']), kb3(tabs['w1im']), kb3(tabs['w2re']), kb3(tabs['w2im']), kb3(tabs['kcat']),
                  pl.BlockSpec((1, nstep, 2, P2), lambda k, t: (k, 0, 0, 0))],
        out_specs=pl.BlockSpec((rows, W), lambda k, t: (t, k)),
        out_shape=jax.ShapeDtypeStruct((L, GROUP_W), F32),
        scratch_shapes=[pltpu.VMEM((rt, S * W), BF16), pltpu.VMEM((rt, S * W), F32), pltpu.VMEM((2, 1, P2), F32)],
        compiler_params=_cparams(("parallel", "arbitrary")),
        name="s5_scan",
    )(proj_a, tabs['w1re'], tabs['w1im'], tabs['w2re'], tabs['w2im'], tabs['kcat'], tabs['lampow'])


def _mixers_kernel(s5u_ref, cb_ref, cc_ref, ch_ref, su_ref, sv_ref, cch_ref, chh_ref, ys_ref,
                   d_ref, wglu_ref, cw_ref, lng_ref, lnb_ref, ws_ref, bs_ref, g_ref, o_ref):
    i = pl.program_id(0)
    tm = o_ref.shape[0]
    gw = GROUP_W
    y = ys_ref[...] + d_ref[...] * s5u_ref[...]
    y = jax.nn.gelu(y)
    y = y * jax.nn.sigmoid(jnp.dot(y.astype(BF16), wglu_ref[...], preferred_element_type=F32))
    o_ref[:, 0:gw] = _rms(y, g_ref[0:1, :]).astype(o_ref.dtype)
    z = cc_ref[...] * ch_ref[...]
    zh = jnp.where(i > 0, cch_ref[...] * chh_ref[...], 0.0)
    row = lax.broadcasted_iota(jnp.int32, z.shape, 0)
    z1 = jnp.where(row == 0, zh[7:8, :], pltpu.roll(z, 1, 0))
    z2 = jnp.where(row == 0, zh[6:7, :], jnp.where(row == 1, zh[7:8, :], pltpu.roll(z, 2, 0)))
    conv = cw_ref[0:1, :] * z2 + cw_ref[1:2, :] * z1 + cw_ref[2:3, :] * z
    o_ref[:, gw:2 * gw] = _rms(cb_ref[...] * conv, g_ref[1:2, :]).astype(o_ref.dtype)
    uu = jax.nn.gelu(su_ref[...])
    vv = _ln(jax.nn.gelu(sv_ref[...]), lng_ref[...], lnb_ref[...]).astype(BF16)
    pi = lax.broadcasted_iota(jnp.int32, (SGU_BLK, SGU_BLK), 0)
    pj = lax.broadcasted_iota(jnp.int32, (SGU_BLK, SGU_BLK), 1)
    causal = (pj // CHUNK) <= (pi // CHUNK)
    hd = gw // SGU_HEADS
    ws = [jnp.where(causal, ws_ref[h], 0.0).astype(BF16) for h in range(SGU_HEADS)]
    blocks = []
    for n in range(tm // SGU_BLK):
        vb = vv[n * SGU_BLK:(n + 1) * SGU_BLK, :]
        blocks.append(jnp.concatenate(
            [jnp.dot(ws[h], vb[:, h * hd:(h + 1) * hd], preferred_element_type=F32) for h in range(SGU_HEADS)],
            axis=1) + bs_ref[...])
    mixed = jnp.concatenate(blocks, axis=0)
    o_ref[:, 2 * gw:3 * gw] = _rms(uu * mixed, g_ref[2:3, :]).astype(o_ref.dtype)


def _mixers(proj_a, ys5, s5_d, w_glu, conv_w, ln_g, ln_b, sgu_w, sgu_b, mix_g, tm):
    L = proj_a.shape[0]
    gw = GROUP_W
    hb = tm // 8
    col = lambda c: pl.BlockSpec((tm, gw), lambda i, c=c: (i, c))
    halo = lambda c: pl.BlockSpec((8, gw), lambda i, c=c: (jnp.maximum(i * hb - 1, 0), c))
    full = lambda a: pl.BlockSpec(a.shape, lambda i: (0,) * a.ndim)
    hd = gw // SGU_HEADS
    bs_full = jnp.repeat(sgu_b.astype(F32).T, hd, axis=1)
    consts = [s5_d.reshape(1, gw).astype(F32), w_glu.astype(BF16), conv_w.astype(F32),
              ln_g.reshape(1, gw).astype(F32), ln_b.reshape(1, gw).astype(F32), sgu_w.astype(F32),
              bs_full, mix_g.reshape(3, gw).astype(F32)]
    return pl.pallas_call(
        _mixers_kernel,
        grid=(L // tm,),
        in_specs=[col(0), col(1), col(2), col(3), col(4), col(5), halo(2), halo(3),
                  pl.BlockSpec((tm, gw), lambda i: (i, 0))] + [full(a) for a in consts],
        out_specs=pl.BlockSpec((tm, 3 * gw), lambda i: (i, 0)),
        out_shape=jax.ShapeDtypeStruct((L, 3 * gw), BF16),
        compiler_params=_cparams(("parallel",)),
        name="row_mixers",
    )(proj_a, proj_a, proj_a, proj_a, proj_a, proj_a, proj_a, proj_a, ys5, *consts)


def _t5_bucket(rel):
    half = NUM_BUCKETS // 2
    ret = jnp.where(rel > 0, half, 0)
    n = jnp.abs(rel)
    max_exact = half // 2
    large = max_exact + (jnp.log(jnp.maximum(n, 1).astype(F32) / max_exact)
                         / math.log(MAX_DISTANCE / max_exact) * (half - max_exact)).astype(jnp.int32)
    large = jnp.minimum(large, half - 1)
    return ret + jnp.where(n < max_exact, n, large)


def _attn_bias_tables(rel_bias, tq):
    assert tq >= MAX_DISTANCE
    rb = rel_bias.astype(F32)
    far = rb[NUM_BUCKETS // 2 - 1]
    buckets = jnp.arange(NUM_BUCKETS)[:, None]

    def bias_of(rel):
        onehot = _t5_bucket(rel)[:, :, None, None] == buckets
        return jnp.sum(jnp.where(onehot, rb, 0.0), axis=2) - far

    kj = jnp.arange(tq)[:, None]
    qi = jnp.arange(tq)[None, :]
    diag = jnp.where(((kj // CHUNK) <= (qi // CHUNK))[..., None], bias_of(kj - qi), NEG_INF)
    prev = bias_of(kj - tq - qi)
    tabs = jnp.stack([jnp.transpose(diag, (2, 0, 1)), jnp.transpose(prev, (2, 0, 1))], axis=1)
    tabs = jnp.where(tabs > 0.5 * NEG_INF, tabs * LOG2E, NEG_INF)
    return jnp.concatenate([tabs, tabs], axis=3)


def _attn_kernel(qt_ref, k_ref, vt_ref, nb_ref, lq1_ref, lk1_ref, lq2_ref, lk2_ref, g_ref, o_ref,
                 qq_s, sa_s, sb_s, p_s, m_s, l_s, a_s, acc_s, *, lambda_init):
    i = pl.program_id(1)
    tq = qt_ref.shape[2]
    dq = DIFF_QK_DIM
    qt = qt_ref[0]
    feat = lax.broadcasted_iota(jnp.int32, qt.shape, 0)
    zero = jnp.zeros_like(qt)
    qq = jnp.concatenate([jnp.where(feat < dq, qt, zero), jnp.where(feat >= dq, qt, zero)], axis=1)

    qq_s[...] = qq
    m_s[...] = jnp.full(m_s.shape, -jnp.inf, F32)
    l_s[...] = jnp.zeros(l_s.shape, F32)
    acc_s[...] = jnp.zeros(acc_s.shape, F32)

    def scores(s_ref, j):
        kb = k_ref[pl.ds(pl.multiple_of(j * tq, tq), tq), :]
        s_ref[...] = jnp.dot(kb, qq_s[...], preferred_element_type=F32)

    def absorb(s_ref, j, bias=None):
        for c in range(2 * tq // 128):
            cs = slice(c * 128, (c + 1) * 128)
            s = s_ref[:, cs]
            if bias is not None:
                s = s + bias(cs)
            m_old = m_s[:, cs]
            m_new = jnp.maximum(m_old, jnp.max(s, axis=0, keepdims=True))
            alpha = jnp.exp2(m_old - m_new)
            p = jnp.exp2(s - m_new)
            l_s[:, cs] = alpha * l_s[:, cs] + jnp.sum(p, axis=0, keepdims=True)
            m_s[:, cs] = m_new
            a_s[:, cs] = alpha
            p_s[:, cs] = p.astype(BF16)
        acc_s[...] = a_s[...] * acc_s[...] + jnp.dot(vt_ref[j], p_s[...], preferred_element_type=F32)

    scores(sa_s, i)
    absorb(sa_s, i, lambda cs: nb_ref[0, 0, :, cs])
    jp = jnp.maximum(i - 1, 0)
    first = jnp.where(i > 0, 0.0, NEG_INF)
    scores(sa_s, jp)
    absorb(sa_s, jp, lambda cs: nb_ref[0, 1, :, cs] + first)
    n_far = jnp.maximum(i - 1, 0)
    n_pairs = n_far // 2

    @pl.when(n_far % 2 == 1)
    def _():
        scores(sa_s, n_far - 1)
        absorb(sa_s, n_far - 1)

    scores(sa_s, 0)

    def pair(jj, c):
        scores(sb_s, 2 * jj + 1)
        absorb(sa_s, 2 * jj)
        scores(sa_s, jnp.minimum(2 * jj + 2, jnp.maximum(2 * n_pairs - 2, 0)))
        absorb(sb_s, 2 * jj + 1)
        return c

    lax.fori_loop(0, n_pairs, pair, 0)
    o = acc_s[...] / l_s[...]
    lam = (jnp.exp(jnp.sum(lq1_ref[...] * lk1_ref[...], keepdims=True))
           - jnp.exp(jnp.sum(lq2_ref[...] * lk2_ref[...], keepdims=True)) + lambda_init)
    out = o[:, :tq] - lam * o[:, tq:]
    out = out * lax.rsqrt(jnp.mean(jnp.square(out), axis=0, keepdims=True) + EPS) * g_ref[...]
    o_ref[...] = (out * (1.0 - lambda_init)).T.astype(o_ref.dtype)


def _diff_attention(qvt, kmat, nb, lq1, lk1, lq2, lk2, subln_g, lambda_init):
    nq, _, tq = qvt.shape
    L = kmat.shape[0]
    H, dv = DIFF_HEADS, DIFF_V_DIM
    vec = lambda a: a.reshape(1, -1).astype(F32)
    small = lambda n: pl.BlockSpec((1, n), lambda h, i: (0, 0))
    return pl.pallas_call(
        functools.partial(_attn_kernel, lambda_init=lambda_init),
        grid=(H, nq),
        in_specs=[pl.BlockSpec((1, dv, tq), lambda h, i: (i, h, 0)),
                  pl.BlockSpec((L, dv), lambda h, i: (0, h)),
                  pl.BlockSpec((nq, dv, tq), lambda h, i: (0, H + h, 0)),
                  pl.BlockSpec((1, 2, tq, 2 * tq), lambda h, i: (h, 0, 0, 0)),
                  small(DIFF_QK_DIM), small(DIFF_QK_DIM), small(DIFF_QK_DIM), small(DIFF_QK_DIM),
                  pl.BlockSpec((dv, 1), lambda h, i: (0, 0))],
        out_specs=pl.BlockSpec((tq, dv), lambda h, i: (i, h)),
        out_shape=jax.ShapeDtypeStruct((L, H * dv), BF16),
        scratch_shapes=[pltpu.VMEM((dv, 2 * tq), BF16),
                        pltpu.VMEM((tq, 2 * tq), F32), pltpu.VMEM((tq, 2 * tq), F32),
                        pltpu.VMEM((tq, 2 * tq), BF16),
                        pltpu.VMEM((1, 2 * tq), F32), pltpu.VMEM((1, 2 * tq), F32), pltpu.VMEM((1, 2 * tq), F32),
                        pltpu.VMEM((dv, 2 * tq), F32)],
        compiler_params=_cparams(("parallel", "arbitrary")),
        name="diff_attention",
    )(qvt, kmat, qvt, nb, vec(lq1), vec(lk1), vec(lq2), vec(lk2), subln_g.reshape(dv, 1).astype(F32))


def _outproj_kernel(abc_ref, d_ref, x_ref, wa_ref, wd_ref, g_ref, b_ref, rhi_ref, rlo_ref,
                    x1_ref, x1b_ref, lg_ref, *, alpha):
    mix = (jnp.dot(abc_ref[...], wa_ref[...], preferred_element_type=F32)
           + jnp.dot(d_ref[...], wd_ref[...], preferred_element_type=F32))
    x1 = _ln(alpha * x_ref[...] + mix, g_ref[...], b_ref[...])
    x1_ref[...] = x1
    hi = x1.astype(BF16)
    x1b_ref[...] = hi
    lo = (x1 - hi.astype(F32)).astype(BF16)
    nt = (((1,), (1,)), ((), ()))
    lg_ref[...] = (lax.dot_general(rhi_ref[...], hi, nt, preferred_element_type=F32)
                   + lax.dot_general(rhi_ref[...], lo, nt, preferred_element_type=F32)
                   + lax.dot_general(rlo_ref[...], hi, nt, preferred_element_type=F32))


def _outproj(abc, d_out, x, w_out, ln_g, ln_b, router_w, alpha, tm):
    L, D = x.shape
    E = router_w.shape[1]
    ka = abc.shape[1]
    kd = d_out.shape[1]
    rwt = router_w.astype(F32).T
    rhi = rwt.astype(BF16)
    rlo = (rwt - rhi.astype(F32)).astype(BF16)
    full = lambda shape: pl.BlockSpec(shape, lambda i: (0,) * len(shape))
    return pl.pallas_call(
        functools.partial(_outproj_kernel, alpha=alpha),
        grid=(L // tm,),
        in_specs=[pl.BlockSpec((tm, ka), lambda i: (i, 0)),
                  pl.BlockSpec((tm, kd), lambda i: (i, 0)),
                  pl.BlockSpec((tm, D), lambda i: (i, 0)),
                  pl.BlockSpec((ka, D), lambda i: (0, 0)),
                  pl.BlockSpec((kd, D), lambda i: (ka // kd, 0)),
                  full((1, D)), full((1, D)), full((E, D)), full((E, D))],
        out_specs=[pl.BlockSpec((tm, D), lambda i: (i, 0)),
                   pl.BlockSpec((tm, D), lambda i: (i, 0)),
                   pl.BlockSpec((E, tm), lambda i: (0, i))],
        out_shape=[jax.ShapeDtypeStruct((L, D), F32), jax.ShapeDtypeStruct((L, D), BF16),
                   jax.ShapeDtypeStruct((E, L), F32)],
        compiler_params=_cparams(("parallel",)),
        name="outproj_ln1",
    )(abc, d_out, x, w_out, w_out, ln_g.reshape(1, D).astype(F32), ln_b.reshape(1, D).astype(F32), rhi, rlo)


def _router_kernel(lg_ref, bias_ref, tri_ref, e_ref, r_ref, w_ref, cnt_ref, carry_ref):
    i = pl.program_id(0)
    E, tn = lg_ref.shape
    ng = N_EXPERT_GROUPS
    gs_ = E // ng

    @pl.when(i == 0)
    def _():
        carry_ref[...] = jnp.zeros_like(carry_ref)

    s = jax.nn.sigmoid(lg_ref[...])
    sel = s + bias_ref[...]
    midx = lax.broadcasted_iota(jnp.int32, (gs_, tn), 0).astype(F32)
    rows, gscore = [], []
    for g in range(ng):
        rg = sel[g * gs_:(g + 1) * gs_, :]
        m1 = jnp.max(rg, axis=0, keepdims=True)
        first = jnp.min(jnp.where(rg == m1, midx, float(gs_)), axis=0, keepdims=True)
        m2 = jnp.max(jnp.where(midx == first, -jnp.inf, rg), axis=0, keepdims=True)
        rows.append(rg)
        gscore.append(m1 + m2)
    vals = []
    for g in range(ng):
        rank = jnp.zeros((1, tn), F32)
        for o in range(ng):
            if o != g:
                beats = (gscore[o] >= gscore[g]) if o < g else (gscore[o] > gscore[g])
                rank = rank + jnp.where(beats, 1.0, 0.0)
        vals.append(jnp.where(rank < TOPK_GROUPS, rows[g], -jnp.inf))
    val = jnp.concatenate(vals, axis=0)
    eidx = lax.broadcasted_iota(jnp.int32, val.shape, 0)
    erank = jnp.zeros(val.shape, F32)
    for e in range(E):
        other = val[e:e + 1, :]
        erank = erank + jnp.where(eidx > e, jnp.where(other >= val, 1.0, 0.0), jnp.where(other > val, 1.0, 0.0))
    chosen = erank < TOP_K
    wsel = jnp.where(chosen, s, 0.0)
    wn = wsel / (jnp.sum(wsel, axis=0, keepdims=True) + 1e-20) * ROUTED_SCALE
    chf = jnp.where(chosen, 1.0, 0.0)
    incl = jnp.dot(chf.astype(BF16), tri_ref[...], preferred_element_type=F32)
    base = carry_ref[...]
    pos = base + incl - chf
    carry_ref[...] = base + incl[:, tn - 1:tn]
    cnt_ref[...] = (base + incl[:, tn - 1:tn]).astype(jnp.int32)
    eidf = eidx.astype(F32)
    cand = jnp.where(chosen, eidf, float(E))
    for k in range(TOP_K):
        ek = jnp.min(cand, axis=0, keepdims=True)
        hit = cand == ek
        e_ref[k:k + 1, :] = ek.astype(jnp.int32)
        r_ref[k:k + 1, :] = jnp.sum(jnp.where(hit, pos, 0.0), axis=0, keepdims=True).astype(jnp.int32)
        w_ref[k:k + 1, :] = jnp.sum(jnp.where(hit, wn, 0.0), axis=0, keepdims=True)
        cand = jnp.where(hit, float(E), cand)


def _router(logits_t, router_bias, tn):
    E, L = logits_t.shape
    tri = (jnp.arange(tn)[:, None] <= jnp.arange(tn)[None, :]).astype(BF16)
    slot = lambda dt: jax.ShapeDtypeStruct((TOP_K, L), dt)
    return pl.pallas_call(
        _router_kernel,
        grid=(L // tn,),
        in_specs=[pl.BlockSpec((E, tn), lambda i: (0, i)),
                  pl.BlockSpec((E, 1), lambda i: (0, 0)),
                  pl.BlockSpec((tn, tn), lambda i: (0, 0))],
        out_specs=[pl.BlockSpec((TOP_K, tn), lambda i: (0, i)),
                   pl.BlockSpec((TOP_K, tn), lambda i: (0, i)),
                   pl.BlockSpec((TOP_K, tn), lambda i: (0, i)),
                   pl.BlockSpec((E, 1), lambda i: (0, 0))],
        out_shape=[slot(jnp.int32), slot(jnp.int32), slot(F32), jax.ShapeDtypeStruct((E, 1), jnp.int32)],
        scratch_shapes=[pltpu.VMEM((E, 1), F32)],
        compiler_params=_cparams(("arbitrary",)),
        name="router_topk",
    )(logits_t, router_bias.reshape(E, 1).astype(F32), tri)


def _dispatch_kernel(dest_ref, pad_ref, x_ref, xs_ref, zero_ref, sem, zsem):
    i = pl.program_id(0)
    tm = x_ref.shape[0]

    def row_copy(r, k):
        return pltpu.make_async_copy(x_ref.at[pl.ds(r, 1), :], xs_ref.at[pl.ds(dest_ref[k, r], 1), :], sem)

    def zero_copy(dst):
        return pltpu.make_async_copy(zero_ref, xs_ref.at[pl.ds(dst, 1), :], zsem)

    @pl.when(i == 0)
    def _():
        zero_ref[...] = jnp.zeros_like(zero_ref)

        def per_expert(e, c):
            lo, hi = pad_ref[0, e], pad_ref[1, e]
            lax.fori_loop(lo, hi, lambda d, c2: (zero_copy(d).start(), c2)[1], 0)
            lax.fori_loop(lo, hi, lambda d, c2: (zero_copy(d).wait(), c2)[1], 0)
            return c
        lax.fori_loop(0, N_EXPERTS, per_expert, 0)

    def issue(r, c):
        for k in range(TOP_K):
            row_copy(r, k).start()
        return c
    lax.fori_loop(0, tm, issue, 0)

    def drain(r, c):
        for k in range(TOP_K):
            row_copy(r, k).wait()
        return c
    lax.fori_loop(0, tm, drain, 0)


def _dispatch(x1, dest, pad_lo_hi, n_slots, tm):
    L, D = x1.shape
    return pl.pallas_call(
        _dispatch_kernel,
        grid=(L // tm,),
        in_specs=[pl.BlockSpec((TOP_K, tm), lambda i: (0, i), memory_space=pltpu.SMEM),
                  pl.BlockSpec(memory_space=pltpu.SMEM),
                  pl.BlockSpec((tm, D), lambda i: (i, 0))],
        out_specs=pl.BlockSpec(memory_space=pl.ANY),
        out_shape=jax.ShapeDtypeStruct((n_slots, D), x1.dtype),
        scratch_shapes=[pltpu.VMEM((1, D), x1.dtype), pltpu.SemaphoreType.DMA(()), pltpu.SemaphoreType.DMA(())],
        compiler_params=_cparams(("arbitrary",)),
        name="moe_dispatch",
    )(dest, pad_lo_hi, x1)


def _experts_kernel(be_ref, nb_ref, xs_ref, wgu_ref, wd_ref, y_ref, wgu_b, wd_b):
    b = pl.program_id(0)

    @pl.when((b == 0) | (be_ref[b] != be_ref[jnp.maximum(b - 1, 0)]))
    def _():
        wgu_b[...] = wgu_ref[0, 0].astype(BF16)
        wd_b[...] = wd_ref[0, 0].astype(BF16)

    @pl.when(b < nb_ref[0])
    def _():
        de = wd_b.shape[0]
        h = jnp.dot(xs_ref[...].astype(BF16), wgu_b[...], preferred_element_type=F32)
        a = jax.nn.silu(h[:, :de]) * h[:, de:]
        y_ref[...] = jnp.dot(a.astype(BF16), wd_b[...], preferred_element_type=F32).astype(y_ref.dtype)


def _experts(xs, block_e, n_used, w_gu, w_down, layer, blk):
    n_slots, D = xs.shape
    nblk = n_slots // blk
    de2 = w_gu.shape[3]
    de = w_down.shape[2]
    last = lambda b, be, nb: jnp.minimum(b, nb[0] - 1)
    return pl.pallas_call(
        _experts_kernel,
        grid_spec=pltpu.PrefetchScalarGridSpec(
            num_scalar_prefetch=2,
            grid=(nblk,),
            in_specs=[pl.BlockSpec((blk, D), lambda b, be, nb: (last(b, be, nb), 0)),
                      pl.BlockSpec((1, 1, D, de2), lambda b, be, nb: (layer, be[b], 0, 0)),
                      pl.BlockSpec((1, 1, de, D), lambda b, be, nb: (layer, be[b], 0, 0))],
            out_specs=pl.BlockSpec((blk, D), lambda b, be, nb: (last(b, be, nb), 0)),
            scratch_shapes=[pltpu.VMEM((D, de2), BF16), pltpu.VMEM((de, D), BF16)]),
        out_shape=jax.ShapeDtypeStruct((n_slots, D), F32),
        compiler_params=_cparams(("arbitrary",)),
        name="moe_experts",
    )(block_e, n_used, xs, w_gu, w_down)


def _combine_kernel(dcur_ref, dnxt_ref, y_ref, w_ref, x1_ref, x1b_ref, sgu_ref, sdn_ref, g_ref, b_ref,
                    x2_ref, x2b_ref, buf, sem, *, alpha):
    i = pl.program_id(0)
    n = pl.num_programs(0)
    tm = x1_ref.shape[0]
    slot = i % 2

    def row_copy(d_ref, s, r, k):
        return pltpu.make_async_copy(y_ref.at[pl.ds(d_ref[k, r], 1), :],
                                     buf.at[s, k, pl.ds(r, 1), :], sem.at[s])

    def issue(d_ref, s):
        def body(r, c):
            for k in range(TOP_K):
                row_copy(d_ref, s, r, k).start()
            return c
        lax.fori_loop(0, tm, body, 0)

    @pl.when(i == 0)
    def _():
        issue(dcur_ref, 0)

    @pl.when(i + 1 < n)
    def _():
        issue(dnxt_ref, 1 - slot)

    de = sdn_ref.shape[0]
    h = jnp.dot(x1b_ref[...], sgu_ref[...], preferred_element_type=F32)
    a = jax.nn.silu(h[:, :de]) * h[:, de:]
    ffn = jnp.dot(a.astype(BF16), sdn_ref[...], preferred_element_type=F32)

    def drain(r, c):
        for k in range(TOP_K):
            row_copy(dcur_ref, slot, r, k).wait()
        return c
    lax.fori_loop(0, tm, drain, 0)

    for k in range(TOP_K):
        ffn = ffn + buf[slot, k] * w_ref[:, k:k + 1]
    x2 = _ln(alpha * x1_ref[...] + ffn, g_ref[...], b_ref[...])
    x2_ref[...] = x2
    x2b_ref[...] = x2.astype(BF16)


def _combine(dest, y, w_t, x1, x1b, sh_gu, sh_down, ln_g, ln_b, alpha, tm):
    L, D = x1.shape
    n = L // tm
    full = lambda shape: pl.BlockSpec(shape, lambda i: (0,) * len(shape))
    return pl.pallas_call(
        functools.partial(_combine_kernel, alpha=alpha),
        grid=(n,),
        in_specs=[pl.BlockSpec((TOP_K, tm), lambda i: (0, i), memory_space=pltpu.SMEM),
                  pl.BlockSpec((TOP_K, tm), lambda i: (0, jnp.minimum(i + 1, n - 1)), memory_space=pltpu.SMEM),
                  pl.BlockSpec(memory_space=pl.ANY),
                  pl.BlockSpec((tm, TOP_K), lambda i: (i, 0)),
                  pl.BlockSpec((tm, D), lambda i: (i, 0)),
                  pl.BlockSpec((tm, D), lambda i: (i, 0)),
                  full(sh_gu.shape), full(sh_down.shape), full((1, D)), full((1, D))],
        out_specs=[pl.BlockSpec((tm, D), lambda i: (i, 0)), pl.BlockSpec((tm, D), lambda i: (i, 0))],
        out_shape=[jax.ShapeDtypeStruct((L, D), F32), jax.ShapeDtypeStruct((L, D), BF16)],
        scratch_shapes=[pltpu.VMEM((2, TOP_K, tm, D), y.dtype), pltpu.SemaphoreType.DMA((2,))],
        compiler_params=_cparams(("arbitrary",)),
        name="moe_combine_ln2",
    )(dest, dest, y, w_t, x1, x1b, sh_gu, sh_down, ln_g.reshape(1, D).astype(F32), ln_b.reshape(1, D).astype(F32))


MOE_BLK = 256


def _moe_layer(x1, x1b, logits_t, router_bias, w_gu, w_down, layer, sh_gu, sh_down, ln_g, ln_b, alpha,
               router_tn, dispatch_tm, combine_tm):
    L, D = x1.shape
    E = N_EXPERTS
    e_k, r_k, w_k, counts = _router(logits_t, router_bias, router_tn)
    counts = counts.reshape(E)
    padded = (counts + MOE_BLK - 1) // MOE_BLK * MOE_BLK
    pends = jnp.cumsum(padded)
    pstarts = pends - padded
    dest = jnp.sum(jnp.where(e_k[..., None] == jnp.arange(E), pstarts, 0), axis=-1) + r_k
    nblk = -(-(L * TOP_K) // MOE_BLK) + E
    n_used = (pends[-1] // MOE_BLK).astype(jnp.int32)
    blocks = jnp.minimum(jnp.arange(nblk, dtype=jnp.int32), n_used - 1)
    block_e = jnp.sum((pends[None, :] <= (blocks * MOE_BLK)[:, None]).astype(jnp.int32), axis=1)
    block_e = jnp.minimum(block_e, E - 1).astype(jnp.int32)
    pad_lo_hi = jnp.stack([pstarts + counts, pends]).astype(jnp.int32)
    xs = _dispatch(x1, dest, pad_lo_hi, nblk * MOE_BLK, dispatch_tm)
    y = _experts(xs, block_e, n_used.reshape(1), w_gu, w_down, layer, MOE_BLK)
    return _combine(dest, y, w_k.T, x1, x1b, sh_gu, sh_down, ln_g, ln_b, alpha, combine_tm)


def _pick(n, pref):
    t = min(n, pref)
    assert n % t == 0
    return t


def kernel(x, w_in, w_out, mix_norm_g, s5_lambda_re, s5_lambda_im, s5_log_dt, s5_b_re, s5_b_im, s5_c_re, s5_c_im, s5_d, s5_w_glu, conv_w, sgu_ln_g, sgu_ln_b, sgu_w, sgu_b, diff_lq1, diff_lk1, diff_lq2, diff_lk2, diff_subln_g, rel_bias, ln1_g, ln1_b, router_w, router_bias, moe_w_gu, moe_w_down, shared_w_gu, shared_w_down, ln2_g, ln2_b):
    Bt, L, D = x.shape
    assert Bt == 1
    depth = w_in.shape[0]
    alpha = (2 * depth) ** 0.25
    gw = GROUP_W
    tq = _pick(L, 256)
    nb = _attn_bias_tables(rel_bias, tq)
    xf = x.reshape(L, D)
    xb = xf.astype(BF16)
    for l in range(depth):
        w_in_b = w_in[l].astype(BF16)
        proj_a = _matmul(xb, w_in_b[:, :6 * gw], _pick(L, 1024), 512, F32)
        kmat = _matmul(xb, w_in_b[:, 7 * gw:8 * gw], _pick(L, 1024), 512, BF16)
        w_q = (w_in[l][:, 6 * gw:7 * gw] * (DIFF_QK_DIM ** -0.5 * LOG2E)).astype(BF16)
        w_qv_t = jnp.concatenate([w_q, w_in_b[:, 8 * gw:]], axis=1).T
        qvt = _matmul_nt(w_qv_t, xb, tq, BF16)
        s5_rt = _pick(L // S5_SUB, 256)
        tabs = _s5_tables(s5_lambda_re[l], s5_lambda_im[l], s5_log_dt[l], s5_b_re[l], s5_b_im[l],
                          s5_c_re[l], s5_c_im[l], s5_rt)
        ys5 = _s5_scan(proj_a, tabs, s5_rt)
        abc = _mixers(proj_a, ys5, s5_d[l], s5_w_glu[l], conv_w[l], sgu_ln_g[l], sgu_ln_b[l], sgu_w[l], sgu_b[l],
                      mix_norm_g[l], _pick(L, 512))
        lambda_init = 0.8 - 0.6 * math.exp(-0.3 * l)
        d_out = _diff_attention(qvt, kmat, nb, diff_lq1[l], diff_lk1[l], diff_lq2[l], diff_lk2[l],
                                diff_subln_g[l], lambda_init)
        x1, x1b, logits_t = _outproj(abc, d_out, xf, w_out[l].astype(BF16), ln1_g[l], ln1_b[l], router_w[l],
                                     alpha, _pick(L, 256))
        xf, xb = _moe_layer(x1, x1b, logits_t, router_bias[l], moe_w_gu, moe_w_down, l,
                            shared_w_gu[l].astype(BF16), shared_w_down[l].astype(BF16),
                            ln2_g[l], ln2_b[l], alpha, _pick(L, 512), _pick(L, 256), _pick(L, 128))
    return xf.reshape(Bt, L, D)
```

```python
import functools
import math

import jax
import jax.numpy as jnp
from jax import lax
from jax.experimental import pallas as pl
from jax.experimental.pallas import tpu as pltpu

F32 = jnp.float32
BF16 = jnp.bfloat16

GROUP_W = 512
CHUNK = 64
S5_GROUP_CH = 16
S5_GROUPS = 32
S5_STATE = 64
S5_SUB = 16
S5_KBLOCKS = 4
SGU_BLK = 128
SGU_HEADS = 4
DIFF_HEADS = 4
DIFF_QK_DIM = 64
DIFF_V_DIM = 128
NUM_BUCKETS = 32
MAX_DISTANCE = 128
N_EXPERTS = 64
TOP_K = 8
N_EXPERT_GROUPS = 8
TOPK_GROUPS = 4
ROUTED_SCALE = 2.5
EPS = 1e-5
NEG_INF = -1e30
LOG2E = math.log2(math.e)

VMEM_LIMIT = 56 * 1024 * 1024


def _cparams(sem):
    return pltpu.CompilerParams(dimension_semantics=sem, vmem_limit_bytes=VMEM_LIMIT)


def _rms(x, g):
    return x * lax.rsqrt(jnp.mean(jnp.square(x), -1, keepdims=True) + EPS) * g


def _pack_bf16_pairs(x):
    c = x.shape[1] // 2
    hi = lax.bitcast_convert_type(x[:, :c].astype(BF16).astype(F32), jnp.uint32)
    lo = lax.bitcast_convert_type(x[:, c:].astype(BF16).astype(F32), jnp.uint32)
    return hi | (lo >> 16)


def _unpack_bf16_pairs(u):
    hi = lax.bitcast_convert_type(u & jnp.uint32(0xFFFF0000), F32)
    lo = lax.bitcast_convert_type(u << 16, F32)
    return hi, lo


ROW_TILE = 8
LANES = 128


def _store_token_tiles(ref, x):
    p = _pack_bf16_pairs(x)
    n = x.shape[0]
    for c in range(ROW_TILE):
        ref[pl.ds(c, n, stride=ROW_TILE), :] = p[:, c * LANES:(c + 1) * LANES]


def _load_token_tiles(ref, n):
    p = jnp.concatenate([ref[pl.ds(c, n, stride=ROW_TILE), :] for c in range(ROW_TILE)], axis=1)
    return _unpack_bf16_pairs(p)


def _ln(x, g, b):
    mu = jnp.mean(x, -1, keepdims=True)
    var = jnp.mean(jnp.square(x - mu), -1, keepdims=True)
    return (x - mu) * lax.rsqrt(var + EPS) * g + b


def _matmul_kernel(x_ref, w_ref, o_ref):
    o_ref[...] = jnp.dot(x_ref[...], w_ref[...], preferred_element_type=F32).astype(o_ref.dtype)


def _matmul(x, w, tm, tn, out_dtype):
    M, K = x.shape
    N = w.shape[1]
    return pl.pallas_call(
        _matmul_kernel,
        grid=(M // tm, N // tn),
        in_specs=[pl.BlockSpec((tm, K), lambda i, j: (i, 0)),
                  pl.BlockSpec((K, tn), lambda i, j: (0, j))],
        out_specs=pl.BlockSpec((tm, tn), lambda i, j: (i, j)),
        out_shape=jax.ShapeDtypeStruct((M, N), out_dtype),
        compiler_params=_cparams(("parallel", "arbitrary")),
        name="proj_matmul",
    )(x, w)


def _matmul_nt_kernel(w_ref, x_ref, o_ref):
    o_ref[0] = lax.dot_general(w_ref[...], x_ref[...], (((1,), (1,)), ((), ())),
                               preferred_element_type=F32).astype(o_ref.dtype)


def _matmul_nt(w_t, x, tm, out_dtype):
    M, K = x.shape
    N = w_t.shape[0]
    return pl.pallas_call(
        _matmul_nt_kernel,
        grid=(M // tm,),
        in_specs=[pl.BlockSpec((N, K), lambda i: (0, 0)),
                  pl.BlockSpec((tm, K), lambda i: (i, 0))],
        out_specs=pl.BlockSpec((1, N, tm), lambda i: (i, 0, 0)),
        out_shape=jax.ShapeDtypeStruct((M // tm, N, tm), out_dtype),
        compiler_params=_cparams(("parallel",)),
        name="proj_matmul_nt",
    )(w_t, x)


def _s5_tables(lam_re, lam_im, log_dt, b_re, b_im, c_re, c_im, n_rows):
    G, P, H, S = S5_GROUPS, S5_STATE, S5_GROUP_CH, S5_SUB
    hp = lax.Precision.HIGHEST
    dt = jnp.exp(log_dt.astype(F32))[:, None]
    lam = lax.complex(lam_re.astype(F32), lam_im.astype(F32))
    ldt = lam * dt
    lam_bar = jnp.exp(ldt)
    b_bar = ((lam_bar - 1.0) / lam)[..., None] * lax.complex(b_re.astype(F32), b_im.astype(F32))
    c = lax.complex(c_re.astype(F32), c_im.astype(F32))
    tau = jnp.arange(S + 1, dtype=F32)
    pows = jnp.exp(ldt[None] * tau[:, None, None])
    KB, GL = S5_KBLOCKS, S5_GROUPS // S5_KBLOCKS
    eye = jnp.eye(GL, dtype=bool)
    w1 = pows[:S][::-1][:, :, None, :] * jnp.transpose(b_bar, (0, 2, 1))[None]
    w1 = jnp.transpose(w1.reshape(S, KB, GL, H, P), (1, 0, 2, 3, 4))
    w1 = jnp.where(eye[None, None, :, None, :, None], w1[:, :, :, :, None, :], 0.0)
    w1 = w1.reshape(KB, S * GL * H, GL * P)
    w2 = jnp.transpose(c, (0, 2, 1))[:, :, None, :] * jnp.transpose(pows[1:], (1, 2, 0))[..., None]
    w2 = w2.reshape(KB, GL, P, S, H)
    w2 = jnp.where(eye[None, :, None, None, :, None], w2[:, :, :, :, None, :], 0.0)
    w2 = w2.reshape(KB, GL * P, S * GL * H)
    kc = jnp.real(jnp.einsum('ghp,tgp,gpi->tghi', c, pows[:S], b_bar, precision=hp))
    kc = jnp.transpose(kc.reshape(S, KB, GL, H, H), (1, 2, 4, 0, 3))
    kc = jnp.where(eye[None, :, None, None, :, None], kc[:, :, :, :, None, :], 0.0)
    kcat = kc.reshape(KB, GL * H, S * GL * H)

    nstep = max(1, (n_rows - 1).bit_length())
    kk = (S * (2 ** jnp.arange(nstep))).astype(F32)
    lp = jnp.exp(ldt[None] * kk[:, None, None])
    lp = jnp.transpose(lp.reshape(nstep, KB, GL * P), (1, 0, 2))
    lampow = jnp.stack([jnp.real(lp), jnp.imag(lp)], axis=2)
    return dict(
        kcat=kcat.astype(BF16),
        w1re=jnp.real(w1).astype(BF16), w1im=jnp.imag(w1).astype(BF16),
        w2re=jnp.real(w2).astype(BF16), w2im=(-jnp.imag(w2)).astype(BF16),
        lampow=lampow.astype(F32))


def _s5_kernel(u_ref, w1re_ref, w1im_ref, w2re_ref, w2im_ref, kcat_ref, lp_ref, o_ref,
               ucat_ref, yall_ref, carry_ref, *, nstep):
    t = pl.program_id(1)
    S = S5_SUB
    R = ucat_ref.shape[0]
    W = u_ref.shape[1]

    @pl.when(t == 0)
    def _():
        carry_ref[...] = jnp.zeros_like(carry_ref)

    for j in range(S):
        ucat_ref[:, j * W:(j + 1) * W] = u_ref[pl.ds(j, R, stride=S), :].astype(BF16)
    ucat = ucat_ref[...]
    xre = jnp.dot(ucat, w1re_ref[0], preferred_element_type=F32)
    xim = jnp.dot(ucat, w1im_ref[0], preferred_element_type=F32)
    row = lax.broadcasted_iota(jnp.int32, xre.shape, 0)
    cre, cim = carry_ref[0], carry_ref[1]
    lr, li = lp_ref[0, 0, 0:1, :], lp_ref[0, 0, 1:2, :]
    xre = xre + jnp.where(row == 0, lr * cre - li * cim, 0.0)
    xim = xim + jnp.where(row == 0, lr * cim + li * cre, 0.0)
    for k in range(nstep):
        sh = 1 << k
        pre = pltpu.roll(xre, sh, 0)
        pim = pltpu.roll(xim, sh, 0)
        lr, li = lp_ref[0, k, 0:1, :], lp_ref[0, k, 1:2, :]
        keep = row >= sh
        xre, xim = (xre + jnp.where(keep, lr * pre - li * pim, 0.0),
                    xim + jnp.where(keep, lr * pim + li * pre, 0.0))
    carry_ref[0] = xre[R - 1:R, :]
    carry_ref[1] = xim[R - 1:R, :]
    sre = jnp.where(row >= 1, pltpu.roll(xre, 1, 0), cre).astype(BF16)
    sim = jnp.where(row >= 1, pltpu.roll(xim, 1, 0), cim).astype(BF16)
    yall_ref[...] = (jnp.dot(sre, w2re_ref[0], preferred_element_type=F32)
                     + jnp.dot(sim, w2im_ref[0], preferred_element_type=F32))
    for j in range(S):
        yall_ref[:, j * W:] += jnp.dot(ucat_ref[:, j * W:(j + 1) * W], kcat_ref[0, :, :(S - j) * W],
                                       preferred_element_type=F32)
    for j in range(S):
        o_ref[pl.ds(j, R, stride=S), :] = yall_ref[:, j * W:(j + 1) * W]


def _s5_scan(proj_a, tabs, rt):
    L = proj_a.shape[0]
    S, KB = S5_SUB, S5_KBLOCKS
    W = GROUP_W // KB
    nstep = tabs['lampow'].shape[1]
    P2 = tabs['lampow'].shape[3]
    rows = rt * S
    kb3 = lambda a: pl.BlockSpec((1,) + a.shape[1:], lambda k, t: (k, 0, 0))
    return pl.pallas_call(
        functools.partial(_s5_kernel, nstep=nstep),
        grid=(KB, L // rows),
        in_specs=[pl.BlockSpec((rows, W), lambda k, t: (t, k)),
                  kb3(tabs['w1re']), kb3(tabs['w1im']), kb3(tabs['w2re']), kb3(tabs['w2im']), kb3(tabs['kcat']),
                  pl.BlockSpec((1, nstep, 2, P2), lambda k, t: (k, 0, 0, 0))],
        out_specs=pl.BlockSpec((rows, W), lambda k, t: (t, k)),
        out_shape=jax.ShapeDtypeStruct((L, GROUP_W), F32),
        scratch_shapes=[pltpu.VMEM((rt, S * W), BF16), pltpu.VMEM((rt, S * W), F32), pltpu.VMEM((2, 1, P2), F32)],
        compiler_params=_cparams(("parallel", "arbitrary")),
        name="s5_scan",
    )(proj_a, tabs['w1re'], tabs['w1im'], tabs['w2re'], tabs['w2im'], tabs['kcat'], tabs['lampow'])


def _mixers_kernel(s5u_ref, cb_ref, cc_ref, ch_ref, su_ref, sv_ref, cch_ref, chh_ref, ys_ref,
                   d_ref, wglu_ref, cw_ref, lng_ref, lnb_ref, ws_ref, bs_ref, g_ref, o_ref):
    i = pl.program_id(0)
    tm = o_ref.shape[0]
    gw = GROUP_W
    y = ys_ref[...] + d_ref[...] * s5u_ref[...]
    y = jax.nn.gelu(y)
    y = y * jax.nn.sigmoid(jnp.dot(y.astype(BF16), wglu_ref[...], preferred_element_type=F32))
    o_ref[:, 0:gw] = _rms(y, g_ref[0:1, :]).astype(o_ref.dtype)
    z = cc_ref[...] * ch_ref[...]
    zh = jnp.where(i > 0, cch_ref[...] * chh_ref[...], 0.0)
    row = lax.broadcasted_iota(jnp.int32, z.shape, 0)
    z1 = jnp.where(row == 0, zh[7:8, :], pltpu.roll(z, 1, 0))
    z2 = jnp.where(row == 0, zh[6:7, :], jnp.where(row == 1, zh[7:8, :], pltpu.roll(z, 2, 0)))
    conv = cw_ref[0:1, :] * z2 + cw_ref[1:2, :] * z1 + cw_ref[2:3, :] * z
    o_ref[:, gw:2 * gw] = _rms(cb_ref[...] * conv, g_ref[1:2, :]).astype(o_ref.dtype)
    uu = jax.nn.gelu(su_ref[...])
    vv = _ln(jax.nn.gelu(sv_ref[...]), lng_ref[...], lnb_ref[...]).astype(BF16)
    pi = lax.broadcasted_iota(jnp.int32, (SGU_BLK, SGU_BLK), 0)
    pj = lax.broadcasted_iota(jnp.int32, (SGU_BLK, SGU_BLK), 1)
    causal = (pj // CHUNK) <= (pi // CHUNK)
    hd = gw // SGU_HEADS
    ws = [jnp.where(causal, ws_ref[h], 0.0).astype(BF16) for h in range(SGU_HEADS)]
    blocks = []
    for n in range(tm // SGU_BLK):
        vb = vv[n * SGU_BLK:(n + 1) * SGU_BLK, :]
        blocks.append(jnp.concatenate(
            [jnp.dot(ws[h], vb[:, h * hd:(h + 1) * hd], preferred_element_type=F32) for h in range(SGU_HEADS)],
            axis=1) + bs_ref[...])
    mixed = jnp.concatenate(blocks, axis=0)
    o_ref[:, 2 * gw:3 * gw] = _rms(uu * mixed, g_ref[2:3, :]).astype(o_ref.dtype)


def _mixers(proj_a, ys5, s5_d, w_glu, conv_w, ln_g, ln_b, sgu_w, sgu_b, mix_g, tm):
    L = proj_a.shape[0]
    gw = GROUP_W
    hb = tm // 8
    col = lambda c: pl.BlockSpec((tm, gw), lambda i, c=c: (i, c))
    halo = lambda c: pl.BlockSpec((8, gw), lambda i, c=c: (jnp.maximum(i * hb - 1, 0), c))
    full = lambda a: pl.BlockSpec(a.shape, lambda i: (0,) * a.ndim)
    hd = gw // SGU_HEADS
    bs_full = jnp.repeat(sgu_b.astype(F32).T, hd, axis=1)
    consts = [s5_d.reshape(1, gw).astype(F32), w_glu.astype(BF16), conv_w.astype(F32),
              ln_g.reshape(1, gw).astype(F32), ln_b.reshape(1, gw).astype(F32), sgu_w.astype(F32),
              bs_full, mix_g.reshape(3, gw).astype(F32)]
    return pl.pallas_call(
        _mixers_kernel,
        grid=(L // tm,),
        in_specs=[col(0), col(1), col(2), col(3), col(4), col(5), halo(2), halo(3),
                  pl.BlockSpec((tm, gw), lambda i: (i, 0))] + [full(a) for a in consts],
        out_specs=pl.BlockSpec((tm, 3 * gw), lambda i: (i, 0)),
        out_shape=jax.ShapeDtypeStruct((L, 3 * gw), BF16),
        compiler_params=_cparams(("parallel",)),
        name="row_mixers",
    )(proj_a, proj_a, proj_a, proj_a, proj_a, proj_a, proj_a, proj_a, ys5, *consts)


def _t5_bucket(rel):
    half = NUM_BUCKETS // 2
    ret = jnp.where(rel > 0, half, 0)
    n = jnp.abs(rel)
    max_exact = half // 2
    large = max_exact + (jnp.log(jnp.maximum(n, 1).astype(F32) / max_exact)
                         / math.log(MAX_DISTANCE / max_exact) * (half - max_exact)).astype(jnp.int32)
    large = jnp.minimum(large, half - 1)
    return ret + jnp.where(n < max_exact, n, large)


def _attn_bias_tables(rel_bias, tq):
    assert tq >= MAX_DISTANCE
    rb = rel_bias.astype(F32)
    far = rb[NUM_BUCKETS // 2 - 1]
    buckets = jnp.arange(NUM_BUCKETS)[:, None]

    def bias_of(rel):
        onehot = _t5_bucket(rel)[:, :, None, None] == buckets
        return jnp.sum(jnp.where(onehot, rb, 0.0), axis=2) - far

    kj = jnp.arange(tq)[:, None]
    qi = jnp.arange(tq)[None, :]
    diag = jnp.where(((kj // CHUNK) <= (qi // CHUNK))[..., None], bias_of(kj - qi), NEG_INF)
    prev = bias_of(kj - tq - qi)
    tabs = jnp.stack([jnp.transpose(diag, (2, 0, 1)), jnp.transpose(prev, (2, 0, 1))], axis=1)
    tabs = jnp.where(tabs > 0.5 * NEG_INF, tabs * LOG2E, NEG_INF)
    return jnp.concatenate([tabs, tabs], axis=3)


def _attn_kernel(qt_ref, k_ref, vt_ref, nb_ref, lq1_ref, lk1_ref, lq2_ref, lk2_ref, g_ref, o_ref,
                 qq_s, sa_s, sb_s, p_s, m_s, l_s, a_s, acc_s, *, lambda_init):
    i = pl.program_id(1)
    tq = qt_ref.shape[2]
    dq = DIFF_QK_DIM
    qt = qt_ref[0]
    feat = lax.broadcasted_iota(jnp.int32, qt.shape, 0)
    zero = jnp.zeros_like(qt)
    qq = jnp.concatenate([jnp.where(feat < dq, qt, zero), jnp.where(feat >= dq, qt, zero)], axis=1)

    qq_s[...] = qq
    m_s[...] = jnp.full(m_s.shape, -jnp.inf, F32)
    l_s[...] = jnp.zeros(l_s.shape, F32)
    acc_s[...] = jnp.zeros(acc_s.shape, F32)

    def scores(s_ref, j, nblk):
        rows = nblk * tq
        kb = k_ref[pl.ds(pl.multiple_of(j * tq, tq), rows), :]
        s_ref[:rows, :] = jnp.dot(kb, qq_s[...], preferred_element_type=F32)

    def absorb(s_ref, j, nblk, bias=None):
        rows = nblk * tq
        for c in range(2 * tq // 128):
            cs = slice(c * 128, (c + 1) * 128)
            s = s_ref[:rows, cs]
            if bias is not None:
                s = s + bias(cs)
            m_old = m_s[:, cs]
            m_new = jnp.maximum(m_old, jnp.max(s, axis=0, keepdims=True))
            alpha = jnp.exp2(m_old - m_new)
            p = jnp.exp2(s - m_new)
            l_s[:, cs] = alpha * l_s[:, cs] + jnp.sum(p, axis=0, keepdims=True)
            m_s[:, cs] = m_new
            a_s[:, cs] = alpha
            p_s[:rows, cs] = p.astype(BF16)
        vt = jnp.concatenate([vt_ref[j + b] for b in range(nblk)], axis=1)
        acc_s[...] = a_s[...] * acc_s[...] + jnp.dot(vt, p_s[:rows, :], preferred_element_type=F32)

    scores(sa_s, i, 1)
    absorb(sa_s, i, 1, lambda cs: nb_ref[0, 0, :, cs])
    jp = jnp.maximum(i - 1, 0)
    first = jnp.where(i > 0, 0.0, NEG_INF)
    scores(sa_s, jp, 1)
    absorb(sa_s, jp, 1, lambda cs: nb_ref[0, 1, :, cs] + first)
    n_far = jnp.maximum(i - 1, 0)
    n_single = n_far % 2
    n_head = n_far % 4

    @pl.when(n_single == 1)
    def _():
        scores(sb_s, 0, 1)
        absorb(sb_s, 0, 1)

    @pl.when(n_head >= 2)
    def _():
        scores(sb_s, n_single, 2)
        absorb(sb_s, n_single, 2)

    scores(sa_s, n_head, 2)

    def quad(qd, c):
        j0 = n_head + 4 * qd
        scores(sb_s, j0 + 2, 2)
        absorb(sa_s, j0, 2)
        scores(sa_s, j0 + 4, 2)
        absorb(sb_s, j0 + 2, 2)
        return c

    lax.fori_loop(0, n_far // 4, quad, 0)
    o = acc_s[...] / l_s[...]
    lam = (jnp.exp(jnp.sum(lq1_ref[...] * lk1_ref[...], keepdims=True))
           - jnp.exp(jnp.sum(lq2_ref[...] * lk2_ref[...], keepdims=True)) + lambda_init)
    out = o[:, :tq] - lam * o[:, tq:]
    out = out * lax.rsqrt(jnp.mean(jnp.square(out), axis=0, keepdims=True) + EPS) * g_ref[...]
    o_ref[...] = (out * (1.0 - lambda_init)).T.astype(o_ref.dtype)


def _diff_attention(qvt, kmat, nb, lq1, lk1, lq2, lk2, subln_g, lambda_init):
    nq, _, tq = qvt.shape
    assert nq >= 2
    L = kmat.shape[0]
    H, dv = DIFF_HEADS, DIFF_V_DIM
    vec = lambda a: a.reshape(1, -1).astype(F32)
    small = lambda n: pl.BlockSpec((1, n), lambda h, i: (0, 0))
    return pl.pallas_call(
        functools.partial(_attn_kernel, lambda_init=lambda_init),
        grid=(H, nq),
        in_specs=[pl.BlockSpec((1, dv, tq), lambda h, i: (i, h, 0)),
                  pl.BlockSpec((L, dv), lambda h, i: (0, h)),
                  pl.BlockSpec((nq, dv, tq), lambda h, i: (0, H + h, 0)),
                  pl.BlockSpec((1, 2, tq, 2 * tq), lambda h, i: (h, 0, 0, 0)),
                  small(DIFF_QK_DIM), small(DIFF_QK_DIM), small(DIFF_QK_DIM), small(DIFF_QK_DIM),
                  pl.BlockSpec((dv, 1), lambda h, i: (0, 0))],
        out_specs=pl.BlockSpec((tq, dv), lambda h, i: (i, h)),
        out_shape=jax.ShapeDtypeStruct((L, H * dv), BF16),
        scratch_shapes=[pltpu.VMEM((dv, 2 * tq), BF16),
                        pltpu.VMEM((2 * tq, 2 * tq), F32), pltpu.VMEM((2 * tq, 2 * tq), F32),
                        pltpu.VMEM((2 * tq, 2 * tq), BF16),
                        pltpu.VMEM((1, 2 * tq), F32), pltpu.VMEM((1, 2 * tq), F32), pltpu.VMEM((1, 2 * tq), F32),
                        pltpu.VMEM((dv, 2 * tq), F32)],
        compiler_params=_cparams(("parallel", "arbitrary")),
        name="diff_attention",
    )(qvt, kmat, qvt, nb, vec(lq1), vec(lk1), vec(lq2), vec(lk2), subln_g.reshape(dv, 1).astype(F32))


def _outproj_kernel(abc_ref, d_ref, x_ref, wa_ref, wd_ref, g_ref, b_ref, rhi_ref, rlo_ref,
                    x1_ref, x1b_ref, x1p_ref, lg_ref, *, alpha):
    mix = (jnp.dot(abc_ref[...], wa_ref[...], preferred_element_type=F32)
           + jnp.dot(d_ref[...], wd_ref[...], preferred_element_type=F32))
    x1 = _ln(alpha * x_ref[...] + mix, g_ref[...], b_ref[...])
    x1_ref[...] = x1
    hi = x1.astype(BF16)
    x1b_ref[...] = hi
    _store_token_tiles(x1p_ref, x1)
    lo = (x1 - hi.astype(F32)).astype(BF16)
    nt = (((1,), (1,)), ((), ()))
    lg_ref[...] = (lax.dot_general(rhi_ref[...], hi, nt, preferred_element_type=F32)
                   + lax.dot_general(rhi_ref[...], lo, nt, preferred_element_type=F32)
                   + lax.dot_general(rlo_ref[...], hi, nt, preferred_element_type=F32))


def _outproj(abc, d_out, x, w_out, ln_g, ln_b, router_w, alpha, tm):
    L, D = x.shape
    E = router_w.shape[1]
    ka = abc.shape[1]
    kd = d_out.shape[1]
    rwt = router_w.astype(F32).T
    rhi = rwt.astype(BF16)
    rlo = (rwt - rhi.astype(F32)).astype(BF16)
    full = lambda shape: pl.BlockSpec(shape, lambda i: (0,) * len(shape))
    return pl.pallas_call(
        functools.partial(_outproj_kernel, alpha=alpha),
        grid=(L // tm,),
        in_specs=[pl.BlockSpec((tm, ka), lambda i: (i, 0)),
                  pl.BlockSpec((tm, kd), lambda i: (i, 0)),
                  pl.BlockSpec((tm, D), lambda i: (i, 0)),
                  pl.BlockSpec((ka, D), lambda i: (0, 0)),
                  pl.BlockSpec((kd, D), lambda i: (ka // kd, 0)),
                  full((1, D)), full((1, D)), full((E, D)), full((E, D))],
        out_specs=[pl.BlockSpec((tm, D), lambda i: (i, 0)),
                   pl.BlockSpec((tm, D), lambda i: (i, 0)),
                   pl.BlockSpec((tm * ROW_TILE, LANES), lambda i: (i, 0)),
                   pl.BlockSpec((E, tm), lambda i: (0, i))],
        out_shape=[jax.ShapeDtypeStruct((L, D), F32), jax.ShapeDtypeStruct((L, D), BF16),
                   jax.ShapeDtypeStruct((L * ROW_TILE, LANES), jnp.uint32), jax.ShapeDtypeStruct((E, L), F32)],
        compiler_params=_cparams(("parallel",)),
        name="outproj_ln1",
    )(abc, d_out, x, w_out, w_out, ln_g.reshape(1, D).astype(F32), ln_b.reshape(1, D).astype(F32), rhi, rlo)


def _router_kernel(lg_ref, bias_ref, tri_ref, e_ref, r_ref, w_ref, cnt_ref, carry_ref):
    i = pl.program_id(0)
    E, tn = lg_ref.shape
    ng = N_EXPERT_GROUPS
    gs_ = E // ng

    @pl.when(i == 0)
    def _():
        carry_ref[...] = jnp.zeros_like(carry_ref)

    s = jax.nn.sigmoid(lg_ref[...])
    sel = s + bias_ref[...]
    midx = lax.broadcasted_iota(jnp.int32, (gs_, tn), 0).astype(F32)
    rows, gscore = [], []
    for g in range(ng):
        rg = sel[g * gs_:(g + 1) * gs_, :]
        m1 = jnp.max(rg, axis=0, keepdims=True)
        first = jnp.min(jnp.where(rg == m1, midx, float(gs_)), axis=0, keepdims=True)
        m2 = jnp.max(jnp.where(midx == first, -jnp.inf, rg), axis=0, keepdims=True)
        rows.append(rg)
        gscore.append(m1 + m2)
    vals = []
    for g in range(ng):
        rank = jnp.zeros((1, tn), F32)
        for o in range(ng):
            if o != g:
                beats = (gscore[o] >= gscore[g]) if o < g else (gscore[o] > gscore[g])
                rank = rank + jnp.where(beats, 1.0, 0.0)
        vals.append(jnp.where(rank < TOPK_GROUPS, rows[g], -jnp.inf))
    val = jnp.concatenate(vals, axis=0)
    eidx = lax.broadcasted_iota(jnp.int32, val.shape, 0)
    erank = jnp.zeros(val.shape, F32)
    for e in range(E):
        other = val[e:e + 1, :]
        erank = erank + jnp.where(eidx > e, jnp.where(other >= val, 1.0, 0.0), jnp.where(other > val, 1.0, 0.0))
    chosen = erank < TOP_K
    wsel = jnp.where(chosen, s, 0.0)
    wn = wsel / (jnp.sum(wsel, axis=0, keepdims=True) + 1e-20) * ROUTED_SCALE
    chf = jnp.where(chosen, 1.0, 0.0)
    incl = jnp.dot(chf.astype(BF16), tri_ref[...], preferred_element_type=F32)
    base = carry_ref[...]
    pos = base + incl - chf
    carry_ref[...] = base + incl[:, tn - 1:tn]
    cnt_ref[...] = (base + incl[:, tn - 1:tn]).astype(jnp.int32)
    eidf = eidx.astype(F32)
    cand = jnp.where(chosen, eidf, float(E))
    for k in range(TOP_K):
        ek = jnp.min(cand, axis=0, keepdims=True)
        hit = cand == ek
        e_ref[k:k + 1, :] = ek.astype(jnp.int32)
        r_ref[k:k + 1, :] = jnp.sum(jnp.where(hit, pos, 0.0), axis=0, keepdims=True).astype(jnp.int32)
        w_ref[k:k + 1, :] = jnp.sum(jnp.where(hit, wn, 0.0), axis=0, keepdims=True)
        cand = jnp.where(hit, float(E), cand)


def _router(logits_t, router_bias, tn):
    E, L = logits_t.shape
    tri = (jnp.arange(tn)[:, None] <= jnp.arange(tn)[None, :]).astype(BF16)
    slot = lambda dt: jax.ShapeDtypeStruct((TOP_K, L), dt)
    return pl.pallas_call(
        _router_kernel,
        grid=(L // tn,),
        in_specs=[pl.BlockSpec((E, tn), lambda i: (0, i)),
                  pl.BlockSpec((E, 1), lambda i: (0, 0)),
                  pl.BlockSpec((tn, tn), lambda i: (0, 0))],
        out_specs=[pl.BlockSpec((TOP_K, tn), lambda i: (0, i)),
                   pl.BlockSpec((TOP_K, tn), lambda i: (0, i)),
                   pl.BlockSpec((TOP_K, tn), lambda i: (0, i)),
                   pl.BlockSpec((E, 1), lambda i: (0, 0))],
        out_shape=[slot(jnp.int32), slot(jnp.int32), slot(F32), jax.ShapeDtypeStruct((E, 1), jnp.int32)],
        scratch_shapes=[pltpu.VMEM((E, 1), F32)],
        compiler_params=_cparams(("arbitrary",)),
        name="router_topk",
    )(logits_t, router_bias.reshape(E, 1).astype(F32), tri)


def _dispatch_kernel(dest_ref, pad_ref, x_ref, xs_ref, zero_ref, sem, zsem):
    i = pl.program_id(0)
    tm = x_ref.shape[0] // ROW_TILE

    def tile(ref, t):
        return ref.at[pl.ds(pl.multiple_of(t * ROW_TILE, ROW_TILE), ROW_TILE), :]

    def row_copy(r, k):
        return pltpu.make_async_copy(tile(x_ref, r), tile(xs_ref, dest_ref[k, r]), sem)

    def zero_copy(dst):
        return pltpu.make_async_copy(zero_ref, tile(xs_ref, dst), zsem)

    @pl.when(i == 0)
    def _():
        zero_ref[...] = jnp.zeros_like(zero_ref)

        def per_expert(e, c):
            lo, hi = pad_ref[0, e], pad_ref[1, e]
            lax.fori_loop(lo, hi, lambda d, c2: (zero_copy(d).start(), c2)[1], 0)
            lax.fori_loop(lo, hi, lambda d, c2: (zero_copy(d).wait(), c2)[1], 0)
            return c
        lax.fori_loop(0, N_EXPERTS, per_expert, 0)

    def issue(r, c):
        for k in range(TOP_K):
            row_copy(r, k).start()
        return c
    lax.fori_loop(0, tm, issue, 0)

    def drain(r, c):
        for k in range(TOP_K):
            row_copy(r, k).wait()
        return c
    lax.fori_loop(0, tm, drain, 0)


def _dispatch(x1p, dest, pad_lo_hi, n_slots, tm):
    L = x1p.shape[0] // ROW_TILE
    return pl.pallas_call(
        _dispatch_kernel,
        grid=(L // tm,),
        in_specs=[pl.BlockSpec((TOP_K, tm), lambda i: (0, i), memory_space=pltpu.SMEM),
                  pl.BlockSpec(memory_space=pltpu.SMEM),
                  pl.BlockSpec((tm * ROW_TILE, LANES), lambda i: (i, 0))],
        out_specs=pl.BlockSpec(memory_space=pl.ANY),
        out_shape=jax.ShapeDtypeStruct((n_slots * ROW_TILE, LANES), x1p.dtype),
        scratch_shapes=[pltpu.VMEM((ROW_TILE, LANES), x1p.dtype), pltpu.SemaphoreType.DMA(()),
                        pltpu.SemaphoreType.DMA(())],
        compiler_params=_cparams(("arbitrary",)),
        name="moe_dispatch",
    )(dest, pad_lo_hi, x1p)


def _experts_kernel(be_ref, nb_ref, xs_ref, wgu_ref, wd_ref, y_ref, wgu_b, wd_b):
    b = pl.program_id(0)

    @pl.when((b == 0) | (be_ref[b] != be_ref[jnp.maximum(b - 1, 0)]))
    def _():
        wgu_b[...] = wgu_ref[0, 0].astype(BF16)
        wd_b[...] = wd_ref[0, 0].astype(BF16)

    @pl.when(b < nb_ref[0])
    def _():
        de = wd_b.shape[0]
        xa, xb = _load_token_tiles(xs_ref, xs_ref.shape[0] // ROW_TILE)
        x = jnp.concatenate([xa.astype(BF16), xb.astype(BF16)], axis=1)
        h = jnp.dot(x, wgu_b[...], preferred_element_type=F32)
        a = jax.nn.silu(h[:, :de]) * h[:, de:]
        _store_token_tiles(y_ref, jnp.dot(a.astype(BF16), wd_b[...], preferred_element_type=F32))


def _experts(xs, block_e, n_used, w_gu, w_down, layer, blk):
    n_slots = xs.shape[0] // ROW_TILE
    nblk = n_slots // blk
    rows = blk * ROW_TILE
    D = w_gu.shape[2]
    de2 = w_gu.shape[3]
    de = w_down.shape[2]
    last = lambda b, be, nb: jnp.minimum(b, nb[0] - 1)
    return pl.pallas_call(
        _experts_kernel,
        grid_spec=pltpu.PrefetchScalarGridSpec(
            num_scalar_prefetch=2,
            grid=(nblk,),
            in_specs=[pl.BlockSpec((rows, LANES), lambda b, be, nb: (last(b, be, nb), 0)),
                      pl.BlockSpec((1, 1, D, de2), lambda b, be, nb: (layer, be[b], 0, 0)),
                      pl.BlockSpec((1, 1, de, D), lambda b, be, nb: (layer, be[b], 0, 0))],
            out_specs=pl.BlockSpec((rows, LANES), lambda b, be, nb: (last(b, be, nb), 0)),
            scratch_shapes=[pltpu.VMEM((D, de2), BF16), pltpu.VMEM((de, D), BF16)]),
        out_shape=jax.ShapeDtypeStruct(xs.shape, jnp.uint32),
        compiler_params=_cparams(("arbitrary",)),
        name="moe_experts",
    )(block_e, n_used, xs, w_gu, w_down)


def _combine_kernel(dcur_ref, dnxt_ref, y_ref, w_ref, x1_ref, x1b_ref, sgu_ref, sdn_ref, g_ref, b_ref,
                    x2_ref, x2b_ref, buf, sem, *, alpha):
    i = pl.program_id(0)
    n = pl.num_programs(0)
    tm = x1_ref.shape[0]
    slot = i % 2

    def row_copy(d_ref, s, r, k):
        src = y_ref.at[pl.ds(pl.multiple_of(d_ref[k, r] * ROW_TILE, ROW_TILE), ROW_TILE), :]
        dst = buf.at[s, k, pl.ds(pl.multiple_of(r * ROW_TILE, ROW_TILE), ROW_TILE), :]
        return pltpu.make_async_copy(src, dst, sem.at[s])

    def issue(d_ref, s):
        def body(r, c):
            for k in range(TOP_K):
                row_copy(d_ref, s, r, k).start()
            return c
        lax.fori_loop(0, tm, body, 0)

    @pl.when(i == 0)
    def _():
        issue(dcur_ref, 0)

    @pl.when(i + 1 < n)
    def _():
        issue(dnxt_ref, 1 - slot)

    de = sdn_ref.shape[0]
    h = jnp.dot(x1b_ref[...], sgu_ref[...], preferred_element_type=F32)
    a = jax.nn.silu(h[:, :de]) * h[:, de:]
    ffn = jnp.dot(a.astype(BF16), sdn_ref[...], preferred_element_type=F32)

    def drain(r, c):
        for k in range(TOP_K):
            row_copy(dcur_ref, slot, r, k).wait()
        return c
    lax.fori_loop(0, tm, drain, 0)

    half = ffn.shape[1] // 2
    fa, fb = ffn[:, :half], ffn[:, half:]
    for k in range(TOP_K):
        ya, yb = _load_token_tiles(buf.at[slot, k], tm)
        w = w_ref[:, k:k + 1]
        fa = fa + ya * w
        fb = fb + yb * w
    ffn = jnp.concatenate([fa, fb], axis=1)
    x2 = _ln(alpha * x1_ref[...] + ffn, g_ref[...], b_ref[...])
    x2_ref[...] = x2
    x2b_ref[...] = x2.astype(BF16)


def _combine(dest, y, w_t, x1, x1b, sh_gu, sh_down, ln_g, ln_b, alpha, tm):
    L, D = x1.shape
    n = L // tm
    full = lambda shape: pl.BlockSpec(shape, lambda i: (0,) * len(shape))
    return pl.pallas_call(
        functools.partial(_combine_kernel, alpha=alpha),
        grid=(n,),
        in_specs=[pl.BlockSpec((TOP_K, tm), lambda i: (0, i), memory_space=pltpu.SMEM),
                  pl.BlockSpec((TOP_K, tm), lambda i: (0, jnp.minimum(i + 1, n - 1)), memory_space=pltpu.SMEM),
                  pl.BlockSpec(memory_space=pl.ANY),
                  pl.BlockSpec((tm, TOP_K), lambda i: (i, 0)),
                  pl.BlockSpec((tm, D), lambda i: (i, 0)),
                  pl.BlockSpec((tm, D), lambda i: (i, 0)),
                  full(sh_gu.shape), full(sh_down.shape), full((1, D)), full((1, D))],
        out_specs=[pl.BlockSpec((tm, D), lambda i: (i, 0)), pl.BlockSpec((tm, D), lambda i: (i, 0))],
        out_shape=[jax.ShapeDtypeStruct((L, D), F32), jax.ShapeDtypeStruct((L, D), BF16)],
        scratch_shapes=[pltpu.VMEM((2, TOP_K, tm * ROW_TILE, LANES), y.dtype), pltpu.SemaphoreType.DMA((2,))],
        compiler_params=_cparams(("arbitrary",)),
        name="moe_combine_ln2",
    )(dest, dest, y, w_t, x1, x1b, sh_gu, sh_down, ln_g.reshape(1, D).astype(F32), ln_b.reshape(1, D).astype(F32))


MOE_BLK = 256


def _moe_layer(x1, x1b, x1p, logits_t, router_bias, w_gu, w_down, layer, sh_gu, sh_down, ln_g, ln_b, alpha,
               router_tn, dispatch_tm, combine_tm):
    L, D = x1.shape
    E = N_EXPERTS
    e_k, r_k, w_k, counts = _router(logits_t, router_bias, router_tn)
    counts = counts.reshape(E)
    padded = (counts + MOE_BLK - 1) // MOE_BLK * MOE_BLK
    pends = jnp.cumsum(padded)
    pstarts = pends - padded
    dest = jnp.sum(jnp.where(e_k[..., None] == jnp.arange(E), pstarts, 0), axis=-1) + r_k
    nblk = -(-(L * TOP_K) // MOE_BLK) + E
    n_used = (pends[-1] // MOE_BLK).astype(jnp.int32)
    blocks = jnp.minimum(jnp.arange(nblk, dtype=jnp.int32), n_used - 1)
    block_e = jnp.sum((pends[None, :] <= (blocks * MOE_BLK)[:, None]).astype(jnp.int32), axis=1)
    block_e = jnp.minimum(block_e, E - 1).astype(jnp.int32)
    pad_lo_hi = jnp.stack([pstarts + counts, pends]).astype(jnp.int32)
    xs = _dispatch(x1p, dest, pad_lo_hi, nblk * MOE_BLK, dispatch_tm)
    y = _experts(xs, block_e, n_used.reshape(1), w_gu, w_down, layer, MOE_BLK)
    return _combine(dest, y, w_k.T, x1, x1b, sh_gu, sh_down, ln_g, ln_b, alpha, combine_tm)


def _pick(n, pref):
    t = min(n, pref)
    assert n % t == 0
    return t


def kernel(x, w_in, w_out, mix_norm_g, s5_lambda_re, s5_lambda_im, s5_log_dt, s5_b_re, s5_b_im, s5_c_re, s5_c_im, s5_d, s5_w_glu, conv_w, sgu_ln_g, sgu_ln_b, sgu_w, sgu_b, diff_lq1, diff_lk1, diff_lq2, diff_lk2, diff_subln_g, rel_bias, ln1_g, ln1_b, router_w, router_bias, moe_w_gu, moe_w_down, shared_w_gu, shared_w_down, ln2_g, ln2_b):
    Bt, L, D = x.shape
    assert Bt == 1
    depth = w_in.shape[0]
    alpha = (2 * depth) ** 0.25
    gw = GROUP_W
    tq = _pick(L, 256)
    nb = _attn_bias_tables(rel_bias, tq)
    xf = x.reshape(L, D)
    xb = xf.astype(BF16)
    for l in range(depth):
        w_in_b = w_in[l].astype(BF16)
        proj_a = _matmul(xb, w_in_b[:, :6 * gw], _pick(L, 1024), 512, F32)
        kmat = _matmul(xb, w_in_b[:, 7 * gw:8 * gw], _pick(L, 1024), 512, BF16)
        w_q = (w_in[l][:, 6 * gw:7 * gw] * (DIFF_QK_DIM ** -0.5 * LOG2E)).astype(BF16)
        w_qv_t = jnp.concatenate([w_q, w_in_b[:, 8 * gw:]], axis=1).T
        qvt = _matmul_nt(w_qv_t, xb, tq, BF16)
        s5_rt = _pick(L // S5_SUB, 256)
        tabs = _s5_tables(s5_lambda_re[l], s5_lambda_im[l], s5_log_dt[l], s5_b_re[l], s5_b_im[l],
                          s5_c_re[l], s5_c_im[l], s5_rt)
        ys5 = _s5_scan(proj_a, tabs, s5_rt)
        abc = _mixers(proj_a, ys5, s5_d[l], s5_w_glu[l], conv_w[l], sgu_ln_g[l], sgu_ln_b[l], sgu_w[l], sgu_b[l],
                      mix_norm_g[l], _pick(L, 512))
        lambda_init = 0.8 - 0.6 * math.exp(-0.3 * l)
        d_out = _diff_attention(qvt, kmat, nb, diff_lq1[l], diff_lk1[l], diff_lq2[l], diff_lk2[l],
                                diff_subln_g[l], lambda_init)
        x1, x1b, x1p, logits_t = _outproj(abc, d_out, xf, w_out[l].astype(BF16), ln1_g[l], ln1_b[l], router_w[l],
                                          alpha, _pick(L, 256))
        xf, xb = _moe_layer(x1, x1b, x1p, logits_t, router_bias[l], moe_w_gu, moe_w_down, l,
                            shared_w_gu[l].astype(BF16), shared_w_down[l].astype(BF16),
                            ln2_g[l], ln2_b[l], alpha, _pick(L, 512), _pick(L, 256), _pick(L, 256))
    return xf.reshape(Bt, L, D)
```

```python
import functools
import math

import jax
import jax.numpy as jnp
from jax import lax
from jax.experimental import pallas as pl
from jax.experimental.pallas import tpu as pltpu

F32 = jnp.float32
BF16 = jnp.bfloat16

GROUP_W = 512
CHUNK = 64
S5_GROUP_CH = 16
S5_GROUPS = 32
S5_STATE = 64
S5_SUB = 16
S5_KBLOCKS = 4
SGU_BLK = 128
SGU_HEADS = 4
DIFF_HEADS = 4
DIFF_QK_DIM = 64
DIFF_V_DIM = 128
NUM_BUCKETS = 32
MAX_DISTANCE = 128
N_EXPERTS = 64
TOP_K = 8
N_EXPERT_GROUPS = 8
TOPK_GROUPS = 4
ROUTED_SCALE = 2.5
EPS = 1e-5
NEG_INF = -1e30
LOG2E = math.log2(math.e)
ONES_ROWS = 16

VMEM_LIMIT = 56 * 1024 * 1024


def _cparams(sem):
    return pltpu.CompilerParams(dimension_semantics=sem, vmem_limit_bytes=VMEM_LIMIT)


def _rms(x, g):
    return x * lax.rsqrt(jnp.mean(jnp.square(x), -1, keepdims=True) + EPS) * g


def _pack_bf16_pairs(x):
    c = x.shape[1] // 2
    hi = lax.bitcast_convert_type(x[:, :c].astype(BF16).astype(F32), jnp.uint32)
    lo = lax.bitcast_convert_type(x[:, c:].astype(BF16).astype(F32), jnp.uint32)
    return hi | (lo >> 16)


def _unpack_bf16_pairs(u):
    hi = lax.bitcast_convert_type(u & jnp.uint32(0xFFFF0000), F32)
    lo = lax.bitcast_convert_type(u << 16, F32)
    return hi, lo


ROW_TILE = 8
LANES = 128


def _store_token_tiles(ref, x):
    p = _pack_bf16_pairs(x)
    n = x.shape[0]
    for c in range(ROW_TILE):
        ref[pl.ds(c, n, stride=ROW_TILE), :] = p[:, c * LANES:(c + 1) * LANES]


def _load_token_tiles(ref, n):
    p = jnp.concatenate([ref[pl.ds(c, n, stride=ROW_TILE), :] for c in range(ROW_TILE)], axis=1)
    return _unpack_bf16_pairs(p)


def _ln(x, g, b):
    mu = jnp.mean(x, -1, keepdims=True)
    var = jnp.mean(jnp.square(x - mu), -1, keepdims=True)
    return (x - mu) * lax.rsqrt(var + EPS) * g + b


def _matmul_kernel(x_ref, w_ref, o_ref):
    o_ref[...] = jnp.dot(x_ref[...], w_ref[...], preferred_element_type=F32).astype(o_ref.dtype)


def _matmul(x, w, tm, tn, out_dtype):
    M, K = x.shape
    N = w.shape[1]
    return pl.pallas_call(
        _matmul_kernel,
        grid=(M // tm, N // tn),
        in_specs=[pl.BlockSpec((tm, K), lambda i, j: (i, 0)),
                  pl.BlockSpec((K, tn), lambda i, j: (0, j))],
        out_specs=pl.BlockSpec((tm, tn), lambda i, j: (i, j)),
        out_shape=jax.ShapeDtypeStruct((M, N), out_dtype),
        compiler_params=_cparams(("parallel", "arbitrary")),
        name="proj_matmul",
    )(x, w)


def _matmul_nt_kernel(w_ref, x_ref, o_ref):
    o_ref[0] = lax.dot_general(w_ref[...], x_ref[...], (((1,), (1,)), ((), ())),
                               preferred_element_type=F32).astype(o_ref.dtype)


def _matmul_nt(w_t, x, tm, out_dtype):
    M, K = x.shape
    N = w_t.shape[0]
    return pl.pallas_call(
        _matmul_nt_kernel,
        grid=(M // tm,),
        in_specs=[pl.BlockSpec((N, K), lambda i: (0, 0)),
                  pl.BlockSpec((tm, K), lambda i: (i, 0))],
        out_specs=pl.BlockSpec((1, N, tm), lambda i: (i, 0, 0)),
        out_shape=jax.ShapeDtypeStruct((M // tm, N, tm), out_dtype),
        compiler_params=_cparams(("parallel",)),
        name="proj_matmul_nt",
    )(w_t, x)


def _s5_tables(lam_re, lam_im, log_dt, b_re, b_im, c_re, c_im, n_rows):
    G, P, H, S = S5_GROUPS, S5_STATE, S5_GROUP_CH, S5_SUB
    hp = lax.Precision.HIGHEST
    dt = jnp.exp(log_dt.astype(F32))[:, None]
    lam = lax.complex(lam_re.astype(F32), lam_im.astype(F32))
    ldt = lam * dt
    lam_bar = jnp.exp(ldt)
    b_bar = ((lam_bar - 1.0) / lam)[..., None] * lax.complex(b_re.astype(F32), b_im.astype(F32))
    c = lax.complex(c_re.astype(F32), c_im.astype(F32))
    tau = jnp.arange(S + 1, dtype=F32)
    pows = jnp.exp(ldt[None] * tau[:, None, None])
    KB, GL = S5_KBLOCKS, S5_GROUPS // S5_KBLOCKS
    eye = jnp.eye(GL, dtype=bool)
    w1 = pows[:S][::-1][:, :, None, :] * jnp.transpose(b_bar, (0, 2, 1))[None]
    w1 = jnp.transpose(w1.reshape(S, KB, GL, H, P), (1, 0, 2, 3, 4))
    w1 = jnp.where(eye[None, None, :, None, :, None], w1[:, :, :, :, None, :], 0.0)
    w1 = w1.reshape(KB, S * GL * H, GL * P)
    w2 = jnp.transpose(c, (0, 2, 1))[:, :, None, :] * jnp.transpose(pows[1:], (1, 2, 0))[..., None]
    w2 = w2.reshape(KB, GL, P, S, H)
    w2 = jnp.where(eye[None, :, None, None, :, None], w2[:, :, :, :, None, :], 0.0)
    w2 = w2.reshape(KB, GL * P, S * GL * H)
    kc = jnp.real(jnp.einsum('ghp,tgp,gpi->tghi', c, pows[:S], b_bar, precision=hp))
    kc = jnp.transpose(kc.reshape(S, KB, GL, H, H), (1, 2, 4, 0, 3))
    kc = jnp.where(eye[None, :, None, None, :, None], kc[:, :, :, :, None, :], 0.0)
    kcat = kc.reshape(KB, GL * H, S * GL * H)

    nstep = max(1, (n_rows - 1).bit_length())
    kk = (S * (2 ** jnp.arange(nstep))).astype(F32)
    lp = jnp.exp(ldt[None] * kk[:, None, None])
    lp = jnp.transpose(lp.reshape(nstep, KB, GL * P), (1, 0, 2))
    lampow = jnp.stack([jnp.real(lp), jnp.imag(lp)], axis=2)
    return dict(
        kcat=kcat.astype(BF16),
        w1re=jnp.real(w1).astype(BF16), w1im=jnp.imag(w1).astype(BF16),
        w2re=jnp.real(w2).astype(BF16), w2im=(-jnp.imag(w2)).astype(BF16),
        lampow=lampow.astype(F32))


def _s5_kernel(u_ref, w1re_ref, w1im_ref, w2re_ref, w2im_ref, kcat_ref, lp_ref, o_ref,
               ucat_ref, yall_ref, carry_ref, *, nstep):
    t = pl.program_id(1)
    S = S5_SUB
    R = ucat_ref.shape[0]
    W = u_ref.shape[1]

    @pl.when(t == 0)
    def _():
        carry_ref[...] = jnp.zeros_like(carry_ref)

    for j in range(S):
        ucat_ref[:, j * W:(j + 1) * W] = u_ref[pl.ds(j, R, stride=S), :].astype(BF16)
    ucat = ucat_ref[...]
    xre = jnp.dot(ucat, w1re_ref[0], preferred_element_type=F32)
    xim = jnp.dot(ucat, w1im_ref[0], preferred_element_type=F32)
    row = lax.broadcasted_iota(jnp.int32, xre.shape, 0)
    cre, cim = carry_ref[0], carry_ref[1]
    lr, li = lp_ref[0, 0, 0:1, :], lp_ref[0, 0, 1:2, :]
    xre = xre + jnp.where(row == 0, lr * cre - li * cim, 0.0)
    xim = xim + jnp.where(row == 0, lr * cim + li * cre, 0.0)
    for k in range(nstep):
        sh = 1 << k
        pre = pltpu.roll(xre, sh, 0)
        pim = pltpu.roll(xim, sh, 0)
        lr, li = lp_ref[0, k, 0:1, :], lp_ref[0, k, 1:2, :]
        keep = row >= sh
        xre, xim = (xre + jnp.where(keep, lr * pre - li * pim, 0.0),
                    xim + jnp.where(keep, lr * pim + li * pre, 0.0))
    carry_ref[0] = xre[R - 1:R, :]
    carry_ref[1] = xim[R - 1:R, :]
    sre = jnp.where(row >= 1, pltpu.roll(xre, 1, 0), cre).astype(BF16)
    sim = jnp.where(row >= 1, pltpu.roll(xim, 1, 0), cim).astype(BF16)
    yall_ref[...] = (jnp.dot(sre, w2re_ref[0], preferred_element_type=F32)
                     + jnp.dot(sim, w2im_ref[0], preferred_element_type=F32))
    for j in range(S):
        yall_ref[:, j * W:] += jnp.dot(ucat_ref[:, j * W:(j + 1) * W], kcat_ref[0, :, :(S - j) * W],
                                       preferred_element_type=F32)
    for j in range(S):
        o_ref[pl.ds(j, R, stride=S), :] = yall_ref[:, j * W:(j + 1) * W]


def _s5_scan(proj_a, tabs, rt):
    L = proj_a.shape[0]
    S, KB = S5_SUB, S5_KBLOCKS
    W = GROUP_W // KB
    nstep = tabs['lampow'].shape[1]
    P2 = tabs['lampow'].shape[3]
    rows = rt * S
    kb3 = lambda a: pl.BlockSpec((1,) + a.shape[1:], lambda k, t: (k, 0, 0))
    return pl.pallas_call(
        functools.partial(_s5_kernel, nstep=nstep),
        grid=(KB, L // rows),
        in_specs=[pl.BlockSpec((rows, W), lambda k, t: (t, k)),
                  kb3(tabs['w1re']), kb3(tabs['w1im']), kb3(tabs['w2re']), kb3(tabs['w2im']), kb3(tabs['kcat']),
                  pl.BlockSpec((1, nstep, 2, P2), lambda k, t: (k, 0, 0, 0))],
        out_specs=pl.BlockSpec((rows, W), lambda k, t: (t, k)),
        out_shape=jax.ShapeDtypeStruct((L, GROUP_W), F32),
        scratch_shapes=[pltpu.VMEM((rt, S * W), BF16), pltpu.VMEM((rt, S * W), F32), pltpu.VMEM((2, 1, P2), F32)],
        compiler_params=_cparams(("parallel", "arbitrary")),
        name="s5_scan",
    )(proj_a, tabs['w1re'], tabs['w1im'], tabs['w2re'], tabs['w2im'], tabs['kcat'], tabs['lampow'])


def _mixers_kernel(s5u_ref, cb_ref, cc_ref, ch_ref, su_ref, sv_ref, cch_ref, chh_ref, ys_ref,
                   d_ref, wglu_ref, cw_ref, lng_ref, lnb_ref, ws_ref, bs_ref, g_ref, o_ref):
    i = pl.program_id(0)
    tm = o_ref.shape[0]
    gw = GROUP_W
    y = ys_ref[...] + d_ref[...] * s5u_ref[...]
    y = jax.nn.gelu(y)
    y = y * jax.nn.sigmoid(jnp.dot(y.astype(BF16), wglu_ref[...], preferred_element_type=F32))
    o_ref[:, 0:gw] = _rms(y, g_ref[0:1, :]).astype(o_ref.dtype)
    z = cc_ref[...] * ch_ref[...]
    zh = jnp.where(i > 0, cch_ref[...] * chh_ref[...], 0.0)
    row = lax.broadcasted_iota(jnp.int32, z.shape, 0)
    z1 = jnp.where(row == 0, zh[7:8, :], pltpu.roll(z, 1, 0))
    z2 = jnp.where(row == 0, zh[6:7, :], jnp.where(row == 1, zh[7:8, :], pltpu.roll(z, 2, 0)))
    conv = cw_ref[0:1, :] * z2 + cw_ref[1:2, :] * z1 + cw_ref[2:3, :] * z
    o_ref[:, gw:2 * gw] = _rms(cb_ref[...] * conv, g_ref[1:2, :]).astype(o_ref.dtype)
    uu = jax.nn.gelu(su_ref[...])
    vv = _ln(jax.nn.gelu(sv_ref[...]), lng_ref[...], lnb_ref[...]).astype(BF16)
    pi = lax.broadcasted_iota(jnp.int32, (SGU_BLK, SGU_BLK), 0)
    pj = lax.broadcasted_iota(jnp.int32, (SGU_BLK, SGU_BLK), 1)
    causal = (pj // CHUNK) <= (pi // CHUNK)
    hd = gw // SGU_HEADS
    ws = [jnp.where(causal, ws_ref[h], 0.0).astype(BF16) for h in range(SGU_HEADS)]
    blocks = []
    for n in range(tm // SGU_BLK):
        vb = vv[n * SGU_BLK:(n + 1) * SGU_BLK, :]
        blocks.append(jnp.concatenate(
            [jnp.dot(ws[h], vb[:, h * hd:(h + 1) * hd], preferred_element_type=F32) for h in range(SGU_HEADS)],
            axis=1) + bs_ref[...])
    mixed = jnp.concatenate(blocks, axis=0)
    o_ref[:, 2 * gw:3 * gw] = _rms(uu * mixed, g_ref[2:3, :]).astype(o_ref.dtype)


def _mixers(proj_a, ys5, s5_d, w_glu, conv_w, ln_g, ln_b, sgu_w, sgu_b, mix_g, tm):
    L = proj_a.shape[0]
    gw = GROUP_W
    hb = tm // 8
    col = lambda c: pl.BlockSpec((tm, gw), lambda i, c=c: (i, c))
    halo = lambda c: pl.BlockSpec((8, gw), lambda i, c=c: (jnp.maximum(i * hb - 1, 0), c))
    full = lambda a: pl.BlockSpec(a.shape, lambda i: (0,) * a.ndim)
    hd = gw // SGU_HEADS
    bs_full = jnp.repeat(sgu_b.astype(F32).T, hd, axis=1)
    consts = [s5_d.reshape(1, gw).astype(F32), w_glu.astype(BF16), conv_w.astype(F32),
              ln_g.reshape(1, gw).astype(F32), ln_b.reshape(1, gw).astype(F32), sgu_w.astype(F32),
              bs_full, mix_g.reshape(3, gw).astype(F32)]
    return pl.pallas_call(
        _mixers_kernel,
        grid=(L // tm,),
        in_specs=[col(0), col(1), col(2), col(3), col(4), col(5), halo(2), halo(3),
                  pl.BlockSpec((tm, gw), lambda i: (i, 0))] + [full(a) for a in consts],
        out_specs=pl.BlockSpec((tm, 3 * gw), lambda i: (i, 0)),
        out_shape=jax.ShapeDtypeStruct((L, 3 * gw), BF16),
        compiler_params=_cparams(("parallel",)),
        name="row_mixers",
    )(proj_a, proj_a, proj_a, proj_a, proj_a, proj_a, proj_a, proj_a, ys5, *consts)


def _t5_bucket(rel):
    half = NUM_BUCKETS // 2
    ret = jnp.where(rel > 0, half, 0)
    n = jnp.abs(rel)
    max_exact = half // 2
    large = max_exact + (jnp.log(jnp.maximum(n, 1).astype(F32) / max_exact)
                         / math.log(MAX_DISTANCE / max_exact) * (half - max_exact)).astype(jnp.int32)
    large = jnp.minimum(large, half - 1)
    return ret + jnp.where(n < max_exact, n, large)


def _attn_bias_tables(rel_bias, tq):
    assert tq >= MAX_DISTANCE
    rb = rel_bias.astype(F32)
    far = rb[NUM_BUCKETS // 2 - 1]
    buckets = jnp.arange(NUM_BUCKETS)[:, None]

    def bias_of(rel):
        onehot = _t5_bucket(rel)[:, :, None, None] == buckets
        return jnp.sum(jnp.where(onehot, rb, 0.0), axis=2) - far

    kj = jnp.arange(tq)[:, None]
    qi = jnp.arange(tq)[None, :]
    diag = jnp.where(((kj // CHUNK) <= (qi // CHUNK))[..., None], bias_of(kj - qi), NEG_INF)
    prev = bias_of(kj - tq - qi)
    tabs = jnp.stack([jnp.transpose(diag, (2, 0, 1)), jnp.transpose(prev, (2, 0, 1))], axis=1)
    tabs = jnp.where(tabs > 0.5 * NEG_INF, tabs * LOG2E, NEG_INF)
    return jnp.concatenate([tabs, tabs], axis=3)


def _attn_kernel(qt_ref, k_ref, vt_ref, nb_ref, lq1_ref, lk1_ref, lq2_ref, lk2_ref, g_ref, o_ref,
                 qq_s, sa_s, sb_s, p_s, m_s, a_s, acc_s, *, lambda_init):
    i = pl.program_id(1)
    tq = qt_ref.shape[2]
    dq = DIFF_QK_DIM
    qt = qt_ref[0]
    feat = lax.broadcasted_iota(jnp.int32, qt.shape, 0)
    zero = jnp.zeros_like(qt)
    qq = jnp.concatenate([jnp.where(feat < dq, qt, zero), jnp.where(feat >= dq, qt, zero)], axis=1)

    qq_s[...] = qq
    m_s[...] = jnp.full(m_s.shape, -jnp.inf, F32)
    acc_s[...] = jnp.zeros(acc_s.shape, F32)

    def scores(s_ref, j, nblk):
        rows = nblk * tq
        kb = k_ref[pl.ds(pl.multiple_of(j * tq, tq), rows), :]
        s_ref[:rows, :] = jnp.dot(kb, qq_s[...], preferred_element_type=F32)

    def absorb(s_ref, j, nblk, bias=None):
        rows = nblk * tq
        for c in range(2 * tq // 128):
            cs = slice(c * 128, (c + 1) * 128)
            s = s_ref[:rows, cs]
            if bias is not None:
                s = s + bias(cs)
            m_old = m_s[:, cs]
            m_new = jnp.maximum(m_old, jnp.max(s, axis=0, keepdims=True))
            m_s[:, cs] = m_new
            a_s[:, cs] = jnp.exp2(m_old - m_new)
            p_s[:rows, cs] = jnp.exp2((s - m_new).astype(BF16))
        vt = jnp.concatenate([vt_ref[j + b] for b in range(nblk)], axis=1)
        vt = jnp.concatenate([vt, jnp.ones((ONES_ROWS, rows), BF16)], axis=0)
        acc_s[...] = a_s[...] * acc_s[...] + jnp.dot(vt, p_s[:rows, :], preferred_element_type=F32)

    scores(sa_s, i, 1)
    absorb(sa_s, i, 1, lambda cs: nb_ref[0, 0, :, cs])
    jp = jnp.maximum(i - 1, 0)
    first = jnp.where(i > 0, 0.0, NEG_INF)
    scores(sa_s, jp, 1)
    absorb(sa_s, jp, 1, lambda cs: nb_ref[0, 1, :, cs] + first)
    n_far = jnp.maximum(i - 1, 0)
    n_single = n_far % 2
    n_head = n_far % 4

    @pl.when(n_single == 1)
    def _():
        scores(sb_s, 0, 1)
        absorb(sb_s, 0, 1)

    @pl.when(n_head >= 2)
    def _():
        scores(sb_s, n_single, 2)
        absorb(sb_s, n_single, 2)

    scores(sa_s, n_head, 2)

    def quad(qd, c):
        j0 = n_head + 4 * qd
        scores(sb_s, j0 + 2, 2)
        absorb(sa_s, j0, 2)
        scores(sa_s, j0 + 4, 2)
        absorb(sb_s, j0 + 2, 2)
        return c

    lax.fori_loop(0, n_far // 4, quad, 0)
    o = acc_s[:DIFF_V_DIM, :] / acc_s[DIFF_V_DIM:DIFF_V_DIM + 1, :]
    lam = (jnp.exp(jnp.sum(lq1_ref[...] * lk1_ref[...], keepdims=True))
           - jnp.exp(jnp.sum(lq2_ref[...] * lk2_ref[...], keepdims=True)) + lambda_init)
    out = o[:, :tq] - lam * o[:, tq:]
    out = out * lax.rsqrt(jnp.mean(jnp.square(out), axis=0, keepdims=True) + EPS) * g_ref[...]
    o_ref[...] = (out * (1.0 - lambda_init)).T.astype(o_ref.dtype)


def _diff_attention(qvt, kmat, nb, lq1, lk1, lq2, lk2, subln_g, lambda_init):
    nq, _, tq = qvt.shape
    assert nq >= 2
    L = kmat.shape[0]
    H, dv = DIFF_HEADS, DIFF_V_DIM
    vec = lambda a: a.reshape(1, -1).astype(F32)
    small = lambda n: pl.BlockSpec((1, n), lambda h, i: (0, 0))
    return pl.pallas_call(
        functools.partial(_attn_kernel, lambda_init=lambda_init),
        grid=(H, nq),
        in_specs=[pl.BlockSpec((1, dv, tq), lambda h, i: (i, h, 0)),
                  pl.BlockSpec((L, dv), lambda h, i: (0, h)),
                  pl.BlockSpec((nq, dv, tq), lambda h, i: (0, H + h, 0)),
                  pl.BlockSpec((1, 2, tq, 2 * tq), lambda h, i: (h, 0, 0, 0)),
                  small(DIFF_QK_DIM), small(DIFF_QK_DIM), small(DIFF_QK_DIM), small(DIFF_QK_DIM),
                  pl.BlockSpec((dv, 1), lambda h, i: (0, 0))],
        out_specs=pl.BlockSpec((tq, dv), lambda h, i: (i, h)),
        out_shape=jax.ShapeDtypeStruct((L, H * dv), BF16),
        scratch_shapes=[pltpu.VMEM((dv, 2 * tq), BF16),
                        pltpu.VMEM((2 * tq, 2 * tq), F32), pltpu.VMEM((2 * tq, 2 * tq), F32),
                        pltpu.VMEM((2 * tq, 2 * tq), BF16),
                        pltpu.VMEM((1, 2 * tq), F32), pltpu.VMEM((1, 2 * tq), F32),
                        pltpu.VMEM((dv + ONES_ROWS, 2 * tq), F32)],
        compiler_params=_cparams(("parallel", "arbitrary")),
        name="diff_attention",
    )(qvt, kmat, qvt, nb, vec(lq1), vec(lk1), vec(lq2), vec(lk2), subln_g.reshape(dv, 1).astype(F32))


def _outproj_kernel(abc_ref, d_ref, x_ref, wa_ref, wd_ref, g_ref, b_ref, rhi_ref, rlo_ref,
                    x1_ref, x1b_ref, x1p_ref, lg_ref, *, alpha):
    mix = (jnp.dot(abc_ref[...], wa_ref[...], preferred_element_type=F32)
           + jnp.dot(d_ref[...], wd_ref[...], preferred_element_type=F32))
    x1 = _ln(alpha * x_ref[...] + mix, g_ref[...], b_ref[...])
    x1_ref[...] = x1
    hi = x1.astype(BF16)
    x1b_ref[...] = hi
    _store_token_tiles(x1p_ref, x1)
    lo = (x1 - hi.astype(F32)).astype(BF16)
    nt = (((1,), (1,)), ((), ()))
    lg_ref[...] = (lax.dot_general(rhi_ref[...], hi, nt, preferred_element_type=F32)
                   + lax.dot_general(rhi_ref[...], lo, nt, preferred_element_type=F32)
                   + lax.dot_general(rlo_ref[...], hi, nt, preferred_element_type=F32))


def _outproj(abc, d_out, x, w_out, ln_g, ln_b, router_w, alpha, tm):
    L, D = x.shape
    E = router_w.shape[1]
    ka = abc.shape[1]
    kd = d_out.shape[1]
    rwt = router_w.astype(F32).T
    rhi = rwt.astype(BF16)
    rlo = (rwt - rhi.astype(F32)).astype(BF16)
    full = lambda shape: pl.BlockSpec(shape, lambda i: (0,) * len(shape))
    return pl.pallas_call(
        functools.partial(_outproj_kernel, alpha=alpha),
        grid=(L // tm,),
        in_specs=[pl.BlockSpec((tm, ka), lambda i: (i, 0)),
                  pl.BlockSpec((tm, kd), lambda i: (i, 0)),
                  pl.BlockSpec((tm, D), lambda i: (i, 0)),
                  pl.BlockSpec((ka, D), lambda i: (0, 0)),
                  pl.BlockSpec((kd, D), lambda i: (ka // kd, 0)),
                  full((1, D)), full((1, D)), full((E, D)), full((E, D))],
        out_specs=[pl.BlockSpec((tm, D), lambda i: (i, 0)),
                   pl.BlockSpec((tm, D), lambda i: (i, 0)),
                   pl.BlockSpec((tm * ROW_TILE, LANES), lambda i: (i, 0)),
                   pl.BlockSpec((E, tm), lambda i: (0, i))],
        out_shape=[jax.ShapeDtypeStruct((L, D), F32), jax.ShapeDtypeStruct((L, D), BF16),
                   jax.ShapeDtypeStruct((L * ROW_TILE, LANES), jnp.uint32), jax.ShapeDtypeStruct((E, L), F32)],
        compiler_params=_cparams(("parallel",)),
        name="outproj_ln1",
    )(abc, d_out, x, w_out, w_out, ln_g.reshape(1, D).astype(F32), ln_b.reshape(1, D).astype(F32), rhi, rlo)


def _router_kernel(lg_ref, bias_ref, tri_ref, e_ref, r_ref, w_ref, cnt_ref, carry_ref):
    i = pl.program_id(0)
    E, tn = lg_ref.shape
    ng = N_EXPERT_GROUPS
    gs_ = E // ng

    @pl.when(i == 0)
    def _():
        carry_ref[...] = jnp.zeros_like(carry_ref)

    s = jax.nn.sigmoid(lg_ref[...])
    sel = s + bias_ref[...]
    midx = lax.broadcasted_iota(jnp.int32, (gs_, tn), 0).astype(F32)
    rows, gscore = [], []
    for g in range(ng):
        rg = sel[g * gs_:(g + 1) * gs_, :]
        m1 = jnp.max(rg, axis=0, keepdims=True)
        first = jnp.min(jnp.where(rg == m1, midx, float(gs_)), axis=0, keepdims=True)
        m2 = jnp.max(jnp.where(midx == first, -jnp.inf, rg), axis=0, keepdims=True)
        rows.append(rg)
        gscore.append(m1 + m2)
    vals = []
    for g in range(ng):
        rank = jnp.zeros((1, tn), F32)
        for o in range(ng):
            if o != g:
                beats = (gscore[o] >= gscore[g]) if o < g else (gscore[o] > gscore[g])
                rank = rank + jnp.where(beats, 1.0, 0.0)
        vals.append(jnp.where(rank < TOPK_GROUPS, rows[g], -jnp.inf))
    val = jnp.concatenate(vals, axis=0)
    eidx = lax.broadcasted_iota(jnp.int32, val.shape, 0)
    erank = jnp.zeros(val.shape, F32)
    for e in range(E):
        other = val[e:e + 1, :]
        erank = erank + jnp.where(eidx > e, jnp.where(other >= val, 1.0, 0.0), jnp.where(other > val, 1.0, 0.0))
    chosen = erank < TOP_K
    wsel = jnp.where(chosen, s, 0.0)
    wn = wsel / (jnp.sum(wsel, axis=0, keepdims=True) + 1e-20) * ROUTED_SCALE
    chf = jnp.where(chosen, 1.0, 0.0)
    incl = jnp.dot(chf.astype(BF16), tri_ref[...], preferred_element_type=F32)
    base = carry_ref[...]
    pos = base + incl - chf
    carry_ref[...] = base + incl[:, tn - 1:tn]
    cnt_ref[...] = (base + incl[:, tn - 1:tn]).astype(jnp.int32)
    eidf = eidx.astype(F32)
    cand = jnp.where(chosen, eidf, float(E))
    for k in range(TOP_K):
        ek = jnp.min(cand, axis=0, keepdims=True)
        hit = cand == ek
        e_ref[k:k + 1, :] = ek.astype(jnp.int32)
        r_ref[k:k + 1, :] = jnp.sum(jnp.where(hit, pos, 0.0), axis=0, keepdims=True).astype(jnp.int32)
        w_ref[k:k + 1, :] = jnp.sum(jnp.where(hit, wn, 0.0), axis=0, keepdims=True)
        cand = jnp.where(hit, float(E), cand)


def _router(logits_t, router_bias, tn):
    E, L = logits_t.shape
    tri = (jnp.arange(tn)[:, None] <= jnp.arange(tn)[None, :]).astype(BF16)
    slot = lambda dt: jax.ShapeDtypeStruct((TOP_K, L), dt)
    return pl.pallas_call(
        _router_kernel,
        grid=(L // tn,),
        in_specs=[pl.BlockSpec((E, tn), lambda i: (0, i)),
                  pl.BlockSpec((E, 1), lambda i: (0, 0)),
                  pl.BlockSpec((tn, tn), lambda i: (0, 0))],
        out_specs=[pl.BlockSpec((TOP_K, tn), lambda i: (0, i)),
                   pl.BlockSpec((TOP_K, tn), lambda i: (0, i)),
                   pl.BlockSpec((TOP_K, tn), lambda i: (0, i)),
                   pl.BlockSpec((E, 1), lambda i: (0, 0))],
        out_shape=[slot(jnp.int32), slot(jnp.int32), slot(F32), jax.ShapeDtypeStruct((E, 1), jnp.int32)],
        scratch_shapes=[pltpu.VMEM((E, 1), F32)],
        compiler_params=_cparams(("arbitrary",)),
        name="router_topk",
    )(logits_t, router_bias.reshape(E, 1).astype(F32), tri)


def _dispatch_kernel(dest_ref, pad_ref, x_ref, xs_ref, zero_ref, sem, zsem):
    i = pl.program_id(0)
    tm = x_ref.shape[0] // ROW_TILE

    def tile(ref, t):
        return ref.at[pl.ds(pl.multiple_of(t * ROW_TILE, ROW_TILE), ROW_TILE), :]

    def row_copy(r, k):
        return pltpu.make_async_copy(tile(x_ref, r), tile(xs_ref, dest_ref[k, r]), sem)

    def zero_copy(dst):
        return pltpu.make_async_copy(zero_ref, tile(xs_ref, dst), zsem)

    @pl.when(i == 0)
    def _():
        zero_ref[...] = jnp.zeros_like(zero_ref)

        def per_expert(e, c):
            lo, hi = pad_ref[0, e], pad_ref[1, e]
            lax.fori_loop(lo, hi, lambda d, c2: (zero_copy(d).start(), c2)[1], 0)
            lax.fori_loop(lo, hi, lambda d, c2: (zero_copy(d).wait(), c2)[1], 0)
            return c
        lax.fori_loop(0, N_EXPERTS, per_expert, 0)

    def issue(r, c):
        for k in range(TOP_K):
            row_copy(r, k).start()
        return c
    lax.fori_loop(0, tm, issue, 0)

    def drain(r, c):
        for k in range(TOP_K):
            row_copy(r, k).wait()
        return c
    lax.fori_loop(0, tm, drain, 0)


def _dispatch(x1p, dest, pad_lo_hi, n_slots, tm):
    L = x1p.shape[0] // ROW_TILE
    return pl.pallas_call(
        _dispatch_kernel,
        grid=(L // tm,),
        in_specs=[pl.BlockSpec((TOP_K, tm), lambda i: (0, i), memory_space=pltpu.SMEM),
                  pl.BlockSpec(memory_space=pltpu.SMEM),
                  pl.BlockSpec((tm * ROW_TILE, LANES), lambda i: (i, 0))],
        out_specs=pl.BlockSpec(memory_space=pl.ANY),
        out_shape=jax.ShapeDtypeStruct((n_slots * ROW_TILE, LANES), x1p.dtype),
        scratch_shapes=[pltpu.VMEM((ROW_TILE, LANES), x1p.dtype), pltpu.SemaphoreType.DMA(()),
                        pltpu.SemaphoreType.DMA(())],
        compiler_params=_cparams(("arbitrary",)),
        name="moe_dispatch",
    )(dest, pad_lo_hi, x1p)


def _experts_kernel(be_ref, nb_ref, xs_ref, wgu_ref, wd_ref, y_ref, wgu_b, wd_b):
    b = pl.program_id(0)

    @pl.when((b == 0) | (be_ref[b] != be_ref[jnp.maximum(b - 1, 0)]))
    def _():
        wgu_b[...] = wgu_ref[0, 0].astype(BF16)
        wd_b[...] = wd_ref[0, 0].astype(BF16)

    @pl.when(b < nb_ref[0])
    def _():
        de = wd_b.shape[0]
        xa, xb = _load_token_tiles(xs_ref, xs_ref.shape[0] // ROW_TILE)
        x = jnp.concatenate([xa.astype(BF16), xb.astype(BF16)], axis=1)
        h = jnp.dot(x, wgu_b[...], preferred_element_type=F32)
        a = jax.nn.silu(h[:, :de]) * h[:, de:]
        _store_token_tiles(y_ref, jnp.dot(a.astype(BF16), wd_b[...], preferred_element_type=F32))


def _experts(xs, block_e, n_used, w_gu, w_down, layer, blk):
    n_slots = xs.shape[0] // ROW_TILE
    nblk = n_slots // blk
    rows = blk * ROW_TILE
    D = w_gu.shape[2]
    de2 = w_gu.shape[3]
    de = w_down.shape[2]
    last = lambda b, be, nb: jnp.minimum(b, nb[0] - 1)
    return pl.pallas_call(
        _experts_kernel,
        grid_spec=pltpu.PrefetchScalarGridSpec(
            num_scalar_prefetch=2,
            grid=(nblk,),
            in_specs=[pl.BlockSpec((rows, LANES), lambda b, be, nb: (last(b, be, nb), 0)),
                      pl.BlockSpec((1, 1, D, de2), lambda b, be, nb: (layer, be[b], 0, 0)),
                      pl.BlockSpec((1, 1, de, D), lambda b, be, nb: (layer, be[b], 0, 0))],
            out_specs=pl.BlockSpec((rows, LANES), lambda b, be, nb: (last(b, be, nb), 0)),
            scratch_shapes=[pltpu.VMEM((D, de2), BF16), pltpu.VMEM((de, D), BF16)]),
        out_shape=jax.ShapeDtypeStruct(xs.shape, jnp.uint32),
        compiler_params=_cparams(("arbitrary",)),
        name="moe_experts",
    )(block_e, n_used, xs, w_gu, w_down)


def _combine_kernel(dcur_ref, dnxt_ref, y_ref, w_ref, x1_ref, x1b_ref, sgu_ref, sdn_ref, g_ref, b_ref,
                    x2_ref, x2b_ref, buf, sem, *, alpha):
    i = pl.program_id(0)
    n = pl.num_programs(0)
    tm = x1_ref.shape[0]
    slot = i % 2

    def row_copy(d_ref, s, r, k):
        src = y_ref.at[pl.ds(pl.multiple_of(d_ref[k, r] * ROW_TILE, ROW_TILE), ROW_TILE), :]
        dst = buf.at[s, k, pl.ds(pl.multiple_of(r * ROW_TILE, ROW_TILE), ROW_TILE), :]
        return pltpu.make_async_copy(src, dst, sem.at[s])

    def issue(d_ref, s):
        def body(r, c):
            for k in range(TOP_K):
                row_copy(d_ref, s, r, k).start()
            return c
        lax.fori_loop(0, tm, body, 0)

    @pl.when(i == 0)
    def _():
        issue(dcur_ref, 0)

    @pl.when(i + 1 < n)
    def _():
        issue(dnxt_ref, 1 - slot)

    de = sdn_ref.shape[0]
    h = jnp.dot(x1b_ref[...], sgu_ref[...], preferred_element_type=F32)
    a = jax.nn.silu(h[:, :de]) * h[:, de:]
    ffn = jnp.dot(a.astype(BF16), sdn_ref[...], preferred_element_type=F32)

    def drain(r, c):
        for k in range(TOP_K):
            row_copy(dcur_ref, slot, r, k).wait()
        return c
    lax.fori_loop(0, tm, drain, 0)

    half = ffn.shape[1] // 2
    fa, fb = ffn[:, :half], ffn[:, half:]
    for k in range(TOP_K):
        ya, yb = _load_token_tiles(buf.at[slot, k], tm)
        w = w_ref[:, k:k + 1]
        fa = fa + ya * w
        fb = fb + yb * w
    ffn = jnp.concatenate([fa, fb], axis=1)
    x2 = _ln(alpha * x1_ref[...] + ffn, g_ref[...], b_ref[...])
    x2_ref[...] = x2
    x2b_ref[...] = x2.astype(BF16)


def _combine(dest, y, w_t, x1, x1b, sh_gu, sh_down, ln_g, ln_b, alpha, tm):
    L, D = x1.shape
    n = L // tm
    full = lambda shape: pl.BlockSpec(shape, lambda i: (0,) * len(shape))
    return pl.pallas_call(
        functools.partial(_combine_kernel, alpha=alpha),
        grid=(n,),
        in_specs=[pl.BlockSpec((TOP_K, tm), lambda i: (0, i), memory_space=pltpu.SMEM),
                  pl.BlockSpec((TOP_K, tm), lambda i: (0, jnp.minimum(i + 1, n - 1)), memory_space=pltpu.SMEM),
                  pl.BlockSpec(memory_space=pl.ANY),
                  pl.BlockSpec((tm, TOP_K), lambda i: (i, 0)),
                  pl.BlockSpec((tm, D), lambda i: (i, 0)),
                  pl.BlockSpec((tm, D), lambda i: (i, 0)),
                  full(sh_gu.shape), full(sh_down.shape), full((1, D)), full((1, D))],
        out_specs=[pl.BlockSpec((tm, D), lambda i: (i, 0)), pl.BlockSpec((tm, D), lambda i: (i, 0))],
        out_shape=[jax.ShapeDtypeStruct((L, D), F32), jax.ShapeDtypeStruct((L, D), BF16)],
        scratch_shapes=[pltpu.VMEM((2, TOP_K, tm * ROW_TILE, LANES), y.dtype), pltpu.SemaphoreType.DMA((2,))],
        compiler_params=_cparams(("arbitrary",)),
        name="moe_combine_ln2",
    )(dest, dest, y, w_t, x1, x1b, sh_gu, sh_down, ln_g.reshape(1, D).astype(F32), ln_b.reshape(1, D).astype(F32))


def _experts_fused_kernel(blk_ref, exp_ref, lo_ref, hi_ref, nv_ref,
                          src_ref, srcn_ref, dst_ref, x_hbm, wgu_ref, wd_ref, y_hbm,
                          xbuf, ybuf, wgu_b, wd_b, gsem, ssem, *, trash_base):
    w = pl.program_id(0)
    nv = nv_ref[0]
    R = xbuf.shape[1] // ROW_TILE
    par = w % 2

    def tile(ref, t):
        start = t * ROW_TILE if isinstance(t, int) else pl.multiple_of(t * ROW_TILE, ROW_TILE)
        return ref.at[pl.ds(start, ROW_TILE), :]

    def gather(s_ref, p):
        for r in range(R):
            pltpu.make_async_copy(tile(x_hbm, s_ref[0, 0, r]), tile(xbuf.at[p], r), gsem.at[p]).start()

    def gather_wait(p):
        pltpu.make_async_copy(x_hbm.at[pl.ds(0, R * ROW_TILE), :], xbuf.at[p], gsem.at[p]).wait()

    def scatter_wait(p):
        pltpu.make_async_copy(ybuf.at[p], y_hbm.at[pl.ds(0, R * ROW_TILE), :], ssem.at[p]).wait()

    @pl.when(w == 0)
    def _():
        gather(src_ref, 0)

    @pl.when((w < nv) & ((w == 0) | (exp_ref[w] != exp_ref[jnp.maximum(w - 1, 0)])))
    def _():
        wgu_b[...] = wgu_ref[0, 0].astype(BF16)
        wd_b[...] = wd_ref[0, 0].astype(BF16)

    @pl.when((w >= 2) & (w < nv))
    def _():
        scatter_wait(par)

    @pl.when(w < nv)
    def _():
        gather_wait(par)
        gather(srcn_ref, 1 - par)
        de = wd_b.shape[0]
        xa, xb = _load_token_tiles(xbuf.at[par], R)
        x = jnp.concatenate([xa.astype(BF16), xb.astype(BF16)], axis=1)
        h = jnp.dot(x, wgu_b[...], preferred_element_type=F32)
        a = jax.nn.silu(h[:, :de]) * h[:, de:]
        _store_token_tiles(ybuf.at[par], jnp.dot(a.astype(BF16), wd_b[...], preferred_element_type=F32))
        base = blk_ref[w] * R
        lo, hi = lo_ref[w], hi_ref[w]
        for r in range(R):
            inside = (base + r >= lo) & (base + r < hi)
            d = jnp.where(inside, dst_ref[0, 0, r], trash_base + par * R + r)
            pltpu.make_async_copy(tile(ybuf.at[par], r), tile(y_hbm, d), ssem.at[par]).start()

    @pl.when(w == nv - 1)
    def _():
        scatter_wait(par)
        gather_wait(1 - par)

    @pl.when((w == nv - 1) & (w >= 1))
    def _():
        scatter_wait(1 - par)


def _experts_fused(x1p, src, dst, items, w_gu, w_down, layer, n_out_tiles, blk):
    blk_w, exp_w, lo_w, hi_w, n_valid = items
    n_items = blk_w.shape[0]
    D, de2 = w_gu.shape[2], w_gu.shape[3]
    de = w_down.shape[2]
    trash_base = n_out_tiles - 2 * blk
    cur = lambda w, nv: jnp.minimum(w, nv[0] - 1)
    return pl.pallas_call(
        functools.partial(_experts_fused_kernel, trash_base=trash_base),
        grid_spec=pltpu.PrefetchScalarGridSpec(
            num_scalar_prefetch=5,
            grid=(n_items,),
            in_specs=[pl.BlockSpec((1, 1, blk), lambda w, b, e, lo, hi, nv: (b[cur(w, nv)], 0, 0),
                                   memory_space=pltpu.SMEM),
                      pl.BlockSpec((1, 1, blk), lambda w, b, e, lo, hi, nv: (b[cur(w + 1, nv)], 0, 0),
                                   memory_space=pltpu.SMEM),
                      pl.BlockSpec((1, 1, blk), lambda w, b, e, lo, hi, nv: (b[cur(w, nv)], 0, 0),
                                   memory_space=pltpu.SMEM),
                      pl.BlockSpec(memory_space=pl.ANY),
                      pl.BlockSpec((1, 1, D, de2), lambda w, b, e, lo, hi, nv: (layer, e[cur(w, nv)], 0, 0)),
                      pl.BlockSpec((1, 1, de, D), lambda w, b, e, lo, hi, nv: (layer, e[cur(w, nv)], 0, 0))],
            out_specs=pl.BlockSpec(memory_space=pl.ANY),
            scratch_shapes=[pltpu.VMEM((2, blk * ROW_TILE, LANES), jnp.uint32),
                            pltpu.VMEM((2, blk * ROW_TILE, LANES), jnp.uint32),
                            pltpu.VMEM((D, de2), BF16), pltpu.VMEM((de, D), BF16),
                            pltpu.SemaphoreType.DMA((2,)), pltpu.SemaphoreType.DMA((2,))]),
        out_shape=jax.ShapeDtypeStruct((n_out_tiles * ROW_TILE, LANES), jnp.uint32),
        compiler_params=_cparams(("arbitrary",)),
        name="moe_experts_fused",
    )(blk_w, exp_w, lo_w, hi_w, n_valid, src, src, dst, x1p, w_gu, w_down)


def _combine_stream_kernel(*refs, alpha):
    y_refs = refs[:TOP_K]
    w_ref, x1_ref, x1b_ref, sgu_ref, sdn_ref, g_ref, b_ref, x2_ref, x2b_ref = refs[TOP_K:]
    tm = x1_ref.shape[0]
    de = sdn_ref.shape[0]
    h = jnp.dot(x1b_ref[...], sgu_ref[...], preferred_element_type=F32)
    a = jax.nn.silu(h[:, :de]) * h[:, de:]
    ffn = jnp.dot(a.astype(BF16), sdn_ref[...], preferred_element_type=F32)
    half = ffn.shape[1] // 2
    fa, fb = ffn[:, :half], ffn[:, half:]
    for k in range(TOP_K):
        ya, yb = _load_token_tiles(y_refs[k], tm)
        w = w_ref[:, k:k + 1]
        fa = fa + ya * w
        fb = fb + yb * w
    ffn = jnp.concatenate([fa, fb], axis=1)
    x2 = _ln(alpha * x1_ref[...] + ffn, g_ref[...], b_ref[...])
    x2_ref[...] = x2
    x2b_ref[...] = x2.astype(BF16)


def _combine_stream(yk, w_t, x1, x1b, sh_gu, sh_down, ln_g, ln_b, alpha, tm):
    L, D = x1.shape
    n = L // tm
    full = lambda shape: pl.BlockSpec(shape, lambda i: (0,) * len(shape))
    y_specs = [pl.BlockSpec((tm * ROW_TILE, LANES), lambda i, k=k: (k * n + i, 0)) for k in range(TOP_K)]
    return pl.pallas_call(
        functools.partial(_combine_stream_kernel, alpha=alpha),
        grid=(n,),
        in_specs=y_specs + [pl.BlockSpec((tm, TOP_K), lambda i: (i, 0)),
                            pl.BlockSpec((tm, D), lambda i: (i, 0)),
                            pl.BlockSpec((tm, D), lambda i: (i, 0)),
                            full(sh_gu.shape), full(sh_down.shape), full((1, D)), full((1, D))],
        out_specs=[pl.BlockSpec((tm, D), lambda i: (i, 0)), pl.BlockSpec((tm, D), lambda i: (i, 0))],
        out_shape=[jax.ShapeDtypeStruct((L, D), F32), jax.ShapeDtypeStruct((L, D), BF16)],
        compiler_params=_cparams(("arbitrary",)),
        name="moe_combine_ln2",
    )(*([yk] * TOP_K), w_t, x1, x1b, sh_gu, sh_down, ln_g.reshape(1, D).astype(F32), ln_b.reshape(1, D).astype(F32))


def _moe_layer_fused(x1, x1b, x1p, logits_t, router_bias, w_gu, w_down, layer, sh_gu, sh_down, ln_g, ln_b, alpha,
                     router_tn, combine_tm):
    L, D = x1.shape
    E, K, R = N_EXPERTS, TOP_K, MOE_BLK
    A = K * L
    assert A % R == 0
    nblk = A // R
    e_k, _, w_k, counts = _router(logits_t, router_bias, router_tn)
    keys = (e_k * L + jnp.arange(L, dtype=jnp.int32)[None, :]) * K + jnp.arange(K, dtype=jnp.int32)[:, None]
    skeys = jnp.sort(keys.reshape(A))
    tok = (skeys // K) % L
    src = tok.reshape(nblk, 1, R)
    dst = ((skeys % K) * L + tok).reshape(nblk, 1, R)
    ends = jnp.cumsum(counts.reshape(E))
    cuts = jnp.sort(jnp.concatenate([jnp.arange(nblk, dtype=jnp.int32) * R, (ends - counts.reshape(E))]))
    lo = cuts
    hi = jnp.concatenate([cuts[1:], jnp.full((1,), A, jnp.int32)])
    valid = hi > lo
    order = jnp.argsort(jnp.logical_not(valid), stable=True)
    lo, hi = lo[order].astype(jnp.int32), hi[order].astype(jnp.int32)
    n_valid = jnp.sum(valid).astype(jnp.int32).reshape(1)
    blk_w = jnp.minimum(lo // R, nblk - 1).astype(jnp.int32)
    exp_w = jnp.minimum(jnp.sum((ends[None, :] <= lo[:, None]).astype(jnp.int32), axis=1), E - 1).astype(jnp.int32)
    n_out_tiles = A + 2 * R
    yk = _experts_fused(x1p, src, dst, (blk_w, exp_w, lo, hi, n_valid), w_gu, w_down, layer, n_out_tiles, R)
    return _combine_stream(yk, w_k.T, x1, x1b, sh_gu, sh_down, ln_g, ln_b, alpha, combine_tm)


MOE_BLK = 256


def _moe_layer(x1, x1b, x1p, logits_t, router_bias, w_gu, w_down, layer, sh_gu, sh_down, ln_g, ln_b, alpha,
               router_tn, dispatch_tm, combine_tm):
    L, D = x1.shape
    E = N_EXPERTS
    e_k, r_k, w_k, counts = _router(logits_t, router_bias, router_tn)
    counts = counts.reshape(E)
    padded = (counts + MOE_BLK - 1) // MOE_BLK * MOE_BLK
    pends = jnp.cumsum(padded)
    pstarts = pends - padded
    dest = jnp.sum(jnp.where(e_k[..., None] == jnp.arange(E), pstarts, 0), axis=-1) + r_k
    nblk = -(-(L * TOP_K) // MOE_BLK) + E
    n_used = (pends[-1] // MOE_BLK).astype(jnp.int32)
    blocks = jnp.minimum(jnp.arange(nblk, dtype=jnp.int32), n_used - 1)
    block_e = jnp.sum((pends[None, :] <= (blocks * MOE_BLK)[:, None]).astype(jnp.int32), axis=1)
    block_e = jnp.minimum(block_e, E - 1).astype(jnp.int32)
    pad_lo_hi = jnp.stack([pstarts + counts, pends]).astype(jnp.int32)
    xs = _dispatch(x1p, dest, pad_lo_hi, nblk * MOE_BLK, dispatch_tm)
    y = _experts(xs, block_e, n_used.reshape(1), w_gu, w_down, layer, MOE_BLK)
    return _combine(dest, y, w_k.T, x1, x1b, sh_gu, sh_down, ln_g, ln_b, alpha, combine_tm)


def _pick(n, pref):
    t = min(n, pref)
    assert n % t == 0
    return t


def kernel(x, w_in, w_out, mix_norm_g, s5_lambda_re, s5_lambda_im, s5_log_dt, s5_b_re, s5_b_im, s5_c_re, s5_c_im, s5_d, s5_w_glu, conv_w, sgu_ln_g, sgu_ln_b, sgu_w, sgu_b, diff_lq1, diff_lk1, diff_lq2, diff_lk2, diff_subln_g, rel_bias, ln1_g, ln1_b, router_w, router_bias, moe_w_gu, moe_w_down, shared_w_gu, shared_w_down, ln2_g, ln2_b):
    Bt, L, D = x.shape
    assert Bt == 1
    depth = w_in.shape[0]
    alpha = (2 * depth) ** 0.25
    gw = GROUP_W
    tq = _pick(L, 256)
    nb = _attn_bias_tables(rel_bias, tq)
    xf = x.reshape(L, D)
    xb = xf.astype(BF16)
    for l in range(depth):
        w_in_b = w_in[l].astype(BF16)
        proj_a = _matmul(xb, w_in_b[:, :6 * gw], _pick(L, 1024), 512, F32)
        kmat = _matmul(xb, w_in_b[:, 7 * gw:8 * gw], _pick(L, 1024), 512, BF16)
        w_q = (w_in[l][:, 6 * gw:7 * gw] * (DIFF_QK_DIM ** -0.5 * LOG2E)).astype(BF16)
        w_qv_t = jnp.concatenate([w_q, w_in_b[:, 8 * gw:]], axis=1).T
        qvt = _matmul_nt(w_qv_t, xb, tq, BF16)
        s5_rt = _pick(L // S5_SUB, 256)
        tabs = _s5_tables(s5_lambda_re[l], s5_lambda_im[l], s5_log_dt[l], s5_b_re[l], s5_b_im[l],
                          s5_c_re[l], s5_c_im[l], s5_rt)
        ys5 = _s5_scan(proj_a, tabs, s5_rt)
        abc = _mixers(proj_a, ys5, s5_d[l], s5_w_glu[l], conv_w[l], sgu_ln_g[l], sgu_ln_b[l], sgu_w[l], sgu_b[l],
                      mix_norm_g[l], _pick(L, 512))
        lambda_init = 0.8 - 0.6 * math.exp(-0.3 * l)
        d_out = _diff_attention(qvt, kmat, nb, diff_lq1[l], diff_lk1[l], diff_lq2[l], diff_lk2[l],
                                diff_subln_g[l], lambda_init)
        x1, x1b, x1p, logits_t = _outproj(abc, d_out, xf, w_out[l].astype(BF16), ln1_g[l], ln1_b[l], router_w[l],
                                          alpha, _pick(L, 256))
        xf, xb = _moe_layer_fused(x1, x1b, x1p, logits_t, router_bias[l], moe_w_gu, moe_w_down, l,
                                  shared_w_gu[l].astype(BF16), shared_w_down[l].astype(BF16),
                                  ln2_g[l], ln2_b[l], alpha, _pick(L, 512), _pick(L, 256))
    return xf.reshape(Bt, L, D)
```

```python
import functools
import math

import jax
import jax.numpy as jnp
from jax import lax
from jax.experimental import pallas as pl
from jax.experimental.pallas import tpu as pltpu

F32 = jnp.float32
BF16 = jnp.bfloat16

GROUP_W = 512
CHUNK = 64
S5_GROUP_CH = 16
S5_GROUPS = 32
S5_STATE = 64
S5_SUB = 16
S5_KBLOCKS = 4
SGU_BLK = 128
SGU_HEADS = 4
DIFF_HEADS = 4
DIFF_QK_DIM = 64
DIFF_V_DIM = 128
NUM_BUCKETS = 32
MAX_DISTANCE = 128
N_EXPERTS = 64
TOP_K = 8
N_EXPERT_GROUPS = 8
TOPK_GROUPS = 4
ROUTED_SCALE = 2.5
EPS = 1e-5
NEG_INF = -1e30
LOG2E = math.log2(math.e)
ONES_ROWS = 16

VMEM_LIMIT = 56 * 1024 * 1024


def _cparams(sem):
    return pltpu.CompilerParams(dimension_semantics=sem, vmem_limit_bytes=VMEM_LIMIT)


def _rms(x, g):
    return x * lax.rsqrt(jnp.mean(jnp.square(x), -1, keepdims=True) + EPS) * g


def _pack_bf16_pairs(x):
    c = x.shape[1] // 2
    hi = lax.bitcast_convert_type(x[:, :c].astype(BF16).astype(F32), jnp.uint32)
    lo = lax.bitcast_convert_type(x[:, c:].astype(BF16).astype(F32), jnp.uint32)
    return hi | (lo >> 16)


def _unpack_bf16_pairs(u):
    hi = lax.bitcast_convert_type(u & jnp.uint32(0xFFFF0000), F32)
    lo = lax.bitcast_convert_type(u << 16, F32)
    return hi, lo


ROW_TILE = 8
LANES = 128
SUB_ROWS = 128


def _store_token_tiles(ref, x):
    p = _pack_bf16_pairs(x)
    n = x.shape[0]
    for c in range(ROW_TILE):
        ref[pl.ds(c, n, stride=ROW_TILE), :] = p[:, c * LANES:(c + 1) * LANES]


def _load_token_tiles(ref, n):
    p = jnp.concatenate([ref[pl.ds(c, n, stride=ROW_TILE), :] for c in range(ROW_TILE)], axis=1)
    return _unpack_bf16_pairs(p)


def _ln(x, g, b):
    mu = jnp.mean(x, -1, keepdims=True)
    var = jnp.mean(jnp.square(x - mu), -1, keepdims=True)
    return (x - mu) * lax.rsqrt(var + EPS) * g + b


def _matmul_kernel(x_ref, w_ref, o_ref):
    o_ref[...] = jnp.dot(x_ref[...], w_ref[...], preferred_element_type=F32).astype(o_ref.dtype)


def _matmul(x, w, tm, tn, out_dtype):
    M, K = x.shape
    N = w.shape[1]
    return pl.pallas_call(
        _matmul_kernel,
        grid=(M // tm, N // tn),
        in_specs=[pl.BlockSpec((tm, K), lambda i, j: (i, 0)),
                  pl.BlockSpec((K, tn), lambda i, j: (0, j))],
        out_specs=pl.BlockSpec((tm, tn), lambda i, j: (i, j)),
        out_shape=jax.ShapeDtypeStruct((M, N), out_dtype),
        compiler_params=_cparams(("parallel", "arbitrary")),
        name="proj_matmul",
    )(x, w)


def _matmul_nt_kernel(w_ref, x_ref, o_ref):
    o_ref[0] = lax.dot_general(w_ref[...], x_ref[...], (((1,), (1,)), ((), ())),
                               preferred_element_type=F32).astype(o_ref.dtype)


def _matmul_nt(w_t, x, tm, out_dtype):
    M, K = x.shape
    N = w_t.shape[0]
    return pl.pallas_call(
        _matmul_nt_kernel,
        grid=(M // tm,),
        in_specs=[pl.BlockSpec((N, K), lambda i: (0, 0)),
                  pl.BlockSpec((tm, K), lambda i: (i, 0))],
        out_specs=pl.BlockSpec((1, N, tm), lambda i: (i, 0, 0)),
        out_shape=jax.ShapeDtypeStruct((M // tm, N, tm), out_dtype),
        compiler_params=_cparams(("parallel",)),
        name="proj_matmul_nt",
    )(w_t, x)


def _s5_tables(lam_re, lam_im, log_dt, b_re, b_im, c_re, c_im, n_rows):
    G, P, H, S = S5_GROUPS, S5_STATE, S5_GROUP_CH, S5_SUB
    hp = lax.Precision.HIGHEST
    dt = jnp.exp(log_dt.astype(F32))[:, None]
    lam = lax.complex(lam_re.astype(F32), lam_im.astype(F32))
    ldt = lam * dt
    lam_bar = jnp.exp(ldt)
    b_bar = ((lam_bar - 1.0) / lam)[..., None] * lax.complex(b_re.astype(F32), b_im.astype(F32))
    c = lax.complex(c_re.astype(F32), c_im.astype(F32))
    tau = jnp.arange(S + 1, dtype=F32)
    pows = jnp.exp(ldt[None] * tau[:, None, None])
    KB, GL = S5_KBLOCKS, S5_GROUPS // S5_KBLOCKS
    eye = jnp.eye(GL, dtype=bool)
    w1 = pows[:S][::-1][:, :, None, :] * jnp.transpose(b_bar, (0, 2, 1))[None]
    w1 = jnp.transpose(w1.reshape(S, KB, GL, H, P), (1, 0, 2, 3, 4))
    w1 = jnp.where(eye[None, None, :, None, :, None], w1[:, :, :, :, None, :], 0.0)
    w1 = w1.reshape(KB, S * GL * H, GL * P)
    w2 = jnp.transpose(c, (0, 2, 1))[:, :, None, :] * jnp.transpose(pows[1:], (1, 2, 0))[..., None]
    w2 = w2.reshape(KB, GL, P, S, H)
    w2 = jnp.where(eye[None, :, None, None, :, None], w2[:, :, :, :, None, :], 0.0)
    w2 = w2.reshape(KB, GL * P, S * GL * H)
    kc = jnp.real(jnp.einsum('ghp,tgp,gpi->tghi', c, pows[:S], b_bar, precision=hp))
    kc = jnp.transpose(kc.reshape(S, KB, GL, H, H), (1, 2, 4, 0, 3))
    kc = jnp.where(eye[None, :, None, None, :, None], kc[:, :, :, :, None, :], 0.0)
    kcat = kc.reshape(KB, GL * H, S * GL * H)

    nstep = max(1, (n_rows - 1).bit_length())
    kk = (S * (2 ** jnp.arange(nstep))).astype(F32)
    lp = jnp.exp(ldt[None] * kk[:, None, None])
    lp = jnp.transpose(lp.reshape(nstep, KB, GL * P), (1, 0, 2))
    lampow = jnp.stack([jnp.real(lp), jnp.imag(lp)], axis=2)
    return dict(
        kcat=kcat.astype(BF16),
        w1re=jnp.real(w1).astype(BF16), w1im=jnp.imag(w1).astype(BF16),
        w2re=jnp.real(w2).astype(BF16), w2im=(-jnp.imag(w2)).astype(BF16),
        lampow=lampow.astype(F32))


def _s5_kernel(u_ref, w1re_ref, w1im_ref, w2re_ref, w2im_ref, kcat_ref, lp_ref, o_ref,
               ucat_ref, yall_ref, carry_ref, *, nstep):
    t = pl.program_id(1)
    S = S5_SUB
    R = ucat_ref.shape[0]
    W = u_ref.shape[1]

    @pl.when(t == 0)
    def _():
        carry_ref[...] = jnp.zeros_like(carry_ref)

    for j in range(S):
        ucat_ref[:, j * W:(j + 1) * W] = u_ref[pl.ds(j, R, stride=S), :].astype(BF16)
    ucat = ucat_ref[...]
    xre = jnp.dot(ucat, w1re_ref[0], preferred_element_type=F32)
    xim = jnp.dot(ucat, w1im_ref[0], preferred_element_type=F32)
    row = lax.broadcasted_iota(jnp.int32, xre.shape, 0)
    cre, cim = carry_ref[0], carry_ref[1]
    lr, li = lp_ref[0, 0, 0:1, :], lp_ref[0, 0, 1:2, :]
    xre = xre + jnp.where(row == 0, lr * cre - li * cim, 0.0)
    xim = xim + jnp.where(row == 0, lr * cim + li * cre, 0.0)
    for k in range(nstep):
        sh = 1 << k
        pre = pltpu.roll(xre, sh, 0)
        pim = pltpu.roll(xim, sh, 0)
        lr, li = lp_ref[0, k, 0:1, :], lp_ref[0, k, 1:2, :]
        keep = row >= sh
        xre, xim = (xre + jnp.where(keep, lr * pre - li * pim, 0.0),
                    xim + jnp.where(keep, lr * pim + li * pre, 0.0))
    carry_ref[0] = xre[R - 1:R, :]
    carry_ref[1] = xim[R - 1:R, :]
    sre = jnp.where(row >= 1, pltpu.roll(xre, 1, 0), cre).astype(BF16)
    sim = jnp.where(row >= 1, pltpu.roll(xim, 1, 0), cim).astype(BF16)
    yall_ref[...] = (jnp.dot(sre, w2re_ref[0], preferred_element_type=F32)
                     + jnp.dot(sim, w2im_ref[0], preferred_element_type=F32))
    for j in range(S):
        yall_ref[:, j * W:] += jnp.dot(ucat_ref[:, j * W:(j + 1) * W], kcat_ref[0, :, :(S - j) * W],
                                       preferred_element_type=F32)
    for j in range(S):
        o_ref[pl.ds(j, R, stride=S), :] = yall_ref[:, j * W:(j + 1) * W]


def _s5_scan(proj_a, tabs, rt):
    L = proj_a.shape[0]
    S, KB = S5_SUB, S5_KBLOCKS
    W = GROUP_W // KB
    nstep = tabs['lampow'].shape[1]
    P2 = tabs['lampow'].shape[3]
    rows = rt * S
    kb3 = lambda a: pl.BlockSpec((1,) + a.shape[1:], lambda k, t: (k, 0, 0))
    return pl.pallas_call(
        functools.partial(_s5_kernel, nstep=nstep),
        grid=(KB, L // rows),
        in_specs=[pl.BlockSpec((rows, W), lambda k, t: (t, k)),
                  kb3(tabs['w1re']), kb3(tabs['w1im']), kb3(tabs['w2re']), kb3(tabs['w2im']), kb3(tabs['kcat']),
                  pl.BlockSpec((1, nstep, 2, P2), lambda k, t: (k, 0, 0, 0))],
        out_specs=pl.BlockSpec((rows, W), lambda k, t: (t, k)),
        out_shape=jax.ShapeDtypeStruct((L, GROUP_W), F32),
        scratch_shapes=[pltpu.VMEM((rt, S * W), BF16), pltpu.VMEM((rt, S * W), F32), pltpu.VMEM((2, 1, P2), F32)],
        compiler_params=_cparams(("parallel", "arbitrary")),
        name="s5_scan",
    )(proj_a, tabs['w1re'], tabs['w1im'], tabs['w2re'], tabs['w2im'], tabs['kcat'], tabs['lampow'])


def _mixers_kernel(s5u_ref, cb_ref, cc_ref, ch_ref, su_ref, sv_ref, cch_ref, chh_ref, ys_ref,
                   d_ref, wglu_ref, cw_ref, lng_ref, lnb_ref, ws_ref, bs_ref, g_ref, o_ref):
    i = pl.program_id(0)
    tm = o_ref.shape[0]
    gw = GROUP_W
    y = ys_ref[...] + d_ref[...] * s5u_ref[...]
    y = jax.nn.gelu(y)
    y = y * jax.nn.sigmoid(jnp.dot(y.astype(BF16), wglu_ref[...], preferred_element_type=F32))
    o_ref[:, 0:gw] = _rms(y, g_ref[0:1, :]).astype(o_ref.dtype)
    z = cc_ref[...] * ch_ref[...]
    zh = jnp.where(i > 0, cch_ref[...] * chh_ref[...], 0.0)
    row = lax.broadcasted_iota(jnp.int32, z.shape, 0)
    z1 = jnp.where(row == 0, zh[7:8, :], pltpu.roll(z, 1, 0))
    z2 = jnp.where(row == 0, zh[6:7, :], jnp.where(row == 1, zh[7:8, :], pltpu.roll(z, 2, 0)))
    conv = cw_ref[0:1, :] * z2 + cw_ref[1:2, :] * z1 + cw_ref[2:3, :] * z
    o_ref[:, gw:2 * gw] = _rms(cb_ref[...] * conv, g_ref[1:2, :]).astype(o_ref.dtype)
    uu = jax.nn.gelu(su_ref[...])
    vv = _ln(jax.nn.gelu(sv_ref[...]), lng_ref[...], lnb_ref[...]).astype(BF16)
    pi = lax.broadcasted_iota(jnp.int32, (SGU_BLK, SGU_BLK), 0)
    pj = lax.broadcasted_iota(jnp.int32, (SGU_BLK, SGU_BLK), 1)
    causal = (pj // CHUNK) <= (pi // CHUNK)
    hd = gw // SGU_HEADS
    ws = [jnp.where(causal, ws_ref[h], 0.0).astype(BF16) for h in range(SGU_HEADS)]
    blocks = []
    for n in range(tm // SGU_BLK):
        vb = vv[n * SGU_BLK:(n + 1) * SGU_BLK, :]
        blocks.append(jnp.concatenate(
            [jnp.dot(ws[h], vb[:, h * hd:(h + 1) * hd], preferred_element_type=F32) for h in range(SGU_HEADS)],
            axis=1) + bs_ref[...])
    mixed = jnp.concatenate(blocks, axis=0)
    o_ref[:, 2 * gw:3 * gw] = _rms(uu * mixed, g_ref[2:3, :]).astype(o_ref.dtype)


def _mixers(proj_a, ys5, s5_d, w_glu, conv_w, ln_g, ln_b, sgu_w, sgu_b, mix_g, tm):
    L = proj_a.shape[0]
    gw = GROUP_W
    hb = tm // 8
    col = lambda c: pl.BlockSpec((tm, gw), lambda i, c=c: (i, c))
    halo = lambda c: pl.BlockSpec((8, gw), lambda i, c=c: (jnp.maximum(i * hb - 1, 0), c))
    full = lambda a: pl.BlockSpec(a.shape, lambda i: (0,) * a.ndim)
    hd = gw // SGU_HEADS
    bs_full = jnp.repeat(sgu_b.astype(F32).T, hd, axis=1)
    consts = [s5_d.reshape(1, gw).astype(F32), w_glu.astype(BF16), conv_w.astype(F32),
              ln_g.reshape(1, gw).astype(F32), ln_b.reshape(1, gw).astype(F32), sgu_w.astype(F32),
              bs_full, mix_g.reshape(3, gw).astype(F32)]
    return pl.pallas_call(
        _mixers_kernel,
        grid=(L // tm,),
        in_specs=[col(0), col(1), col(2), col(3), col(4), col(5), halo(2), halo(3),
                  pl.BlockSpec((tm, gw), lambda i: (i, 0))] + [full(a) for a in consts],
        out_specs=pl.BlockSpec((tm, 3 * gw), lambda i: (i, 0)),
        out_shape=jax.ShapeDtypeStruct((L, 3 * gw), BF16),
        compiler_params=_cparams(("parallel",)),
        name="row_mixers",
    )(proj_a, proj_a, proj_a, proj_a, proj_a, proj_a, proj_a, proj_a, ys5, *consts)


def _t5_bucket(rel):
    half = NUM_BUCKETS // 2
    ret = jnp.where(rel > 0, half, 0)
    n = jnp.abs(rel)
    max_exact = half // 2
    large = max_exact + (jnp.log(jnp.maximum(n, 1).astype(F32) / max_exact)
                         / math.log(MAX_DISTANCE / max_exact) * (half - max_exact)).astype(jnp.int32)
    large = jnp.minimum(large, half - 1)
    return ret + jnp.where(n < max_exact, n, large)


def _attn_bias_tables(rel_bias, tq):
    assert tq >= MAX_DISTANCE
    rb = rel_bias.astype(F32)
    far = rb[NUM_BUCKETS // 2 - 1]
    buckets = jnp.arange(NUM_BUCKETS)[:, None]

    def bias_of(rel):
        onehot = _t5_bucket(rel)[:, :, None, None] == buckets
        return jnp.sum(jnp.where(onehot, rb, 0.0), axis=2) - far

    kj = jnp.arange(tq)[:, None]
    qi = jnp.arange(tq)[None, :]
    diag = jnp.where(((kj // CHUNK) <= (qi // CHUNK))[..., None], bias_of(kj - qi), NEG_INF)
    prev = bias_of(kj - tq - qi)
    tabs = jnp.stack([jnp.transpose(diag, (2, 0, 1)), jnp.transpose(prev, (2, 0, 1))], axis=1)
    tabs = jnp.where(tabs > 0.5 * NEG_INF, tabs * LOG2E, NEG_INF)
    return jnp.concatenate([tabs, tabs], axis=3)


def _attn_kernel(qt_ref, k_ref, vt_ref, nb_ref, lq1_ref, lk1_ref, lq2_ref, lk2_ref, g_ref, o_ref,
                 qq_s, sa_s, sb_s, p_s, p2_s, m_s, a_s, a2_s, acc_s, *, lambda_init):
    i = pl.program_id(1)
    tq = qt_ref.shape[2]
    dq = DIFF_QK_DIM
    qt = qt_ref[0]
    feat = lax.broadcasted_iota(jnp.int32, qt.shape, 0)
    zero = jnp.zeros_like(qt)
    qq = jnp.concatenate([jnp.where(feat < dq, qt, zero), jnp.where(feat >= dq, qt, zero)], axis=1)

    qq_s[...] = qq
    m_s[...] = jnp.full(m_s.shape, -jnp.inf, F32)
    acc_s[...] = jnp.zeros(acc_s.shape, F32)

    def scores(s_ref, j, nblk):
        rows = nblk * tq
        kb = k_ref[pl.ds(pl.multiple_of(j * tq, tq), rows), :]
        s_ref[:rows, :] = jnp.dot(kb, qq_s[...], preferred_element_type=F32)

    def softmax_part(s_ref, nblk, p_ref, a_ref, bias=None):
        rows = nblk * tq
        for c in range(2 * tq // 128):
            cs = slice(c * 128, (c + 1) * 128)
            s = s_ref[:rows, cs]
            if bias is not None:
                s = s + bias(cs)
            m_old = m_s[:, cs]
            m_new = jnp.maximum(m_old, jnp.max(s, axis=0, keepdims=True))
            m_s[:, cs] = m_new
            a_ref[:, cs] = jnp.exp2(m_old - m_new)
            p_ref[:rows, cs] = jnp.exp2((s - m_new).astype(BF16))

    def pv_part(j, nblk, p_ref, a_ref):
        rows = nblk * tq
        vt = jnp.concatenate([vt_ref[j + b] for b in range(nblk)], axis=1)
        vt = jnp.concatenate([vt, jnp.ones((ONES_ROWS, rows), BF16)], axis=0)
        acc_s[...] = a_ref[...] * acc_s[...] + jnp.dot(vt, p_ref[:rows, :], preferred_element_type=F32)

    def absorb(s_ref, j, nblk, bias=None):
        softmax_part(s_ref, nblk, p_s, a_s, bias)
        pv_part(j, nblk, p_s, a_s)

    jp = jnp.maximum(i - 1, 0)
    first = jnp.where(i > 0, 0.0, NEG_INF)
    scores(sa_s, i, 1)
    scores(sb_s, jp, 1)
    absorb(sa_s, i, 1, lambda cs: nb_ref[0, 0, :, cs])
    absorb(sb_s, jp, 1, lambda cs: nb_ref[0, 1, :, cs] + first)
    n_far = jnp.maximum(i - 1, 0)
    n_single = n_far % 2
    n_head = n_far % 4

    @pl.when(n_single == 1)
    def _():
        scores(sb_s, 0, 1)
        absorb(sb_s, 0, 1)

    @pl.when(n_head >= 2)
    def _():
        scores(sb_s, n_single, 2)
        absorb(sb_s, n_single, 2)

    scores(sa_s, n_head, 2)
    p2_s[...] = jnp.zeros(p2_s.shape, BF16)
    a2_s[...] = jnp.ones(a2_s.shape, F32)

    def quad(qd, pending):
        j0 = n_head + 4 * qd
        pv_part(pending, 2, p2_s, a2_s)
        scores(sb_s, j0 + 2, 2)
        softmax_part(sa_s, 2, p_s, a_s)
        pv_part(j0, 2, p_s, a_s)
        scores(sa_s, j0 + 4, 2)
        softmax_part(sb_s, 2, p2_s, a2_s)
        return j0 + 2

    pending = lax.fori_loop(0, n_far // 4, quad, jnp.int32(0))
    pv_part(pending, 2, p2_s, a2_s)
    o = acc_s[:DIFF_V_DIM, :] / acc_s[DIFF_V_DIM:DIFF_V_DIM + 1, :]
    lam = (jnp.exp(jnp.sum(lq1_ref[...] * lk1_ref[...], keepdims=True))
           - jnp.exp(jnp.sum(lq2_ref[...] * lk2_ref[...], keepdims=True)) + lambda_init)
    out = o[:, :tq] - lam * o[:, tq:]
    out = out * lax.rsqrt(jnp.mean(jnp.square(out), axis=0, keepdims=True) + EPS) * g_ref[...]
    o_ref[...] = (out * (1.0 - lambda_init)).T.astype(o_ref.dtype)


def _diff_attention(qvt, kmat, nb, lq1, lk1, lq2, lk2, subln_g, lambda_init):
    nq, _, tq = qvt.shape
    assert nq >= 2
    L = kmat.shape[0]
    H, dv = DIFF_HEADS, DIFF_V_DIM
    vec = lambda a: a.reshape(1, -1).astype(F32)
    small = lambda n: pl.BlockSpec((1, n), lambda h, i: (0, 0))
    return pl.pallas_call(
        functools.partial(_attn_kernel, lambda_init=lambda_init),
        grid=(H, nq),
        in_specs=[pl.BlockSpec((1, dv, tq), lambda h, i: (i, h, 0)),
                  pl.BlockSpec((L, dv), lambda h, i: (0, h)),
                  pl.BlockSpec((nq, dv, tq), lambda h, i: (0, H + h, 0)),
                  pl.BlockSpec((1, 2, tq, 2 * tq), lambda h, i: (h, 0, 0, 0)),
                  small(DIFF_QK_DIM), small(DIFF_QK_DIM), small(DIFF_QK_DIM), small(DIFF_QK_DIM),
                  pl.BlockSpec((dv, 1), lambda h, i: (0, 0))],
        out_specs=pl.BlockSpec((tq, dv), lambda h, i: (i, h)),
        out_shape=jax.ShapeDtypeStruct((L, H * dv), BF16),
        scratch_shapes=[pltpu.VMEM((dv, 2 * tq), BF16),
                        pltpu.VMEM((2 * tq, 2 * tq), F32), pltpu.VMEM((2 * tq, 2 * tq), F32),
                        pltpu.VMEM((2 * tq, 2 * tq), BF16), pltpu.VMEM((2 * tq, 2 * tq), BF16),
                        pltpu.VMEM((1, 2 * tq), F32),
                        pltpu.VMEM((1, 2 * tq), F32), pltpu.VMEM((1, 2 * tq), F32),
                        pltpu.VMEM((dv + ONES_ROWS, 2 * tq), F32)],
        compiler_params=_cparams(("parallel", "arbitrary")),
        name="diff_attention",
    )(qvt, kmat, qvt, nb, vec(lq1), vec(lk1), vec(lq2), vec(lk2), subln_g.reshape(dv, 1).astype(F32))


def _outproj_kernel(abc_ref, d_ref, x_ref, wa_ref, wd_ref, g_ref, b_ref, rhi_ref, rlo_ref,
                    x1_ref, x1b_ref, x1p_ref, lg_ref, *, alpha):
    nt = (((1,), (1,)), ((), ()))
    tm = x_ref.shape[0]
    sub = min(tm, SUB_ROWS)
    for r0 in range(0, tm, sub):
        rows = pl.ds(r0, sub)
        mix = (jnp.dot(abc_ref[rows, :], wa_ref[...], preferred_element_type=F32)
               + jnp.dot(d_ref[rows, :], wd_ref[...], preferred_element_type=F32))
        x1 = _ln(alpha * x_ref[rows, :] + mix, g_ref[...], b_ref[...])
        x1_ref[rows, :] = x1
        hi = x1.astype(BF16)
        x1b_ref[rows, :] = hi
        _store_token_tiles(x1p_ref.at[pl.ds(r0 * ROW_TILE, sub * ROW_TILE), :], x1)
        lo = (x1 - hi.astype(F32)).astype(BF16)
        lg_ref[:, r0:r0 + sub] = (lax.dot_general(rhi_ref[...], hi, nt, preferred_element_type=F32)
                                  + lax.dot_general(rhi_ref[...], lo, nt, preferred_element_type=F32)
                                  + lax.dot_general(rlo_ref[...], hi, nt, preferred_element_type=F32))


def _outproj(abc, d_out, x, w_out, ln_g, ln_b, router_w, alpha, tm):
    L, D = x.shape
    E = router_w.shape[1]
    ka = abc.shape[1]
    kd = d_out.shape[1]
    rwt = router_w.astype(F32).T
    rhi = rwt.astype(BF16)
    rlo = (rwt - rhi.astype(F32)).astype(BF16)
    full = lambda shape: pl.BlockSpec(shape, lambda i: (0,) * len(shape))
    return pl.pallas_call(
        functools.partial(_outproj_kernel, alpha=alpha),
        grid=(L // tm,),
        in_specs=[pl.BlockSpec((tm, ka), lambda i: (i, 0)),
                  pl.BlockSpec((tm, kd), lambda i: (i, 0)),
                  pl.BlockSpec((tm, D), lambda i: (i, 0)),
                  pl.BlockSpec((ka, D), lambda i: (0, 0)),
                  pl.BlockSpec((kd, D), lambda i: (ka // kd, 0)),
                  full((1, D)), full((1, D)), full((E, D)), full((E, D))],
        out_specs=[pl.BlockSpec((tm, D), lambda i: (i, 0)),
                   pl.BlockSpec((tm, D), lambda i: (i, 0)),
                   pl.BlockSpec((tm * ROW_TILE, LANES), lambda i: (i, 0)),
                   pl.BlockSpec((E, tm), lambda i: (0, i))],
        out_shape=[jax.ShapeDtypeStruct((L, D), F32), jax.ShapeDtypeStruct((L, D), BF16),
                   jax.ShapeDtypeStruct((L * ROW_TILE, LANES), jnp.uint32), jax.ShapeDtypeStruct((E, L), F32)],
        compiler_params=_cparams(("parallel",)),
        name="outproj_ln1",
    )(abc, d_out, x, w_out, w_out, ln_g.reshape(1, D).astype(F32), ln_b.reshape(1, D).astype(F32), rhi, rlo)


def _router_kernel(lg_ref, bias_ref, tri_ref, e_ref, r_ref, w_ref, cnt_ref, carry_ref):
    i = pl.program_id(0)
    E, tn = lg_ref.shape
    ng = N_EXPERT_GROUPS
    gs_ = E // ng

    @pl.when(i == 0)
    def _():
        carry_ref[...] = jnp.zeros_like(carry_ref)

    s = jax.nn.sigmoid(lg_ref[...])
    sel = s + bias_ref[...]
    midx = lax.broadcasted_iota(jnp.int32, (gs_, tn), 0).astype(F32)
    rows, gscore = [], []
    for g in range(ng):
        rg = sel[g * gs_:(g + 1) * gs_, :]
        m1 = jnp.max(rg, axis=0, keepdims=True)
        first = jnp.min(jnp.where(rg == m1, midx, float(gs_)), axis=0, keepdims=True)
        m2 = jnp.max(jnp.where(midx == first, -jnp.inf, rg), axis=0, keepdims=True)
        rows.append(rg)
        gscore.append(m1 + m2)
    vals = []
    for g in range(ng):
        rank = jnp.zeros((1, tn), F32)
        for o in range(ng):
            if o != g:
                beats = (gscore[o] >= gscore[g]) if o < g else (gscore[o] > gscore[g])
                rank = rank + jnp.where(beats, 1.0, 0.0)
        vals.append(jnp.where(rank < TOPK_GROUPS, rows[g], -jnp.inf))
    val = jnp.concatenate(vals, axis=0)
    eidx = lax.broadcasted_iota(jnp.int32, val.shape, 0)
    erank = jnp.zeros(val.shape, F32)
    for e in range(E):
        other = val[e:e + 1, :]
        erank = erank + jnp.where(eidx > e, jnp.where(other >= val, 1.0, 0.0), jnp.where(other > val, 1.0, 0.0))
    chosen = erank < TOP_K
    wsel = jnp.where(chosen, s, 0.0)
    wn = wsel / (jnp.sum(wsel, axis=0, keepdims=True) + 1e-20) * ROUTED_SCALE
    chf = jnp.where(chosen, 1.0, 0.0)
    incl = jnp.dot(chf.astype(BF16), tri_ref[...], preferred_element_type=F32)
    base = carry_ref[...]
    pos = base + incl - chf
    carry_ref[...] = base + incl[:, tn - 1:tn]
    cnt_ref[...] = (base + incl[:, tn - 1:tn]).astype(jnp.int32)
    eidf = eidx.astype(F32)
    cand = jnp.where(chosen, eidf, float(E))
    for k in range(TOP_K):
        ek = jnp.min(cand, axis=0, keepdims=True)
        hit = cand == ek
        e_ref[k:k + 1, :] = ek.astype(jnp.int32)
        r_ref[k:k + 1, :] = jnp.sum(jnp.where(hit, pos, 0.0), axis=0, keepdims=True).astype(jnp.int32)
        w_ref[k:k + 1, :] = jnp.sum(jnp.where(hit, wn, 0.0), axis=0, keepdims=True)
        cand = jnp.where(hit, float(E), cand)


def _router(logits_t, router_bias, tn):
    E, L = logits_t.shape
    tri = (jnp.arange(tn)[:, None] <= jnp.arange(tn)[None, :]).astype(BF16)
    slot = lambda dt: jax.ShapeDtypeStruct((TOP_K, L), dt)
    return pl.pallas_call(
        _router_kernel,
        grid=(L // tn,),
        in_specs=[pl.BlockSpec((E, tn), lambda i: (0, i)),
                  pl.BlockSpec((E, 1), lambda i: (0, 0)),
                  pl.BlockSpec((tn, tn), lambda i: (0, 0))],
        out_specs=[pl.BlockSpec((TOP_K, tn), lambda i: (0, i)),
                   pl.BlockSpec((TOP_K, tn), lambda i: (0, i)),
                   pl.BlockSpec((TOP_K, tn), lambda i: (0, i)),
                   pl.BlockSpec((E, 1), lambda i: (0, 0))],
        out_shape=[slot(jnp.int32), slot(jnp.int32), slot(F32), jax.ShapeDtypeStruct((E, 1), jnp.int32)],
        scratch_shapes=[pltpu.VMEM((E, 1), F32)],
        compiler_params=_cparams(("arbitrary",)),
        name="router_topk",
    )(logits_t, router_bias.reshape(E, 1).astype(F32), tri)


def _dispatch_kernel(dest_ref, pad_ref, x_ref, xs_ref, zero_ref, sem, zsem):
    i = pl.program_id(0)
    tm = x_ref.shape[0] // ROW_TILE

    def tile(ref, t):
        return ref.at[pl.ds(pl.multiple_of(t * ROW_TILE, ROW_TILE), ROW_TILE), :]

    def row_copy(r, k):
        return pltpu.make_async_copy(tile(x_ref, r), tile(xs_ref, dest_ref[k, r]), sem)

    def zero_copy(dst):
        return pltpu.make_async_copy(zero_ref, tile(xs_ref, dst), zsem)

    @pl.when(i == 0)
    def _():
        zero_ref[...] = jnp.zeros_like(zero_ref)

        def per_expert(e, c):
            lo, hi = pad_ref[0, e], pad_ref[1, e]
            lax.fori_loop(lo, hi, lambda d, c2: (zero_copy(d).start(), c2)[1], 0)
            lax.fori_loop(lo, hi, lambda d, c2: (zero_copy(d).wait(), c2)[1], 0)
            return c
        lax.fori_loop(0, N_EXPERTS, per_expert, 0)

    def issue(r, c):
        for k in range(TOP_K):
            row_copy(r, k).start()
        return c
    lax.fori_loop(0, tm, issue, 0)

    def drain(r, c):
        for k in range(TOP_K):
            row_copy(r, k).wait()
        return c
    lax.fori_loop(0, tm, drain, 0)


def _dispatch(x1p, dest, pad_lo_hi, n_slots, tm):
    L = x1p.shape[0] // ROW_TILE
    return pl.pallas_call(
        _dispatch_kernel,
        grid=(L // tm,),
        in_specs=[pl.BlockSpec((TOP_K, tm), lambda i: (0, i), memory_space=pltpu.SMEM),
                  pl.BlockSpec(memory_space=pltpu.SMEM),
                  pl.BlockSpec((tm * ROW_TILE, LANES), lambda i: (i, 0))],
        out_specs=pl.BlockSpec(memory_space=pl.ANY),
        out_shape=jax.ShapeDtypeStruct((n_slots * ROW_TILE, LANES), x1p.dtype),
        scratch_shapes=[pltpu.VMEM((ROW_TILE, LANES), x1p.dtype), pltpu.SemaphoreType.DMA(()),
                        pltpu.SemaphoreType.DMA(())],
        compiler_params=_cparams(("arbitrary",)),
        name="moe_dispatch",
    )(dest, pad_lo_hi, x1p)


def _experts_kernel(be_ref, nb_ref, xs_ref, wgu_ref, wd_ref, y_ref, wgu_b, wd_b):
    b = pl.program_id(0)

    @pl.when((b == 0) | (be_ref[b] != be_ref[jnp.maximum(b - 1, 0)]))
    def _():
        wgu_b[...] = wgu_ref[0, 0].astype(BF16)
        wd_b[...] = wd_ref[0, 0].astype(BF16)

    @pl.when(b < nb_ref[0])
    def _():
        de = wd_b.shape[0]
        xa, xb = _load_token_tiles(xs_ref, xs_ref.shape[0] // ROW_TILE)
        x = jnp.concatenate([xa.astype(BF16), xb.astype(BF16)], axis=1)
        h = jnp.dot(x, wgu_b[...], preferred_element_type=F32)
        a = jax.nn.silu(h[:, :de]) * h[:, de:]
        _store_token_tiles(y_ref, jnp.dot(a.astype(BF16), wd_b[...], preferred_element_type=F32))


def _experts(xs, block_e, n_used, w_gu, w_down, layer, blk):
    n_slots = xs.shape[0] // ROW_TILE
    nblk = n_slots // blk
    rows = blk * ROW_TILE
    D = w_gu.shape[2]
    de2 = w_gu.shape[3]
    de = w_down.shape[2]
    last = lambda b, be, nb: jnp.minimum(b, nb[0] - 1)
    return pl.pallas_call(
        _experts_kernel,
        grid_spec=pltpu.PrefetchScalarGridSpec(
            num_scalar_prefetch=2,
            grid=(nblk,),
            in_specs=[pl.BlockSpec((rows, LANES), lambda b, be, nb: (last(b, be, nb), 0)),
                      pl.BlockSpec((1, 1, D, de2), lambda b, be, nb: (layer, be[b], 0, 0)),
                      pl.BlockSpec((1, 1, de, D), lambda b, be, nb: (layer, be[b], 0, 0))],
            out_specs=pl.BlockSpec((rows, LANES), lambda b, be, nb: (last(b, be, nb), 0)),
            scratch_shapes=[pltpu.VMEM((D, de2), BF16), pltpu.VMEM((de, D), BF16)]),
        out_shape=jax.ShapeDtypeStruct(xs.shape, jnp.uint32),
        compiler_params=_cparams(("arbitrary",)),
        name="moe_experts",
    )(block_e, n_used, xs, w_gu, w_down)


def _combine_kernel(dcur_ref, dnxt_ref, y_ref, w_ref, x1_ref, x1b_ref, sgu_ref, sdn_ref, g_ref, b_ref,
                    x2_ref, x2b_ref, buf, sem, *, alpha):
    i = pl.program_id(0)
    n = pl.num_programs(0)
    tm = x1_ref.shape[0]
    slot = i % 2

    def row_copy(d_ref, s, r, k):
        src = y_ref.at[pl.ds(pl.multiple_of(d_ref[k, r] * ROW_TILE, ROW_TILE), ROW_TILE), :]
        dst = buf.at[s, k, pl.ds(pl.multiple_of(r * ROW_TILE, ROW_TILE), ROW_TILE), :]
        return pltpu.make_async_copy(src, dst, sem.at[s])

    def issue(d_ref, s):
        def body(r, c):
            for k in range(TOP_K):
                row_copy(d_ref, s, r, k).start()
            return c
        lax.fori_loop(0, tm, body, 0)

    @pl.when(i == 0)
    def _():
        issue(dcur_ref, 0)

    @pl.when(i + 1 < n)
    def _():
        issue(dnxt_ref, 1 - slot)

    de = sdn_ref.shape[0]
    h = jnp.dot(x1b_ref[...], sgu_ref[...], preferred_element_type=F32)
    a = jax.nn.silu(h[:, :de]) * h[:, de:]
    ffn = jnp.dot(a.astype(BF16), sdn_ref[...], preferred_element_type=F32)

    def drain(r, c):
        for k in range(TOP_K):
            row_copy(dcur_ref, slot, r, k).wait()
        return c
    lax.fori_loop(0, tm, drain, 0)

    half = ffn.shape[1] // 2
    fa, fb = ffn[:, :half], ffn[:, half:]
    for k in range(TOP_K):
        ya, yb = _load_token_tiles(buf.at[slot, k], tm)
        w = w_ref[:, k:k + 1]
        fa = fa + ya * w
        fb = fb + yb * w
    ffn = jnp.concatenate([fa, fb], axis=1)
    x2 = _ln(alpha * x1_ref[...] + ffn, g_ref[...], b_ref[...])
    x2_ref[...] = x2
    x2b_ref[...] = x2.astype(BF16)


def _combine(dest, y, w_t, x1, x1b, sh_gu, sh_down, ln_g, ln_b, alpha, tm):
    L, D = x1.shape
    n = L // tm
    full = lambda shape: pl.BlockSpec(shape, lambda i: (0,) * len(shape))
    return pl.pallas_call(
        functools.partial(_combine_kernel, alpha=alpha),
        grid=(n,),
        in_specs=[pl.BlockSpec((TOP_K, tm), lambda i: (0, i), memory_space=pltpu.SMEM),
                  pl.BlockSpec((TOP_K, tm), lambda i: (0, jnp.minimum(i + 1, n - 1)), memory_space=pltpu.SMEM),
                  pl.BlockSpec(memory_space=pl.ANY),
                  pl.BlockSpec((tm, TOP_K), lambda i: (i, 0)),
                  pl.BlockSpec((tm, D), lambda i: (i, 0)),
                  pl.BlockSpec((tm, D), lambda i: (i, 0)),
                  full(sh_gu.shape), full(sh_down.shape), full((1, D)), full((1, D))],
        out_specs=[pl.BlockSpec((tm, D), lambda i: (i, 0)), pl.BlockSpec((tm, D), lambda i: (i, 0))],
        out_shape=[jax.ShapeDtypeStruct((L, D), F32), jax.ShapeDtypeStruct((L, D), BF16)],
        scratch_shapes=[pltpu.VMEM((2, TOP_K, tm * ROW_TILE, LANES), y.dtype), pltpu.SemaphoreType.DMA((2,))],
        compiler_params=_cparams(("arbitrary",)),
        name="moe_combine_ln2",
    )(dest, dest, y, w_t, x1, x1b, sh_gu, sh_down, ln_g.reshape(1, D).astype(F32), ln_b.reshape(1, D).astype(F32))


def _experts_fused_kernel(blk_ref, exp_ref, lo_ref, hi_ref, nv_ref,
                          src_ref, srcn_ref, dst_ref, x_hbm, wgu_ref, wd_ref, y_hbm,
                          xbuf, ybuf, wgu_b, wd_b, gsem, ssem, *, trash_base):
    w = pl.program_id(0)
    nv = nv_ref[0]
    R = xbuf.shape[1] // ROW_TILE
    par = w % 2

    def tile(ref, t):
        start = t * ROW_TILE if isinstance(t, int) else pl.multiple_of(t * ROW_TILE, ROW_TILE)
        return ref.at[pl.ds(start, ROW_TILE), :]

    def gather(s_ref, p):
        for r in range(R):
            pltpu.make_async_copy(tile(x_hbm, s_ref[0, 0, r]), tile(xbuf.at[p], r), gsem.at[p]).start(priority=r % 2)

    def gather_wait(p):
        pltpu.make_async_copy(x_hbm.at[pl.ds(0, R * ROW_TILE), :], xbuf.at[p], gsem.at[p]).wait()

    def scatter_wait(p):
        pltpu.make_async_copy(ybuf.at[p], y_hbm.at[pl.ds(0, R * ROW_TILE), :], ssem.at[p]).wait()

    @pl.when(w == 0)
    def _():
        gather(src_ref, 0)

    @pl.when((w < nv) & ((w == 0) | (exp_ref[w] != exp_ref[jnp.maximum(w - 1, 0)])))
    def _():
        wgu_b[...] = wgu_ref[0, 0].astype(BF16)
        wd_b[...] = wd_ref[0, 0].astype(BF16)

    @pl.when((w >= 2) & (w < nv))
    def _():
        scatter_wait(par)

    @pl.when(w < nv)
    def _():
        gather_wait(par)
        gather(srcn_ref, 1 - par)
        de = wd_b.shape[0]
        xa, xb = _load_token_tiles(xbuf.at[par], R)
        x = jnp.concatenate([xa.astype(BF16), xb.astype(BF16)], axis=1)
        h = jnp.dot(x, wgu_b[...], preferred_element_type=F32)
        a = jax.nn.silu(h[:, :de]) * h[:, de:]
        _store_token_tiles(ybuf.at[par], jnp.dot(a.astype(BF16), wd_b[...], preferred_element_type=F32))
        base = blk_ref[w] * R
        lo, hi = lo_ref[w], hi_ref[w]
        for r in range(R):
            inside = (base + r >= lo) & (base + r < hi)
            d = jnp.where(inside, dst_ref[0, 0, r], trash_base + par * R + r)
            pltpu.make_async_copy(tile(ybuf.at[par], r), tile(y_hbm, d), ssem.at[par]).start(priority=r % 2)

    @pl.when(w == nv - 1)
    def _():
        scatter_wait(par)
        gather_wait(1 - par)

    @pl.when((w == nv - 1) & (w >= 1))
    def _():
        scatter_wait(1 - par)


def _experts_fused(x1p, src, dst, items, w_gu, w_down, layer, n_out_tiles, blk):
    blk_w, exp_w, lo_w, hi_w, n_valid = items
    n_items = blk_w.shape[0]
    D, de2 = w_gu.shape[2], w_gu.shape[3]
    de = w_down.shape[2]
    trash_base = n_out_tiles - 2 * blk
    cur = lambda w, nv: jnp.minimum(w, nv[0] - 1)
    return pl.pallas_call(
        functools.partial(_experts_fused_kernel, trash_base=trash_base),
        grid_spec=pltpu.PrefetchScalarGridSpec(
            num_scalar_prefetch=5,
            grid=(n_items,),
            in_specs=[pl.BlockSpec((1, 1, blk), lambda w, b, e, lo, hi, nv: (b[cur(w, nv)], 0, 0),
                                   memory_space=pltpu.SMEM),
                      pl.BlockSpec((1, 1, blk), lambda w, b, e, lo, hi, nv: (b[cur(w + 1, nv)], 0, 0),
                                   memory_space=pltpu.SMEM),
                      pl.BlockSpec((1, 1, blk), lambda w, b, e, lo, hi, nv: (b[cur(w, nv)], 0, 0),
                                   memory_space=pltpu.SMEM),
                      pl.BlockSpec(memory_space=pl.ANY),
                      pl.BlockSpec((1, 1, D, de2), lambda w, b, e, lo, hi, nv: (layer, e[cur(w, nv)], 0, 0)),
                      pl.BlockSpec((1, 1, de, D), lambda w, b, e, lo, hi, nv: (layer, e[cur(w, nv)], 0, 0))],
            out_specs=pl.BlockSpec(memory_space=pl.ANY),
            scratch_shapes=[pltpu.VMEM((2, blk * ROW_TILE, LANES), jnp.uint32),
                            pltpu.VMEM((2, blk * ROW_TILE, LANES), jnp.uint32),
                            pltpu.VMEM((D, de2), BF16), pltpu.VMEM((de, D), BF16),
                            pltpu.SemaphoreType.DMA((2,)), pltpu.SemaphoreType.DMA((2,))]),
        out_shape=jax.ShapeDtypeStruct((n_out_tiles * ROW_TILE, LANES), jnp.uint32),
        compiler_params=_cparams(("arbitrary",)),
        name="moe_experts_fused",
    )(blk_w, exp_w, lo_w, hi_w, n_valid, src, src, dst, x1p, w_gu, w_down)


def _combine_stream_kernel(*refs, alpha):
    y_refs = refs[:TOP_K]
    w_ref, x1_ref, x1b_ref, sgu_ref, sdn_ref, g_ref, b_ref, x2_ref, x2b_ref = refs[TOP_K:]
    tm = x1_ref.shape[0]
    de = sdn_ref.shape[0]
    sub = min(tm, SUB_ROWS)
    for r0 in range(0, tm, sub):
        rows = pl.ds(r0, sub)
        h = jnp.dot(x1b_ref[rows, :], sgu_ref[...], preferred_element_type=F32)
        a = jax.nn.silu(h[:, :de]) * h[:, de:]
        ffn = jnp.dot(a.astype(BF16), sdn_ref[...], preferred_element_type=F32)
        half = ffn.shape[1] // 2
        fa, fb = ffn[:, :half], ffn[:, half:]
        for k in range(TOP_K):
            ya, yb = _load_token_tiles(y_refs[k].at[pl.ds(r0 * ROW_TILE, sub * ROW_TILE), :], sub)
            w = w_ref[rows, k:k + 1]
            fa = fa + ya * w
            fb = fb + yb * w
        ffn = jnp.concatenate([fa, fb], axis=1)
        x2 = _ln(alpha * x1_ref[rows, :] + ffn, g_ref[...], b_ref[...])
        x2_ref[rows, :] = x2
        x2b_ref[rows, :] = x2.astype(BF16)


def _combine_stream(yk, w_t, x1, x1b, sh_gu, sh_down, ln_g, ln_b, alpha, tm):
    L, D = x1.shape
    n = L // tm
    full = lambda shape: pl.BlockSpec(shape, lambda i: (0,) * len(shape))
    y_specs = [pl.BlockSpec((tm * ROW_TILE, LANES), lambda i, k=k: (k * n + i, 0)) for k in range(TOP_K)]
    return pl.pallas_call(
        functools.partial(_combine_stream_kernel, alpha=alpha),
        grid=(n,),
        in_specs=y_specs + [pl.BlockSpec((tm, TOP_K), lambda i: (i, 0)),
                            pl.BlockSpec((tm, D), lambda i: (i, 0)),
                            pl.BlockSpec((tm, D), lambda i: (i, 0)),
                            full(sh_gu.shape), full(sh_down.shape), full((1, D)), full((1, D))],
        out_specs=[pl.BlockSpec((tm, D), lambda i: (i, 0)), pl.BlockSpec((tm, D), lambda i: (i, 0))],
        out_shape=[jax.ShapeDtypeStruct((L, D), F32), jax.ShapeDtypeStruct((L, D), BF16)],
        compiler_params=_cparams(("arbitrary",)),
        name="moe_combine_ln2",
    )(*([yk] * TOP_K), w_t, x1, x1b, sh_gu, sh_down, ln_g.reshape(1, D).astype(F32), ln_b.reshape(1, D).astype(F32))


def _moe_layer_fused(x1, x1b, x1p, logits_t, router_bias, w_gu, w_down, layer, sh_gu, sh_down, ln_g, ln_b, alpha,
                     router_tn, combine_tm):
    L, D = x1.shape
    E, K, R = N_EXPERTS, TOP_K, MOE_BLK
    A = K * L
    assert A % R == 0
    nblk = A // R
    e_k, _, w_k, counts = _router(logits_t, router_bias, router_tn)
    keys = (e_k * L + jnp.arange(L, dtype=jnp.int32)[None, :]) * K + jnp.arange(K, dtype=jnp.int32)[:, None]
    skeys = jnp.sort(keys.reshape(A))
    tok = (skeys // K) % L
    src = tok.reshape(nblk, 1, R)
    dst = ((skeys % K) * L + tok).reshape(nblk, 1, R)
    ends = jnp.cumsum(counts.reshape(E))
    cuts = jnp.sort(jnp.concatenate([jnp.arange(nblk, dtype=jnp.int32) * R, (ends - counts.reshape(E))]))
    lo = cuts
    hi = jnp.concatenate([cuts[1:], jnp.full((1,), A, jnp.int32)])
    valid = hi > lo
    order = jnp.argsort(jnp.logical_not(valid), stable=True)
    lo, hi = lo[order].astype(jnp.int32), hi[order].astype(jnp.int32)
    n_valid = jnp.sum(valid).astype(jnp.int32).reshape(1)
    blk_w = jnp.minimum(lo // R, nblk - 1).astype(jnp.int32)
    exp_w = jnp.minimum(jnp.sum((ends[None, :] <= lo[:, None]).astype(jnp.int32), axis=1), E - 1).astype(jnp.int32)
    n_out_tiles = A + 2 * R
    yk = _experts_fused(x1p, src, dst, (blk_w, exp_w, lo, hi, n_valid), w_gu, w_down, layer, n_out_tiles, R)
    return _combine_stream(yk, w_k.T, x1, x1b, sh_gu, sh_down, ln_g, ln_b, alpha, combine_tm)


MOE_BLK = 256


def _moe_layer(x1, x1b, x1p, logits_t, router_bias, w_gu, w_down, layer, sh_gu, sh_down, ln_g, ln_b, alpha,
               router_tn, dispatch_tm, combine_tm):
    L, D = x1.shape
    E = N_EXPERTS
    e_k, r_k, w_k, counts = _router(logits_t, router_bias, router_tn)
    counts = counts.reshape(E)
    padded = (counts + MOE_BLK - 1) // MOE_BLK * MOE_BLK
    pends = jnp.cumsum(padded)
    pstarts = pends - padded
    dest = jnp.sum(jnp.where(e_k[..., None] == jnp.arange(E), pstarts, 0), axis=-1) + r_k
    nblk = -(-(L * TOP_K) // MOE_BLK) + E
    n_used = (pends[-1] // MOE_BLK).astype(jnp.int32)
    blocks = jnp.minimum(jnp.arange(nblk, dtype=jnp.int32), n_used - 1)
    block_e = jnp.sum((pends[None, :] <= (blocks * MOE_BLK)[:, None]).astype(jnp.int32), axis=1)
    block_e = jnp.minimum(block_e, E - 1).astype(jnp.int32)
    pad_lo_hi = jnp.stack([pstarts + counts, pends]).astype(jnp.int32)
    xs = _dispatch(x1p, dest, pad_lo_hi, nblk * MOE_BLK, dispatch_tm)
    y = _experts(xs, block_e, n_used.reshape(1), w_gu, w_down, layer, MOE_BLK)
    return _combine(dest, y, w_k.T, x1, x1b, sh_gu, sh_down, ln_g, ln_b, alpha, combine_tm)


def _pick(n, pref):
    t = min(n, pref)
    assert n % t == 0
    return t


def kernel(x, w_in, w_out, mix_norm_g, s5_lambda_re, s5_lambda_im, s5_log_dt, s5_b_re, s5_b_im, s5_c_re, s5_c_im, s5_d, s5_w_glu, conv_w, sgu_ln_g, sgu_ln_b, sgu_w, sgu_b, diff_lq1, diff_lk1, diff_lq2, diff_lk2, diff_subln_g, rel_bias, ln1_g, ln1_b, router_w, router_bias, moe_w_gu, moe_w_down, shared_w_gu, shared_w_down, ln2_g, ln2_b):
    Bt, L, D = x.shape
    assert Bt == 1
    depth = w_in.shape[0]
    alpha = (2 * depth) ** 0.25
    gw = GROUP_W
    tq = _pick(L, 256)
    nb = _attn_bias_tables(rel_bias, tq)
    xf = x.reshape(L, D)
    xb = xf.astype(BF16)
    for l in range(depth):
        w_in_b = w_in[l].astype(BF16)
        proj_a = _matmul(xb, w_in_b[:, :6 * gw], _pick(L, 1024), 512, F32)
        kmat = _matmul(xb, w_in_b[:, 7 * gw:8 * gw], _pick(L, 1024), 512, BF16)
        w_q = (w_in[l][:, 6 * gw:7 * gw] * (DIFF_QK_DIM ** -0.5 * LOG2E)).astype(BF16)
        w_qv_t = jnp.concatenate([w_q, w_in_b[:, 8 * gw:]], axis=1).T
        qvt = _matmul_nt(w_qv_t, xb, tq, BF16)
        s5_rt = _pick(L // S5_SUB, 256)
        tabs = _s5_tables(s5_lambda_re[l], s5_lambda_im[l], s5_log_dt[l], s5_b_re[l], s5_b_im[l],
                          s5_c_re[l], s5_c_im[l], s5_rt)
        ys5 = _s5_scan(proj_a, tabs, s5_rt)
        abc = _mixers(proj_a, ys5, s5_d[l], s5_w_glu[l], conv_w[l], sgu_ln_g[l], sgu_ln_b[l], sgu_w[l], sgu_b[l],
                      mix_norm_g[l], _pick(L, 512))
        lambda_init = 0.8 - 0.6 * math.exp(-0.3 * l)
        d_out = _diff_attention(qvt, kmat, nb, diff_lq1[l], diff_lk1[l], diff_lq2[l], diff_lk2[l],
                                diff_subln_g[l], lambda_init)
        x1, x1b, x1p, logits_t = _outproj(abc, d_out, xf, w_out[l].astype(BF16), ln1_g[l], ln1_b[l], router_w[l],
                                          alpha, _pick(L, 256))
        xf, xb = _moe_layer_fused(x1, x1b, x1p, logits_t, router_bias[l], moe_w_gu, moe_w_down, l,
                                  shared_w_gu[l].astype(BF16), shared_w_down[l].astype(BF16),
                                  ln2_g[l], ln2_b[l], alpha, _pick(L, 512), _pick(L, 256))
    return xf.reshape(Bt, L, D)
```

```python
import functools
import math

import jax
import jax.numpy as jnp
from jax import lax
from jax.experimental import pallas as pl
from jax.experimental.pallas import tpu as pltpu

F32 = jnp.float32
BF16 = jnp.bfloat16

GROUP_W = 512
CHUNK = 64
S5_GROUP_CH = 16
S5_GROUPS = 32
S5_STATE = 64
S5_SUB = 16
S5_KBLOCKS = 4
SGU_BLK = 128
SGU_HEADS = 4
DIFF_HEADS = 4
DIFF_QK_DIM = 64
DIFF_V_DIM = 128
NUM_BUCKETS = 32
MAX_DISTANCE = 128
N_EXPERTS = 64
TOP_K = 8
N_EXPERT_GROUPS = 8
TOPK_GROUPS = 4
ROUTED_SCALE = 2.5
EPS = 1e-5
NEG_INF = -1e30
LOG2E = math.log2(math.e)
ONES_ROWS = 16

VMEM_LIMIT = 56 * 1024 * 1024


def _cparams(sem):
    return pltpu.CompilerParams(dimension_semantics=sem, vmem_limit_bytes=VMEM_LIMIT)


def _rms(x, g):
    return x * lax.rsqrt(jnp.mean(jnp.square(x), -1, keepdims=True) + EPS) * g


def _pack_bf16_pairs(x):
    c = x.shape[1] // 2
    hi = lax.bitcast_convert_type(x[:, :c].astype(BF16).astype(F32), jnp.uint32)
    lo = lax.bitcast_convert_type(x[:, c:].astype(BF16).astype(F32), jnp.uint32)
    return hi | (lo >> 16)


def _unpack_bf16_pairs(u):
    hi = lax.bitcast_convert_type(u & jnp.uint32(0xFFFF0000), F32)
    lo = lax.bitcast_convert_type(u << 16, F32)
    return hi, lo


ROW_TILE = 8
LANES = 128
SUB_ROWS = 128


def _store_token_tiles(ref, x):
    p = _pack_bf16_pairs(x)
    n = x.shape[0]
    for c in range(ROW_TILE):
        ref[pl.ds(c, n, stride=ROW_TILE), :] = p[:, c * LANES:(c + 1) * LANES]


def _load_token_tiles(ref, n):
    p = jnp.concatenate([ref[pl.ds(c, n, stride=ROW_TILE), :] for c in range(ROW_TILE)], axis=1)
    return _unpack_bf16_pairs(p)


def _ln(x, g, b):
    mu = jnp.mean(x, -1, keepdims=True)
    var = jnp.mean(jnp.square(x - mu), -1, keepdims=True)
    return (x - mu) * lax.rsqrt(var + EPS) * g + b


def _matmul_kernel(x_ref, w_ref, o_ref):
    o_ref[...] = jnp.dot(x_ref[...], w_ref[...], preferred_element_type=F32).astype(o_ref.dtype)


def _matmul(x, w, tm, tn, out_dtype):
    M, K = x.shape
    N = w.shape[1]
    return pl.pallas_call(
        _matmul_kernel,
        grid=(M // tm, N // tn),
        in_specs=[pl.BlockSpec((tm, K), lambda i, j: (i, 0)),
                  pl.BlockSpec((K, tn), lambda i, j: (0, j))],
        out_specs=pl.BlockSpec((tm, tn), lambda i, j: (i, j)),
        out_shape=jax.ShapeDtypeStruct((M, N), out_dtype),
        compiler_params=_cparams(("parallel", "arbitrary")),
        name="proj_matmul",
    )(x, w)


def _matmul_nt_kernel(w_ref, x_ref, o_ref):
    o_ref[0] = lax.dot_general(w_ref[...], x_ref[...], (((1,), (1,)), ((), ())),
                               preferred_element_type=F32).astype(o_ref.dtype)


def _matmul_nt(w_t, x, tm, out_dtype):
    M, K = x.shape
    N = w_t.shape[0]
    return pl.pallas_call(
        _matmul_nt_kernel,
        grid=(M // tm,),
        in_specs=[pl.BlockSpec((N, K), lambda i: (0, 0)),
                  pl.BlockSpec((tm, K), lambda i: (i, 0))],
        out_specs=pl.BlockSpec((1, N, tm), lambda i: (i, 0, 0)),
        out_shape=jax.ShapeDtypeStruct((M // tm, N, tm), out_dtype),
        compiler_params=_cparams(("parallel",)),
        name="proj_matmul_nt",
    )(w_t, x)


def _s5_tables(lam_re, lam_im, log_dt, b_re, b_im, c_re, c_im, n_rows):
    G, P, H, S = S5_GROUPS, S5_STATE, S5_GROUP_CH, S5_SUB
    hp = lax.Precision.HIGHEST
    dt = jnp.exp(log_dt.astype(F32))[:, None]
    lam = lax.complex(lam_re.astype(F32), lam_im.astype(F32))
    ldt = lam * dt
    lam_bar = jnp.exp(ldt)
    b_bar = ((lam_bar - 1.0) / lam)[..., None] * lax.complex(b_re.astype(F32), b_im.astype(F32))
    c = lax.complex(c_re.astype(F32), c_im.astype(F32))
    tau = jnp.arange(S + 1, dtype=F32)
    pows = jnp.exp(ldt[None] * tau[:, None, None])
    KB, GL = S5_KBLOCKS, S5_GROUPS // S5_KBLOCKS
    eye = jnp.eye(GL, dtype=bool)
    w1 = pows[:S][::-1][:, :, None, :] * jnp.transpose(b_bar, (0, 2, 1))[None]
    w1 = jnp.transpose(w1.reshape(S, KB, GL, H, P), (1, 0, 2, 3, 4))
    w1 = jnp.where(eye[None, None, :, None, :, None], w1[:, :, :, :, None, :], 0.0)
    w1 = w1.reshape(KB, S * GL * H, GL * P)
    w2 = jnp.transpose(c, (0, 2, 1))[:, :, None, :] * jnp.transpose(pows[1:], (1, 2, 0))[..., None]
    w2 = w2.reshape(KB, GL, P, S, H)
    w2 = jnp.where(eye[None, :, None, None, :, None], w2[:, :, :, :, None, :], 0.0)
    w2 = w2.reshape(KB, GL * P, S * GL * H)
    kc = jnp.real(jnp.einsum('ghp,tgp,gpi->tghi', c, pows[:S], b_bar, precision=hp))
    kc = jnp.transpose(kc.reshape(S, KB, GL, H, H), (1, 2, 4, 0, 3))
    kc = jnp.where(eye[None, :, None, None, :, None], kc[:, :, :, :, None, :], 0.0)
    kcat = kc.reshape(KB, GL * H, S * GL * H)

    nstep = max(1, (n_rows - 1).bit_length())
    kk = (S * (2 ** jnp.arange(nstep))).astype(F32)
    lp = jnp.exp(ldt[None] * kk[:, None, None])
    lp = jnp.transpose(lp.reshape(nstep, KB, GL * P), (1, 0, 2))
    lampow = jnp.stack([jnp.real(lp), jnp.imag(lp)], axis=2)
    return dict(
        kcat=kcat.astype(BF16),
        w1re=jnp.real(w1).astype(BF16), w1im=jnp.imag(w1).astype(BF16),
        w2re=jnp.real(w2).astype(BF16), w2im=(-jnp.imag(w2)).astype(BF16),
        lampow=lampow.astype(F32))


def _s5_kernel(u_ref, w1re_ref, w1im_ref, w2re_ref, w2im_ref, kcat_ref, lp_ref, o_ref,
               ucat_ref, yall_ref, carry_ref, *, nstep):
    t = pl.program_id(1)
    S = S5_SUB
    R = ucat_ref.shape[0]
    W = u_ref.shape[1]

    @pl.when(t == 0)
    def _():
        carry_ref[...] = jnp.zeros_like(carry_ref)

    for j in range(S):
        ucat_ref[:, j * W:(j + 1) * W] = u_ref[pl.ds(j, R, stride=S), :].astype(BF16)
    ucat = ucat_ref[...]
    xre = jnp.dot(ucat, w1re_ref[0], preferred_element_type=F32)
    xim = jnp.dot(ucat, w1im_ref[0], preferred_element_type=F32)
    row = lax.broadcasted_iota(jnp.int32, xre.shape, 0)
    cre, cim = carry_ref[0], carry_ref[1]
    lr, li = lp_ref[0, 0, 0:1, :], lp_ref[0, 0, 1:2, :]
    xre = xre + jnp.where(row == 0, lr * cre - li * cim, 0.0)
    xim = xim + jnp.where(row == 0, lr * cim + li * cre, 0.0)
    for k in range(nstep):
        sh = 1 << k
        pre = pltpu.roll(xre, sh, 0)
        pim = pltpu.roll(xim, sh, 0)
        lr, li = lp_ref[0, k, 0:1, :], lp_ref[0, k, 1:2, :]
        keep = row >= sh
        xre, xim = (xre + jnp.where(keep, lr * pre - li * pim, 0.0),
                    xim + jnp.where(keep, lr * pim + li * pre, 0.0))
    carry_ref[0] = xre[R - 1:R, :]
    carry_ref[1] = xim[R - 1:R, :]
    sre = jnp.where(row >= 1, pltpu.roll(xre, 1, 0), cre).astype(BF16)
    sim = jnp.where(row >= 1, pltpu.roll(xim, 1, 0), cim).astype(BF16)
    yall_ref[...] = (jnp.dot(sre, w2re_ref[0], preferred_element_type=F32)
                     + jnp.dot(sim, w2im_ref[0], preferred_element_type=F32))
    for j in range(S):
        yall_ref[:, j * W:] += jnp.dot(ucat_ref[:, j * W:(j + 1) * W], kcat_ref[0, :, :(S - j) * W],
                                       preferred_element_type=F32)
    for j in range(S):
        o_ref[pl.ds(j, R, stride=S), :] = yall_ref[:, j * W:(j + 1) * W]


def _s5_scan(proj_a, tabs, rt):
    L = proj_a.shape[0]
    S, KB = S5_SUB, S5_KBLOCKS
    W = GROUP_W // KB
    nstep = tabs['lampow'].shape[1]
    P2 = tabs['lampow'].shape[3]
    rows = rt * S
    kb3 = lambda a: pl.BlockSpec((1,) + a.shape[1:], lambda k, t: (k, 0, 0))
    return pl.pallas_call(
        functools.partial(_s5_kernel, nstep=nstep),
        grid=(KB, L // rows),
        in_specs=[pl.BlockSpec((rows, W), lambda k, t: (t, k)),
                  kb3(tabs['w1re']), kb3(tabs['w1im']), kb3(tabs['w2re']), kb3(tabs['w2im']), kb3(tabs['kcat']),
                  pl.BlockSpec((1, nstep, 2, P2), lambda k, t: (k, 0, 0, 0))],
        out_specs=pl.BlockSpec((rows, W), lambda k, t: (t, k)),
        out_shape=jax.ShapeDtypeStruct((L, GROUP_W), F32),
        scratch_shapes=[pltpu.VMEM((rt, S * W), BF16), pltpu.VMEM((rt, S * W), F32), pltpu.VMEM((2, 1, P2), F32)],
        compiler_params=_cparams(("parallel", "arbitrary")),
        name="s5_scan",
    )(proj_a, tabs['w1re'], tabs['w1im'], tabs['w2re'], tabs['w2im'], tabs['kcat'], tabs['lampow'])


def _mixers_kernel(s5u_ref, cb_ref, cc_ref, ch_ref, su_ref, sv_ref, cch_ref, chh_ref, ys_ref,
                   d_ref, wglu_ref, cw_ref, lng_ref, lnb_ref, ws_ref, bs_ref, g_ref, o_ref):
    i = pl.program_id(0)
    tm = o_ref.shape[0]
    gw = GROUP_W
    y = ys_ref[...] + d_ref[...] * s5u_ref[...]
    y = jax.nn.gelu(y)
    y = y * jax.nn.sigmoid(jnp.dot(y.astype(BF16), wglu_ref[...], preferred_element_type=F32))
    o_ref[:, 0:gw] = _rms(y, g_ref[0:1, :]).astype(o_ref.dtype)
    z = cc_ref[...] * ch_ref[...]
    zh = jnp.where(i > 0, cch_ref[...] * chh_ref[...], 0.0)
    row = lax.broadcasted_iota(jnp.int32, z.shape, 0)
    z1 = jnp.where(row == 0, zh[7:8, :], pltpu.roll(z, 1, 0))
    z2 = jnp.where(row == 0, zh[6:7, :], jnp.where(row == 1, zh[7:8, :], pltpu.roll(z, 2, 0)))
    conv = cw_ref[0:1, :] * z2 + cw_ref[1:2, :] * z1 + cw_ref[2:3, :] * z
    o_ref[:, gw:2 * gw] = _rms(cb_ref[...] * conv, g_ref[1:2, :]).astype(o_ref.dtype)
    uu = jax.nn.gelu(su_ref[...])
    vv = _ln(jax.nn.gelu(sv_ref[...]), lng_ref[...], lnb_ref[...]).astype(BF16)
    pi = lax.broadcasted_iota(jnp.int32, (SGU_BLK, SGU_BLK), 0)
    pj = lax.broadcasted_iota(jnp.int32, (SGU_BLK, SGU_BLK), 1)
    causal = (pj // CHUNK) <= (pi // CHUNK)
    hd = gw // SGU_HEADS
    ws = [jnp.where(causal, ws_ref[h], 0.0).astype(BF16) for h in range(SGU_HEADS)]
    blocks = []
    for n in range(tm // SGU_BLK):
        vb = vv[n * SGU_BLK:(n + 1) * SGU_BLK, :]
        blocks.append(jnp.concatenate(
            [jnp.dot(ws[h], vb[:, h * hd:(h + 1) * hd], preferred_element_type=F32) for h in range(SGU_HEADS)],
            axis=1) + bs_ref[...])
    mixed = jnp.concatenate(blocks, axis=0)
    o_ref[:, 2 * gw:3 * gw] = _rms(uu * mixed, g_ref[2:3, :]).astype(o_ref.dtype)


def _mixers(proj_a, ys5, s5_d, w_glu, conv_w, ln_g, ln_b, sgu_w, sgu_b, mix_g, tm):
    L = proj_a.shape[0]
    gw = GROUP_W
    hb = tm // 8
    col = lambda c: pl.BlockSpec((tm, gw), lambda i, c=c: (i, c))
    halo = lambda c: pl.BlockSpec((8, gw), lambda i, c=c: (jnp.maximum(i * hb - 1, 0), c))
    full = lambda a: pl.BlockSpec(a.shape, lambda i: (0,) * a.ndim)
    hd = gw // SGU_HEADS
    bs_full = jnp.repeat(sgu_b.astype(F32).T, hd, axis=1)
    consts = [s5_d.reshape(1, gw).astype(F32), w_glu.astype(BF16), conv_w.astype(F32),
              ln_g.reshape(1, gw).astype(F32), ln_b.reshape(1, gw).astype(F32), sgu_w.astype(F32),
              bs_full, mix_g.reshape(3, gw).astype(F32)]
    return pl.pallas_call(
        _mixers_kernel,
        grid=(L // tm,),
        in_specs=[col(0), col(1), col(2), col(3), col(4), col(5), halo(2), halo(3),
                  pl.BlockSpec((tm, gw), lambda i: (i, 0))] + [full(a) for a in consts],
        out_specs=pl.BlockSpec((tm, 3 * gw), lambda i: (i, 0)),
        out_shape=jax.ShapeDtypeStruct((L, 3 * gw), BF16),
        compiler_params=_cparams(("parallel",)),
        name="row_mixers",
    )(proj_a, proj_a, proj_a, proj_a, proj_a, proj_a, proj_a, proj_a, ys5, *consts)


def _t5_bucket(rel):
    half = NUM_BUCKETS // 2
    ret = jnp.where(rel > 0, half, 0)
    n = jnp.abs(rel)
    max_exact = half // 2
    large = max_exact + (jnp.log(jnp.maximum(n, 1).astype(F32) / max_exact)
                         / math.log(MAX_DISTANCE / max_exact) * (half - max_exact)).astype(jnp.int32)
    large = jnp.minimum(large, half - 1)
    return ret + jnp.where(n < max_exact, n, large)


def _attn_bias_tables(rel_bias, tq):
    assert tq >= MAX_DISTANCE
    rb = rel_bias.astype(F32)
    far = rb[NUM_BUCKETS // 2 - 1]
    buckets = jnp.arange(NUM_BUCKETS)[:, None]

    def bias_of(rel):
        onehot = _t5_bucket(rel)[:, :, None, None] == buckets
        return jnp.sum(jnp.where(onehot, rb, 0.0), axis=2) - far

    kj = jnp.arange(tq)[:, None]
    qi = jnp.arange(tq)[None, :]
    diag = jnp.where(((kj // CHUNK) <= (qi // CHUNK))[..., None], bias_of(kj - qi), NEG_INF)
    prev = bias_of(kj - tq - qi)
    tabs = jnp.stack([jnp.transpose(diag, (2, 0, 1)), jnp.transpose(prev, (2, 0, 1))], axis=1)
    tabs = jnp.where(tabs > 0.5 * NEG_INF, tabs * LOG2E, NEG_INF)
    return jnp.concatenate([tabs, tabs], axis=3)


def _attn_kernel(qt_ref, k_ref, vt_ref, nb_ref, lq1_ref, lk1_ref, lq2_ref, lk2_ref, g_ref, o_ref,
                 qq_s, sa_s, sb_s, p_s, m_s, a_s, acc_s, *, lambda_init):
    i = pl.program_id(1)
    tq = qt_ref.shape[2]
    dq = DIFF_QK_DIM
    qt = qt_ref[0]
    feat = lax.broadcasted_iota(jnp.int32, qt.shape, 0)
    zero = jnp.zeros_like(qt)
    qq = jnp.concatenate([jnp.where(feat < dq, qt, zero), jnp.where(feat >= dq, qt, zero)], axis=1)

    qq_s[...] = qq
    m_s[...] = jnp.full(m_s.shape, -jnp.inf, F32)
    acc_s[...] = jnp.zeros(acc_s.shape, F32)

    def scores(s_ref, j, nblk):
        rows = nblk * tq
        kb = k_ref[pl.ds(pl.multiple_of(j * tq, tq), rows), :]
        s_ref[:rows, :] = jnp.dot(kb, qq_s[...], preferred_element_type=F32)

    def absorb(s_ref, j, nblk, bias=None):
        rows = nblk * tq
        for c in range(2 * tq // 128):
            cs = slice(c * 128, (c + 1) * 128)
            s = s_ref[:rows, cs]
            if bias is not None:
                s = s + bias(cs)
            m_old = m_s[:, cs]
            m_new = jnp.maximum(m_old, jnp.max(s, axis=0, keepdims=True))
            m_s[:, cs] = m_new
            a_s[:, cs] = jnp.exp2(m_old - m_new)
            p_s[:rows, cs] = jnp.exp2((s - m_new).astype(BF16))
        vt = jnp.concatenate([vt_ref[j + b] for b in range(nblk)], axis=1)
        vt = jnp.concatenate([vt, jnp.ones((ONES_ROWS, rows), BF16)], axis=0)
        acc_s[...] = a_s[...] * acc_s[...] + jnp.dot(vt, p_s[:rows, :], preferred_element_type=F32)

    scores(sa_s, i, 1)
    absorb(sa_s, i, 1, lambda cs: nb_ref[0, 0, :, cs])
    jp = jnp.maximum(i - 1, 0)
    first = jnp.where(i > 0, 0.0, NEG_INF)
    scores(sa_s, jp, 1)
    absorb(sa_s, jp, 1, lambda cs: nb_ref[0, 1, :, cs] + first)
    n_far = jnp.maximum(i - 1, 0)
    n_single = n_far % 2
    n_head = n_far % 4

    @pl.when(n_single == 1)
    def _():
        scores(sb_s, 0, 1)
        absorb(sb_s, 0, 1)

    @pl.when(n_head >= 2)
    def _():
        scores(sb_s, n_single, 2)
        absorb(sb_s, n_single, 2)

    scores(sa_s, n_head, 2)

    def quad(qd, c):
        j0 = n_head + 4 * qd
        scores(sb_s, j0 + 2, 2)
        absorb(sa_s, j0, 2)
        scores(sa_s, j0 + 4, 2)
        absorb(sb_s, j0 + 2, 2)
        return c

    lax.fori_loop(0, n_far // 4, quad, 0)
    o = acc_s[:DIFF_V_DIM, :] / acc_s[DIFF_V_DIM:DIFF_V_DIM + 1, :]
    lam = (jnp.exp(jnp.sum(lq1_ref[...] * lk1_ref[...], keepdims=True))
           - jnp.exp(jnp.sum(lq2_ref[...] * lk2_ref[...], keepdims=True)) + lambda_init)
    out = o[:, :tq] - lam * o[:, tq:]
    out = out * lax.rsqrt(jnp.mean(jnp.square(out), axis=0, keepdims=True) + EPS) * g_ref[...]
    o_ref[...] = (out * (1.0 - lambda_init)).T.astype(o_ref.dtype)


def _diff_attention(qvt, kmat, nb, lq1, lk1, lq2, lk2, subln_g, lambda_init):
    nq, _, tq = qvt.shape
    assert nq >= 2
    L = kmat.shape[0]
    H, dv = DIFF_HEADS, DIFF_V_DIM
    vec = lambda a: a.reshape(1, -1).astype(F32)
    small = lambda n: pl.BlockSpec((1, n), lambda h, i: (0, 0))
    return pl.pallas_call(
        functools.partial(_attn_kernel, lambda_init=lambda_init),
        grid=(H, nq),
        in_specs=[pl.BlockSpec((1, dv, tq), lambda h, i: (i, h, 0)),
                  pl.BlockSpec((L, dv), lambda h, i: (0, h)),
                  pl.BlockSpec((nq, dv, tq), lambda h, i: (0, H + h, 0)),
                  pl.BlockSpec((1, 2, tq, 2 * tq), lambda h, i: (h, 0, 0, 0)),
                  small(DIFF_QK_DIM), small(DIFF_QK_DIM), small(DIFF_QK_DIM), small(DIFF_QK_DIM),
                  pl.BlockSpec((dv, 1), lambda h, i: (0, 0))],
        out_specs=pl.BlockSpec((tq, dv), lambda h, i: (i, h)),
        out_shape=jax.ShapeDtypeStruct((L, H * dv), BF16),
        scratch_shapes=[pltpu.VMEM((dv, 2 * tq), BF16),
                        pltpu.VMEM((2 * tq, 2 * tq), F32), pltpu.VMEM((2 * tq, 2 * tq), F32),
                        pltpu.VMEM((2 * tq, 2 * tq), BF16),
                        pltpu.VMEM((1, 2 * tq), F32), pltpu.VMEM((1, 2 * tq), F32),
                        pltpu.VMEM((dv + ONES_ROWS, 2 * tq), F32)],
        compiler_params=_cparams(("parallel", "arbitrary")),
        name="diff_attention",
    )(qvt, kmat, qvt, nb, vec(lq1), vec(lk1), vec(lq2), vec(lk2), subln_g.reshape(dv, 1).astype(F32))


def _outproj_kernel(abc_ref, d_ref, x_ref, wa_ref, wd_ref, g_ref, b_ref, rhi_ref, rlo_ref,
                    x1_ref, x1b_ref, x1p_ref, lg_ref, *, alpha):
    nt = (((1,), (1,)), ((), ()))
    tm = x_ref.shape[0]
    sub = min(tm, SUB_ROWS)
    for r0 in range(0, tm, sub):
        rows = pl.ds(r0, sub)
        mix = (jnp.dot(abc_ref[rows, :], wa_ref[...], preferred_element_type=F32)
               + jnp.dot(d_ref[rows, :], wd_ref[...], preferred_element_type=F32))
        x1 = _ln(alpha * x_ref[rows, :] + mix, g_ref[...], b_ref[...])
        x1_ref[rows, :] = x1
        hi = x1.astype(BF16)
        x1b_ref[rows, :] = hi
        _store_token_tiles(x1p_ref.at[pl.ds(r0 * ROW_TILE, sub * ROW_TILE), :], x1)
        lo = (x1 - hi.astype(F32)).astype(BF16)
        lg_ref[:, r0:r0 + sub] = (lax.dot_general(rhi_ref[...], hi, nt, preferred_element_type=F32)
                                  + lax.dot_general(rhi_ref[...], lo, nt, preferred_element_type=F32)
                                  + lax.dot_general(rlo_ref[...], hi, nt, preferred_element_type=F32))


def _outproj(abc, d_out, x, w_out, ln_g, ln_b, router_w, alpha, tm):
    L, D = x.shape
    E = router_w.shape[1]
    ka = abc.shape[1]
    kd = d_out.shape[1]
    rwt = router_w.astype(F32).T
    rhi = rwt.astype(BF16)
    rlo = (rwt - rhi.astype(F32)).astype(BF16)
    full = lambda shape: pl.BlockSpec(shape, lambda i: (0,) * len(shape))
    return pl.pallas_call(
        functools.partial(_outproj_kernel, alpha=alpha),
        grid=(L // tm,),
        in_specs=[pl.BlockSpec((tm, ka), lambda i: (i, 0)),
                  pl.BlockSpec((tm, kd), lambda i: (i, 0)),
                  pl.BlockSpec((tm, D), lambda i: (i, 0)),
                  pl.BlockSpec((ka, D), lambda i: (0, 0)),
                  pl.BlockSpec((kd, D), lambda i: (ka // kd, 0)),
                  full((1, D)), full((1, D)), full((E, D)), full((E, D))],
        out_specs=[pl.BlockSpec((tm, D), lambda i: (i, 0)),
                   pl.BlockSpec((tm, D), lambda i: (i, 0)),
                   pl.BlockSpec((tm * ROW_TILE, LANES), lambda i: (i, 0)),
                   pl.BlockSpec((E, tm), lambda i: (0, i))],
        out_shape=[jax.ShapeDtypeStruct((L, D), F32), jax.ShapeDtypeStruct((L, D), BF16),
                   jax.ShapeDtypeStruct((L * ROW_TILE, LANES), jnp.uint32), jax.ShapeDtypeStruct((E, L), F32)],
        compiler_params=_cparams(("parallel",)),
        name="outproj_ln1",
    )(abc, d_out, x, w_out, w_out, ln_g.reshape(1, D).astype(F32), ln_b.reshape(1, D).astype(F32), rhi, rlo)


def _router_kernel(lg_ref, bias_ref, tri_ref, e_ref, r_ref, w_ref, cnt_ref, carry_ref):
    i = pl.program_id(0)
    E, tn = lg_ref.shape
    ng = N_EXPERT_GROUPS
    gs_ = E // ng

    @pl.when(i == 0)
    def _():
        carry_ref[...] = jnp.zeros_like(carry_ref)

    s = jax.nn.sigmoid(lg_ref[...])
    sel = s + bias_ref[...]
    midx = lax.broadcasted_iota(jnp.int32, (gs_, tn), 0).astype(F32)
    rows, gscore = [], []
    for g in range(ng):
        rg = sel[g * gs_:(g + 1) * gs_, :]
        m1 = jnp.max(rg, axis=0, keepdims=True)
        first = jnp.min(jnp.where(rg == m1, midx, float(gs_)), axis=0, keepdims=True)
        m2 = jnp.max(jnp.where(midx == first, -jnp.inf, rg), axis=0, keepdims=True)
        rows.append(rg)
        gscore.append(m1 + m2)
    vals = []
    for g in range(ng):
        rank = jnp.zeros((1, tn), F32)
        for o in range(ng):
            if o != g:
                beats = (gscore[o] >= gscore[g]) if o < g else (gscore[o] > gscore[g])
                rank = rank + jnp.where(beats, 1.0, 0.0)
        vals.append(jnp.where(rank < TOPK_GROUPS, rows[g], -jnp.inf))
    val = jnp.concatenate(vals, axis=0)
    eidx = lax.broadcasted_iota(jnp.int32, val.shape, 0)
    erank = jnp.zeros(val.shape, F32)
    for e in range(E):
        other = val[e:e + 1, :]
        erank = erank + jnp.where(eidx > e, jnp.where(other >= val, 1.0, 0.0), jnp.where(other > val, 1.0, 0.0))
    chosen = erank < TOP_K
    wsel = jnp.where(chosen, s, 0.0)
    wn = wsel / (jnp.sum(wsel, axis=0, keepdims=True) + 1e-20) * ROUTED_SCALE
    chf = jnp.where(chosen, 1.0, 0.0)
    incl = jnp.dot(chf.astype(BF16), tri_ref[...], preferred_element_type=F32)
    base = carry_ref[...]
    pos = base + incl - chf
    carry_ref[...] = base + incl[:, tn - 1:tn]
    cnt_ref[...] = (base + incl[:, tn - 1:tn]).astype(jnp.int32)
    eidf = eidx.astype(F32)
    cand = jnp.where(chosen, eidf, float(E))
    for k in range(TOP_K):
        ek = jnp.min(cand, axis=0, keepdims=True)
        hit = cand == ek
        e_ref[k:k + 1, :] = ek.astype(jnp.int32)
        r_ref[k:k + 1, :] = jnp.sum(jnp.where(hit, pos, 0.0), axis=0, keepdims=True).astype(jnp.int32)
        w_ref[k:k + 1, :] = jnp.sum(jnp.where(hit, wn, 0.0), axis=0, keepdims=True)
        cand = jnp.where(hit, float(E), cand)


def _router(logits_t, router_bias, tn):
    E, L = logits_t.shape
    tri = (jnp.arange(tn)[:, None] <= jnp.arange(tn)[None, :]).astype(BF16)
    slot = lambda dt: jax.ShapeDtypeStruct((TOP_K, L), dt)
    return pl.pallas_call(
        _router_kernel,
        grid=(L // tn,),
        in_specs=[pl.BlockSpec((E, tn), lambda i: (0, i)),
                  pl.BlockSpec((E, 1), lambda i: (0, 0)),
                  pl.BlockSpec((tn, tn), lambda i: (0, 0))],
        out_specs=[pl.BlockSpec((TOP_K, tn), lambda i: (0, i)),
                   pl.BlockSpec((TOP_K, tn), lambda i: (0, i)),
                   pl.BlockSpec((TOP_K, tn), lambda i: (0, i)),
                   pl.BlockSpec((E, 1), lambda i: (0, 0))],
        out_shape=[slot(jnp.int32), slot(jnp.int32), slot(F32), jax.ShapeDtypeStruct((E, 1), jnp.int32)],
        scratch_shapes=[pltpu.VMEM((E, 1), F32)],
        compiler_params=_cparams(("arbitrary",)),
        name="router_topk",
    )(logits_t, router_bias.reshape(E, 1).astype(F32), tri)


def _dispatch_kernel(dest_ref, pad_ref, x_ref, xs_ref, zero_ref, sem, zsem):
    i = pl.program_id(0)
    tm = x_ref.shape[0] // ROW_TILE

    def tile(ref, t):
        return ref.at[pl.ds(pl.multiple_of(t * ROW_TILE, ROW_TILE), ROW_TILE), :]

    def row_copy(r, k):
        return pltpu.make_async_copy(tile(x_ref, r), tile(xs_ref, dest_ref[k, r]), sem)

    def zero_copy(dst):
        return pltpu.make_async_copy(zero_ref, tile(xs_ref, dst), zsem)

    @pl.when(i == 0)
    def _():
        zero_ref[...] = jnp.zeros_like(zero_ref)

        def per_expert(e, c):
            lo, hi = pad_ref[0, e], pad_ref[1, e]
            lax.fori_loop(lo, hi, lambda d, c2: (zero_copy(d).start(), c2)[1], 0)
            lax.fori_loop(lo, hi, lambda d, c2: (zero_copy(d).wait(), c2)[1], 0)
            return c
        lax.fori_loop(0, N_EXPERTS, per_expert, 0)

    def issue(r, c):
        for k in range(TOP_K):
            row_copy(r, k).start()
        return c
    lax.fori_loop(0, tm, issue, 0)

    def drain(r, c):
        for k in range(TOP_K):
            row_copy(r, k).wait()
        return c
    lax.fori_loop(0, tm, drain, 0)


def _dispatch(x1p, dest, pad_lo_hi, n_slots, tm):
    L = x1p.shape[0] // ROW_TILE
    return pl.pallas_call(
        _dispatch_kernel,
        grid=(L // tm,),
        in_specs=[pl.BlockSpec((TOP_K, tm), lambda i: (0, i), memory_space=pltpu.SMEM),
                  pl.BlockSpec(memory_space=pltpu.SMEM),
                  pl.BlockSpec((tm * ROW_TILE, LANES), lambda i: (i, 0))],
        out_specs=pl.BlockSpec(memory_space=pl.ANY),
        out_shape=jax.ShapeDtypeStruct((n_slots * ROW_TILE, LANES), x1p.dtype),
        scratch_shapes=[pltpu.VMEM((ROW_TILE, LANES), x1p.dtype), pltpu.SemaphoreType.DMA(()),
                        pltpu.SemaphoreType.DMA(())],
        compiler_params=_cparams(("arbitrary",)),
        name="moe_dispatch",
    )(dest, pad_lo_hi, x1p)


def _experts_kernel(be_ref, nb_ref, xs_ref, wgu_ref, wd_ref, y_ref, wgu_b, wd_b):
    b = pl.program_id(0)

    @pl.when((b == 0) | (be_ref[b] != be_ref[jnp.maximum(b - 1, 0)]))
    def _():
        wgu_b[...] = wgu_ref[0, 0].astype(BF16)
        wd_b[...] = wd_ref[0, 0].astype(BF16)

    @pl.when(b < nb_ref[0])
    def _():
        de = wd_b.shape[0]
        xa, xb = _load_token_tiles(xs_ref, xs_ref.shape[0] // ROW_TILE)
        x = jnp.concatenate([xa.astype(BF16), xb.astype(BF16)], axis=1)
        h = jnp.dot(x, wgu_b[...], preferred_element_type=F32)
        a = jax.nn.silu(h[:, :de]) * h[:, de:]
        _store_token_tiles(y_ref, jnp.dot(a.astype(BF16), wd_b[...], preferred_element_type=F32))


def _experts(xs, block_e, n_used, w_gu, w_down, layer, blk):
    n_slots = xs.shape[0] // ROW_TILE
    nblk = n_slots // blk
    rows = blk * ROW_TILE
    D = w_gu.shape[2]
    de2 = w_gu.shape[3]
    de = w_down.shape[2]
    last = lambda b, be, nb: jnp.minimum(b, nb[0] - 1)
    return pl.pallas_call(
        _experts_kernel,
        grid_spec=pltpu.PrefetchScalarGridSpec(
            num_scalar_prefetch=2,
            grid=(nblk,),
            in_specs=[pl.BlockSpec((rows, LANES), lambda b, be, nb: (last(b, be, nb), 0)),
                      pl.BlockSpec((1, 1, D, de2), lambda b, be, nb: (layer, be[b], 0, 0)),
                      pl.BlockSpec((1, 1, de, D), lambda b, be, nb: (layer, be[b], 0, 0))],
            out_specs=pl.BlockSpec((rows, LANES), lambda b, be, nb: (last(b, be, nb), 0)),
            scratch_shapes=[pltpu.VMEM((D, de2), BF16), pltpu.VMEM((de, D), BF16)]),
        out_shape=jax.ShapeDtypeStruct(xs.shape, jnp.uint32),
        compiler_params=_cparams(("arbitrary",)),
        name="moe_experts",
    )(block_e, n_used, xs, w_gu, w_down)


def _combine_kernel(dcur_ref, dnxt_ref, y_ref, w_ref, x1_ref, x1b_ref, sgu_ref, sdn_ref, g_ref, b_ref,
                    x2_ref, x2b_ref, buf, sem, *, alpha):
    i = pl.program_id(0)
    n = pl.num_programs(0)
    tm = x1_ref.shape[0]
    slot = i % 2

    def row_copy(d_ref, s, r, k):
        src = y_ref.at[pl.ds(pl.multiple_of(d_ref[k, r] * ROW_TILE, ROW_TILE), ROW_TILE), :]
        dst = buf.at[s, k, pl.ds(pl.multiple_of(r * ROW_TILE, ROW_TILE), ROW_TILE), :]
        return pltpu.make_async_copy(src, dst, sem.at[s])

    def issue(d_ref, s):
        def body(r, c):
            for k in range(TOP_K):
                row_copy(d_ref, s, r, k).start()
            return c
        lax.fori_loop(0, tm, body, 0)

    @pl.when(i == 0)
    def _():
        issue(dcur_ref, 0)

    @pl.when(i + 1 < n)
    def _():
        issue(dnxt_ref, 1 - slot)

    de = sdn_ref.shape[0]
    h = jnp.dot(x1b_ref[...], sgu_ref[...], preferred_element_type=F32)
    a = jax.nn.silu(h[:, :de]) * h[:, de:]
    ffn = jnp.dot(a.astype(BF16), sdn_ref[...], preferred_element_type=F32)

    def drain(r, c):
        for k in range(TOP_K):
            row_copy(dcur_ref, slot, r, k).wait()
        return c
    lax.fori_loop(0, tm, drain, 0)

    half = ffn.shape[1] // 2
    fa, fb = ffn[:, :half], ffn[:, half:]
    for k in range(TOP_K):
        ya, yb = _load_token_tiles(buf.at[slot, k], tm)
        w = w_ref[:, k:k + 1]
        fa = fa + ya * w
        fb = fb + yb * w
    ffn = jnp.concatenate([fa, fb], axis=1)
    x2 = _ln(alpha * x1_ref[...] + ffn, g_ref[...], b_ref[...])
    x2_ref[...] = x2
    x2b_ref[...] = x2.astype(BF16)


def _combine(dest, y, w_t, x1, x1b, sh_gu, sh_down, ln_g, ln_b, alpha, tm):
    L, D = x1.shape
    n = L // tm
    full = lambda shape: pl.BlockSpec(shape, lambda i: (0,) * len(shape))
    return pl.pallas_call(
        functools.partial(_combine_kernel, alpha=alpha),
        grid=(n,),
        in_specs=[pl.BlockSpec((TOP_K, tm), lambda i: (0, i), memory_space=pltpu.SMEM),
                  pl.BlockSpec((TOP_K, tm), lambda i: (0, jnp.minimum(i + 1, n - 1)), memory_space=pltpu.SMEM),
                  pl.BlockSpec(memory_space=pl.ANY),
                  pl.BlockSpec((tm, TOP_K), lambda i: (i, 0)),
                  pl.BlockSpec((tm, D), lambda i: (i, 0)),
                  pl.BlockSpec((tm, D), lambda i: (i, 0)),
                  full(sh_gu.shape), full(sh_down.shape), full((1, D)), full((1, D))],
        out_specs=[pl.BlockSpec((tm, D), lambda i: (i, 0)), pl.BlockSpec((tm, D), lambda i: (i, 0))],
        out_shape=[jax.ShapeDtypeStruct((L, D), F32), jax.ShapeDtypeStruct((L, D), BF16)],
        scratch_shapes=[pltpu.VMEM((2, TOP_K, tm * ROW_TILE, LANES), y.dtype), pltpu.SemaphoreType.DMA((2,))],
        compiler_params=_cparams(("arbitrary",)),
        name="moe_combine_ln2",
    )(dest, dest, y, w_t, x1, x1b, sh_gu, sh_down, ln_g.reshape(1, D).astype(F32), ln_b.reshape(1, D).astype(F32))


def _experts_fused_kernel(blk_ref, exp_ref, nv_ref,
                          src_ref, srcn_ref, dst_ref, x_hbm, wgu_ref, wd_ref, y_hbm,
                          xbuf, ybuf, wgu_b, wd_b, gsem, ssem):
    w = pl.program_id(0)
    nv = nv_ref[0]
    R = xbuf.shape[1] // ROW_TILE
    par = w % 2

    def tile(buf, r):
        return buf.at[pl.ds(r * ROW_TILE, ROW_TILE), :]

    def gather(s_ref, p):
        for r in range(R):
            pltpu.make_async_copy(x_hbm.at[s_ref[0, 0, r]], tile(xbuf.at[p], r), gsem.at[p]).start()

    def gather_wait(p):
        pltpu.make_async_copy(xbuf.at[1 - p], xbuf.at[p], gsem.at[p]).wait()

    def scatter_wait(p):
        pltpu.make_async_copy(ybuf.at[p], ybuf.at[1 - p], ssem.at[p]).wait()

    @pl.when(w == 0)
    def _():
        gather(src_ref, 0)

    @pl.when((w < nv) & ((w == 0) | (exp_ref[w] != exp_ref[jnp.maximum(w - 1, 0)])))
    def _():
        wgu_b[...] = wgu_ref[0, 0].astype(BF16)
        wd_b[...] = wd_ref[0, 0].astype(BF16)

    @pl.when((w >= 2) & (w < nv))
    def _():
        scatter_wait(par)

    @pl.when(w < nv)
    def _():
        gather_wait(par)
        gather(srcn_ref, 1 - par)
        de = wd_b.shape[0]
        xa, xb = _load_token_tiles(xbuf.at[par], R)
        x = jnp.concatenate([xa.astype(BF16), xb.astype(BF16)], axis=1)
        h = jnp.dot(x, wgu_b[...], preferred_element_type=F32)
        a = jax.nn.silu(h[:, :de]) * h[:, de:]
        _store_token_tiles(ybuf.at[par], jnp.dot(a.astype(BF16), wd_b[...], preferred_element_type=F32))
        for r in range(R):
            pltpu.make_async_copy(tile(ybuf.at[par], r), y_hbm.at[dst_ref[0, 0, r]], ssem.at[par]).start()

    @pl.when(w == nv - 1)
    def _():
        scatter_wait(par)
        gather_wait(1 - par)

    @pl.when((w == nv - 1) & (w >= 1))
    def _():
        scatter_wait(1 - par)


def _experts_fused(x1p, src, dst, items, w_gu, w_down, layer, n_out_tiles, blk):
    blk_w, exp_w, n_valid = items
    n_items = blk_w.shape[0]
    D, de2 = w_gu.shape[2], w_gu.shape[3]
    de = w_down.shape[2]
    cur = lambda w, nv: jnp.minimum(w, nv[0] - 1)
    return pl.pallas_call(
        _experts_fused_kernel,
        grid_spec=pltpu.PrefetchScalarGridSpec(
            num_scalar_prefetch=3,
            grid=(n_items,),
            in_specs=[pl.BlockSpec((1, 1, blk), lambda w, b, e, nv: (b[cur(w, nv)], 0, 0),
                                   memory_space=pltpu.SMEM),
                      pl.BlockSpec((1, 1, blk), lambda w, b, e, nv: (b[cur(w + 1, nv)], 0, 0),
                                   memory_space=pltpu.SMEM),
                      pl.BlockSpec((1, 1, blk), lambda w, b, e, nv: (cur(w, nv), 0, 0),
                                   memory_space=pltpu.SMEM),
                      pl.BlockSpec(memory_space=pl.ANY),
                      pl.BlockSpec((1, 1, D, de2), lambda w, b, e, nv: (layer, e[cur(w, nv)], 0, 0)),
                      pl.BlockSpec((1, 1, de, D), lambda w, b, e, nv: (layer, e[cur(w, nv)], 0, 0))],
            out_specs=pl.BlockSpec(memory_space=pl.ANY),
            scratch_shapes=[pltpu.VMEM((2, blk * ROW_TILE, LANES), jnp.uint32),
                            pltpu.VMEM((2, blk * ROW_TILE, LANES), jnp.uint32),
                            pltpu.VMEM((D, de2), BF16), pltpu.VMEM((de, D), BF16),
                            pltpu.SemaphoreType.DMA((2,)), pltpu.SemaphoreType.DMA((2,))]),
        out_shape=jax.ShapeDtypeStruct((n_out_tiles, ROW_TILE, LANES), jnp.uint32),
        compiler_params=_cparams(("arbitrary",)),
        name="moe_experts_fused",
    )(blk_w, exp_w, n_valid, src, src, dst, x1p.reshape(-1, ROW_TILE, LANES), w_gu, w_down).reshape(-1, LANES)


def _combine_stream_kernel(*refs, alpha):
    y_refs = refs[:TOP_K]
    w_ref, x1_ref, x1b_ref, sgu_ref, sdn_ref, g_ref, b_ref, x2_ref, x2b_ref = refs[TOP_K:]
    tm = x1_ref.shape[0]
    de = sdn_ref.shape[0]
    h = jnp.dot(x1b_ref[...], sgu_ref[...], preferred_element_type=F32)
    a = jax.nn.silu(h[:, :de]) * h[:, de:]
    ffn = jnp.dot(a.astype(BF16), sdn_ref[...], preferred_element_type=F32)
    half = ffn.shape[1] // 2
    fa, fb = ffn[:, :half], ffn[:, half:]
    for k in range(TOP_K):
        ya, yb = _load_token_tiles(y_refs[k], tm)
        w = w_ref[:, k:k + 1]
        fa = fa + ya * w
        fb = fb + yb * w
    ffn = jnp.concatenate([fa, fb], axis=1)
    x2 = _ln(alpha * x1_ref[...] + ffn, g_ref[...], b_ref[...])
    x2_ref[...] = x2
    x2b_ref[...] = x2.astype(BF16)


def _combine_stream(yk, w_t, x1, x1b, sh_gu, sh_down, ln_g, ln_b, alpha, tm):
    L, D = x1.shape
    n = L // tm
    full = lambda shape: pl.BlockSpec(shape, lambda i: (0,) * len(shape))
    y_specs = [pl.BlockSpec((tm * ROW_TILE, LANES), lambda i, k=k: (k * n + i, 0)) for k in range(TOP_K)]
    return pl.pallas_call(
        functools.partial(_combine_stream_kernel, alpha=alpha),
        grid=(n,),
        in_specs=y_specs + [pl.BlockSpec((tm, TOP_K), lambda i: (i, 0)),
                            pl.BlockSpec((tm, D), lambda i: (i, 0)),
                            pl.BlockSpec((tm, D), lambda i: (i, 0)),
                            full(sh_gu.shape), full(sh_down.shape), full((1, D)), full((1, D))],
        out_specs=[pl.BlockSpec((tm, D), lambda i: (i, 0)), pl.BlockSpec((tm, D), lambda i: (i, 0))],
        out_shape=[jax.ShapeDtypeStruct((L, D), F32), jax.ShapeDtypeStruct((L, D), BF16)],
        compiler_params=_cparams(("arbitrary",)),
        name="moe_combine_ln2",
    )(*([yk] * TOP_K), w_t, x1, x1b, sh_gu, sh_down, ln_g.reshape(1, D).astype(F32), ln_b.reshape(1, D).astype(F32))


def _moe_layer_fused(x1, x1b, x1p, logits_t, router_bias, w_gu, w_down, layer, sh_gu, sh_down, ln_g, ln_b, alpha,
                     router_tn, combine_tm):
    L, D = x1.shape
    E, K, R = N_EXPERTS, TOP_K, MOE_BLK
    A = K * L
    assert A % R == 0
    nblk = A // R
    e_k, _, w_k, counts = _router(logits_t, router_bias, router_tn)
    keys = (e_k * L + jnp.arange(L, dtype=jnp.int32)[None, :]) * K + jnp.arange(K, dtype=jnp.int32)[:, None]
    skeys = jnp.sort(keys.reshape(A))
    tok = (skeys // K) % L
    src = tok.reshape(nblk, 1, R)
    dst_sorted = ((skeys % K) * L + tok).reshape(nblk, R)
    ends = jnp.cumsum(counts.reshape(E))
    cuts = jnp.sort(jnp.concatenate([jnp.arange(nblk, dtype=jnp.int32) * R, (ends - counts.reshape(E))]))
    lo = cuts
    hi = jnp.concatenate([cuts[1:], jnp.full((1,), A, jnp.int32)])
    valid = hi > lo
    order = jnp.argsort(jnp.logical_not(valid), stable=True)
    lo, hi = lo[order].astype(jnp.int32), hi[order].astype(jnp.int32)
    n_valid = jnp.sum(valid).astype(jnp.int32).reshape(1)
    blk_w = jnp.minimum(lo // R, nblk - 1).astype(jnp.int32)
    exp_w = jnp.minimum(jnp.sum((ends[None, :] <= lo[:, None]).astype(jnp.int32), axis=1), E - 1).astype(jnp.int32)
    n_out_tiles = A + 2 * R
    n_items = blk_w.shape[0]
    onehot = (blk_w[:, None] == jnp.arange(nblk, dtype=jnp.int32)[None, :]).astype(F32)
    dst_rows = jnp.dot(onehot, dst_sorted.astype(F32), precision=lax.Precision.HIGHEST).astype(jnp.int32)
    pos = blk_w[:, None] * R + jnp.arange(R, dtype=jnp.int32)[None, :]
    spare = (A + (jnp.arange(n_items, dtype=jnp.int32)[:, None] % 2) * R
             + jnp.arange(R, dtype=jnp.int32)[None, :])
    dst = jnp.where((pos >= lo[:, None]) & (pos < hi[:, None]), dst_rows, spare).reshape(n_items, 1, R)
    yk = _experts_fused(x1p, src, dst, (blk_w, exp_w, n_valid), w_gu, w_down, layer, n_out_tiles, R)
    return _combine_stream(yk, w_k.T, x1, x1b, sh_gu, sh_down, ln_g, ln_b, alpha, combine_tm)


MOE_BLK = 256


def _moe_layer(x1, x1b, x1p, logits_t, router_bias, w_gu, w_down, layer, sh_gu, sh_down, ln_g, ln_b, alpha,
               router_tn, dispatch_tm, combine_tm):
    L, D = x1.shape
    E = N_EXPERTS
    e_k, r_k, w_k, counts = _router(logits_t, router_bias, router_tn)
    counts = counts.reshape(E)
    padded = (counts + MOE_BLK - 1) // MOE_BLK * MOE_BLK
    pends = jnp.cumsum(padded)
    pstarts = pends - padded
    dest = jnp.sum(jnp.where(e_k[..., None] == jnp.arange(E), pstarts, 0), axis=-1) + r_k
    nblk = -(-(L * TOP_K) // MOE_BLK) + E
    n_used = (pends[-1] // MOE_BLK).astype(jnp.int32)
    blocks = jnp.minimum(jnp.arange(nblk, dtype=jnp.int32), n_used - 1)
    block_e = jnp.sum((pends[None, :] <= (blocks * MOE_BLK)[:, None]).astype(jnp.int32), axis=1)
    block_e = jnp.minimum(block_e, E - 1).astype(jnp.int32)
    pad_lo_hi = jnp.stack([pstarts + counts, pends]).astype(jnp.int32)
    xs = _dispatch(x1p, dest, pad_lo_hi, nblk * MOE_BLK, dispatch_tm)
    y = _experts(xs, block_e, n_used.reshape(1), w_gu, w_down, layer, MOE_BLK)
    return _combine(dest, y, w_k.T, x1, x1b, sh_gu, sh_down, ln_g, ln_b, alpha, combine_tm)


def _pick(n, pref):
    t = min(n, pref)
    assert n % t == 0
    return t


def kernel(x, w_in, w_out, mix_norm_g, s5_lambda_re, s5_lambda_im, s5_log_dt, s5_b_re, s5_b_im, s5_c_re, s5_c_im, s5_d, s5_w_glu, conv_w, sgu_ln_g, sgu_ln_b, sgu_w, sgu_b, diff_lq1, diff_lk1, diff_lq2, diff_lk2, diff_subln_g, rel_bias, ln1_g, ln1_b, router_w, router_bias, moe_w_gu, moe_w_down, shared_w_gu, shared_w_down, ln2_g, ln2_b):
    Bt, L, D = x.shape
    assert Bt == 1
    depth = w_in.shape[0]
    alpha = (2 * depth) ** 0.25
    gw = GROUP_W
    tq = _pick(L, 256)
    nb = _attn_bias_tables(rel_bias, tq)
    xf = x.reshape(L, D)
    xb = xf.astype(BF16)
    for l in range(depth):
        w_in_b = w_in[l].astype(BF16)
        proj_a = _matmul(xb, w_in_b[:, :6 * gw], _pick(L, 1024), 512, F32)
        kmat = _matmul(xb, w_in_b[:, 7 * gw:8 * gw], _pick(L, 1024), 512, BF16)
        w_q = (w_in[l][:, 6 * gw:7 * gw] * (DIFF_QK_DIM ** -0.5 * LOG2E)).astype(BF16)
        w_qv_t = jnp.concatenate([w_q, w_in_b[:, 8 * gw:]], axis=1).T
        qvt = _matmul_nt(w_qv_t, xb, tq, BF16)
        s5_rt = _pick(L // S5_SUB, 256)
        tabs = _s5_tables(s5_lambda_re[l], s5_lambda_im[l], s5_log_dt[l], s5_b_re[l], s5_b_im[l],
                          s5_c_re[l], s5_c_im[l], s5_rt)
        ys5 = _s5_scan(proj_a, tabs, s5_rt)
        abc = _mixers(proj_a, ys5, s5_d[l], s5_w_glu[l], conv_w[l], sgu_ln_g[l], sgu_ln_b[l], sgu_w[l], sgu_b[l],
                      mix_norm_g[l], _pick(L, 512))
        lambda_init = 0.8 - 0.6 * math.exp(-0.3 * l)
        d_out = _diff_attention(qvt, kmat, nb, diff_lq1[l], diff_lk1[l], diff_lq2[l], diff_lk2[l],
                                diff_subln_g[l], lambda_init)
        x1, x1b, x1p, logits_t = _outproj(abc, d_out, xf, w_out[l].astype(BF16), ln1_g[l], ln1_b[l], router_w[l],
                                          alpha, _pick(L, 256))
        xf, xb = _moe_layer_fused(x1, x1b, x1p, logits_t, router_bias[l], moe_w_gu, moe_w_down, l,
                                  shared_w_gu[l].astype(BF16), shared_w_down[l].astype(BF16),
                                  ln2_g[l], ln2_b[l], alpha, _pick(L, 512), _pick(L, 256))
    return xf.reshape(Bt, L, D)
```

```python
import functools
import math

import jax
import jax.numpy as jnp
from jax import lax
from jax.experimental import pallas as pl
from jax.experimental.pallas import tpu as pltpu

F32 = jnp.float32
BF16 = jnp.bfloat16

GROUP_W = 512
CHUNK = 64
S5_GROUP_CH = 16
S5_GROUPS = 32
S5_STATE = 64
S5_SUB = 16
S5_KBLOCKS = 4
SGU_BLK = 128
SGU_HEADS = 4
DIFF_HEADS = 4
DIFF_QK_DIM = 64
DIFF_V_DIM = 128
NUM_BUCKETS = 32
MAX_DISTANCE = 128
N_EXPERTS = 64
TOP_K = 8
N_EXPERT_GROUPS = 8
TOPK_GROUPS = 4
ROUTED_SCALE = 2.5
EPS = 1e-5
NEG_INF = -1e30
LOG2E = math.log2(math.e)
ONES_ROWS = 16
ATTN_GROUP = 2

VMEM_LIMIT = 56 * 1024 * 1024


def _cparams(sem):
    return pltpu.CompilerParams(dimension_semantics=sem, vmem_limit_bytes=VMEM_LIMIT)


def _rms(x, g):
    return x * lax.rsqrt(jnp.mean(jnp.square(x), -1, keepdims=True) + EPS) * g


def _pack_bf16_pairs(x):
    c = x.shape[1] // 2
    hi = lax.bitcast_convert_type(x[:, :c].astype(BF16).astype(F32), jnp.uint32)
    lo = lax.bitcast_convert_type(x[:, c:].astype(BF16).astype(F32), jnp.uint32)
    return hi | (lo >> 16)


def _unpack_bf16_pairs(u):
    hi = lax.bitcast_convert_type(u & jnp.uint32(0xFFFF0000), F32)
    lo = lax.bitcast_convert_type(u << 16, F32)
    return hi, lo


ROW_TILE = 8
LANES = 128
SUB_ROWS = 128


def _store_token_tiles(ref, x):
    p = _pack_bf16_pairs(x)
    n = x.shape[0]
    for c in range(ROW_TILE):
        ref[pl.ds(c, n, stride=ROW_TILE), :] = p[:, c * LANES:(c + 1) * LANES]


def _load_token_tiles(ref, n):
    p = jnp.concatenate([ref[pl.ds(c, n, stride=ROW_TILE), :] for c in range(ROW_TILE)], axis=1)
    return _unpack_bf16_pairs(p)


def _ln(x, g, b):
    mu = jnp.mean(x, -1, keepdims=True)
    var = jnp.mean(jnp.square(x - mu), -1, keepdims=True)
    return (x - mu) * lax.rsqrt(var + EPS) * g + b


def _matmul_kernel(x_ref, w_ref, o_ref):
    o_ref[...] = jnp.dot(x_ref[...], w_ref[...], preferred_element_type=F32).astype(o_ref.dtype)


def _matmul(x, w, tm, tn, out_dtype):
    M, K = x.shape
    N = w.shape[1]
    return pl.pallas_call(
        _matmul_kernel,
        grid=(M // tm, N // tn),
        in_specs=[pl.BlockSpec((tm, K), lambda i, j: (i, 0)),
                  pl.BlockSpec((K, tn), lambda i, j: (0, j))],
        out_specs=pl.BlockSpec((tm, tn), lambda i, j: (i, j)),
        out_shape=jax.ShapeDtypeStruct((M, N), out_dtype),
        compiler_params=_cparams(("parallel", "arbitrary")),
        name="proj_matmul",
    )(x, w)


def _matmul_nt_kernel(w_ref, x_ref, o_ref):
    o_ref[0] = lax.dot_general(w_ref[...], x_ref[...], (((1,), (1,)), ((), ())),
                               preferred_element_type=F32).astype(o_ref.dtype)


def _matmul_nt(w_t, x, tm, out_dtype):
    M, K = x.shape
    N = w_t.shape[0]
    return pl.pallas_call(
        _matmul_nt_kernel,
        grid=(M // tm,),
        in_specs=[pl.BlockSpec((N, K), lambda i: (0, 0)),
                  pl.BlockSpec((tm, K), lambda i: (i, 0))],
        out_specs=pl.BlockSpec((1, N, tm), lambda i: (i, 0, 0)),
        out_shape=jax.ShapeDtypeStruct((M // tm, N, tm), out_dtype),
        compiler_params=_cparams(("parallel",)),
        name="proj_matmul_nt",
    )(w_t, x)


def _s5_tables(lam_re, lam_im, log_dt, b_re, b_im, c_re, c_im, n_rows):
    G, P, H, S = S5_GROUPS, S5_STATE, S5_GROUP_CH, S5_SUB
    hp = lax.Precision.HIGHEST
    dt = jnp.exp(log_dt.astype(F32))[:, None]
    lam = lax.complex(lam_re.astype(F32), lam_im.astype(F32))
    ldt = lam * dt
    lam_bar = jnp.exp(ldt)
    b_bar = ((lam_bar - 1.0) / lam)[..., None] * lax.complex(b_re.astype(F32), b_im.astype(F32))
    c = lax.complex(c_re.astype(F32), c_im.astype(F32))
    tau = jnp.arange(S + 1, dtype=F32)
    pows = jnp.exp(ldt[None] * tau[:, None, None])
    KB, GL = S5_KBLOCKS, S5_GROUPS // S5_KBLOCKS
    eye = jnp.eye(GL, dtype=bool)
    w1 = pows[:S][::-1][:, :, None, :] * jnp.transpose(b_bar, (0, 2, 1))[None]
    w1 = jnp.transpose(w1.reshape(S, KB, GL, H, P), (1, 0, 2, 3, 4))
    w1 = jnp.where(eye[None, None, :, None, :, None], w1[:, :, :, :, None, :], 0.0)
    w1 = w1.reshape(KB, S * GL * H, GL * P)
    w2 = jnp.transpose(c, (0, 2, 1))[:, :, None, :] * jnp.transpose(pows[1:], (1, 2, 0))[..., None]
    w2 = w2.reshape(KB, GL, P, S, H)
    w2 = jnp.where(eye[None, :, None, None, :, None], w2[:, :, :, :, None, :], 0.0)
    w2 = w2.reshape(KB, GL * P, S * GL * H)
    kc = jnp.real(jnp.einsum('ghp,tgp,gpi->tghi', c, pows[:S], b_bar, precision=hp))
    kc = jnp.transpose(kc.reshape(S, KB, GL, H, H), (1, 2, 4, 0, 3))
    kc = jnp.where(eye[None, :, None, None, :, None], kc[:, :, :, :, None, :], 0.0)
    kcat = kc.reshape(KB, GL * H, S * GL * H)

    nstep = max(1, (n_rows - 1).bit_length())
    kk = (S * (2 ** jnp.arange(nstep))).astype(F32)
    lp = jnp.exp(ldt[None] * kk[:, None, None])
    lp = jnp.transpose(lp.reshape(nstep, KB, GL * P), (1, 0, 2))
    lampow = jnp.stack([jnp.real(lp), jnp.imag(lp)], axis=2)
    return dict(
        kcat=kcat.astype(BF16),
        w1re=jnp.real(w1).astype(BF16), w1im=jnp.imag(w1).astype(BF16),
        w2re=jnp.real(w2).astype(BF16), w2im=(-jnp.imag(w2)).astype(BF16),
        lampow=lampow.astype(F32))


def _s5_kernel(u_ref, w1re_ref, w1im_ref, w2re_ref, w2im_ref, kcat_ref, lp_ref, o_ref,
               ucat_ref, yall_ref, carry_ref, *, nstep):
    t = pl.program_id(1)
    S = S5_SUB
    R = ucat_ref.shape[0]
    W = u_ref.shape[1]

    @pl.when(t == 0)
    def _():
        carry_ref[...] = jnp.zeros_like(carry_ref)

    for j in range(S):
        ucat_ref[:, j * W:(j + 1) * W] = u_ref[pl.ds(j, R, stride=S), :].astype(BF16)
    ucat = ucat_ref[...]
    xre = jnp.dot(ucat, w1re_ref[0], preferred_element_type=F32)
    xim = jnp.dot(ucat, w1im_ref[0], preferred_element_type=F32)
    row = lax.broadcasted_iota(jnp.int32, xre.shape, 0)
    cre, cim = carry_ref[0], carry_ref[1]
    lr, li = lp_ref[0, 0, 0:1, :], lp_ref[0, 0, 1:2, :]
    xre = xre + jnp.where(row == 0, lr * cre - li * cim, 0.0)
    xim = xim + jnp.where(row == 0, lr * cim + li * cre, 0.0)
    for k in range(nstep):
        sh = 1 << k
        pre = pltpu.roll(xre, sh, 0)
        pim = pltpu.roll(xim, sh, 0)
        lr, li = lp_ref[0, k, 0:1, :], lp_ref[0, k, 1:2, :]
        keep = row >= sh
        xre, xim = (xre + jnp.where(keep, lr * pre - li * pim, 0.0),
                    xim + jnp.where(keep, lr * pim + li * pre, 0.0))
    carry_ref[0] = xre[R - 1:R, :]
    carry_ref[1] = xim[R - 1:R, :]
    sre = jnp.where(row >= 1, pltpu.roll(xre, 1, 0), cre).astype(BF16)
    sim = jnp.where(row >= 1, pltpu.roll(xim, 1, 0), cim).astype(BF16)
    yall_ref[...] = (jnp.dot(sre, w2re_ref[0], preferred_element_type=F32)
                     + jnp.dot(sim, w2im_ref[0], preferred_element_type=F32))
    for j in range(S):
        yall_ref[:, j * W:] += jnp.dot(ucat_ref[:, j * W:(j + 1) * W], kcat_ref[0, :, :(S - j) * W],
                                       preferred_element_type=F32)
    for j in range(S):
        o_ref[pl.ds(j, R, stride=S), :] = yall_ref[:, j * W:(j + 1) * W]


def _s5_scan(proj_a, tabs, rt):
    L = proj_a.shape[0]
    S, KB = S5_SUB, S5_KBLOCKS
    W = GROUP_W // KB
    nstep = tabs['lampow'].shape[1]
    P2 = tabs['lampow'].shape[3]
    rows = rt * S
    kb3 = lambda a: pl.BlockSpec((1,) + a.shape[1:], lambda k, t: (k, 0, 0))
    return pl.pallas_call(
        functools.partial(_s5_kernel, nstep=nstep),
        grid=(KB, L // rows),
        in_specs=[pl.BlockSpec((rows, W), lambda k, t: (t, k)),
                  kb3(tabs['w1re']), kb3(tabs['w1im']), kb3(tabs['w2re']), kb3(tabs['w2im']), kb3(tabs['kcat']),
                  pl.BlockSpec((1, nstep, 2, P2), lambda k, t: (k, 0, 0, 0))],
        out_specs=pl.BlockSpec((rows, W), lambda k, t: (t, k)),
        out_shape=jax.ShapeDtypeStruct((L, GROUP_W), F32),
        scratch_shapes=[pltpu.VMEM((rt, S * W), BF16), pltpu.VMEM((rt, S * W), F32), pltpu.VMEM((2, 1, P2), F32)],
        compiler_params=_cparams(("parallel", "arbitrary")),
        name="s5_scan",
    )(proj_a, tabs['w1re'], tabs['w1im'], tabs['w2re'], tabs['w2im'], tabs['kcat'], tabs['lampow'])


def _mixers_kernel(s5u_ref, cb_ref, cc_ref, ch_ref, su_ref, sv_ref, cch_ref, chh_ref, ys_ref,
                   d_ref, wglu_ref, cw_ref, lng_ref, lnb_ref, ws_ref, bs_ref, g_ref, o_ref):
    i = pl.program_id(0)
    tm = o_ref.shape[0]
    gw = GROUP_W
    y = ys_ref[...] + d_ref[...] * s5u_ref[...]
    y = jax.nn.gelu(y)
    y = y * jax.nn.sigmoid(jnp.dot(y.astype(BF16), wglu_ref[...], preferred_element_type=F32))
    o_ref[:, 0:gw] = _rms(y, g_ref[0:1, :]).astype(o_ref.dtype)
    z = cc_ref[...] * ch_ref[...]
    zh = jnp.where(i > 0, cch_ref[...] * chh_ref[...], 0.0)
    row = lax.broadcasted_iota(jnp.int32, z.shape, 0)
    z1 = jnp.where(row == 0, zh[7:8, :], pltpu.roll(z, 1, 0))
    z2 = jnp.where(row == 0, zh[6:7, :], jnp.where(row == 1, zh[7:8, :], pltpu.roll(z, 2, 0)))
    conv = cw_ref[0:1, :] * z2 + cw_ref[1:2, :] * z1 + cw_ref[2:3, :] * z
    o_ref[:, gw:2 * gw] = _rms(cb_ref[...] * conv, g_ref[1:2, :]).astype(o_ref.dtype)
    uu = jax.nn.gelu(su_ref[...])
    vv = _ln(jax.nn.gelu(sv_ref[...]), lng_ref[...], lnb_ref[...]).astype(BF16)
    pi = lax.broadcasted_iota(jnp.int32, (SGU_BLK, SGU_BLK), 0)
    pj = lax.broadcasted_iota(jnp.int32, (SGU_BLK, SGU_BLK), 1)
    causal = (pj // CHUNK) <= (pi // CHUNK)
    hd = gw // SGU_HEADS
    ws = [jnp.where(causal, ws_ref[h], 0.0).astype(BF16) for h in range(SGU_HEADS)]
    blocks = []
    for n in range(tm // SGU_BLK):
        vb = vv[n * SGU_BLK:(n + 1) * SGU_BLK, :]
        blocks.append(jnp.concatenate(
            [jnp.dot(ws[h], vb[:, h * hd:(h + 1) * hd], preferred_element_type=F32) for h in range(SGU_HEADS)],
            axis=1) + bs_ref[...])
    mixed = jnp.concatenate(blocks, axis=0)
    o_ref[:, 2 * gw:3 * gw] = _rms(uu * mixed, g_ref[2:3, :]).astype(o_ref.dtype)


def _mixers(proj_a, ys5, s5_d, w_glu, conv_w, ln_g, ln_b, sgu_w, sgu_b, mix_g, tm):
    L = proj_a.shape[0]
    gw = GROUP_W
    hb = tm // 8
    col = lambda c: pl.BlockSpec((tm, gw), lambda i, c=c: (i, c))
    halo = lambda c: pl.BlockSpec((8, gw), lambda i, c=c: (jnp.maximum(i * hb - 1, 0), c))
    full = lambda a: pl.BlockSpec(a.shape, lambda i: (0,) * a.ndim)
    hd = gw // SGU_HEADS
    bs_full = jnp.repeat(sgu_b.astype(F32).T, hd, axis=1)
    consts = [s5_d.reshape(1, gw).astype(F32), w_glu.astype(BF16), conv_w.astype(F32),
              ln_g.reshape(1, gw).astype(F32), ln_b.reshape(1, gw).astype(F32), sgu_w.astype(F32),
              bs_full, mix_g.reshape(3, gw).astype(F32)]
    return pl.pallas_call(
        _mixers_kernel,
        grid=(L // tm,),
        in_specs=[col(0), col(1), col(2), col(3), col(4), col(5), halo(2), halo(3),
                  pl.BlockSpec((tm, gw), lambda i: (i, 0))] + [full(a) for a in consts],
        out_specs=pl.BlockSpec((tm, 3 * gw), lambda i: (i, 0)),
        out_shape=jax.ShapeDtypeStruct((L, 3 * gw), BF16),
        compiler_params=_cparams(("parallel",)),
        name="row_mixers",
    )(proj_a, proj_a, proj_a, proj_a, proj_a, proj_a, proj_a, proj_a, ys5, *consts)


def _t5_bucket(rel):
    half = NUM_BUCKETS // 2
    ret = jnp.where(rel > 0, half, 0)
    n = jnp.abs(rel)
    max_exact = half // 2
    large = max_exact + (jnp.log(jnp.maximum(n, 1).astype(F32) / max_exact)
                         / math.log(MAX_DISTANCE / max_exact) * (half - max_exact)).astype(jnp.int32)
    large = jnp.minimum(large, half - 1)
    return ret + jnp.where(n < max_exact, n, large)


def _attn_bias_tables(rel_bias, tq):
    assert tq >= MAX_DISTANCE
    rb = rel_bias.astype(F32)
    far = rb[NUM_BUCKETS // 2 - 1]
    buckets = jnp.arange(NUM_BUCKETS)[:, None]

    def bias_of(rel):
        onehot = _t5_bucket(rel)[:, :, None, None] == buckets
        return jnp.sum(jnp.where(onehot, rb, 0.0), axis=2) - far

    kj = jnp.arange(tq)[:, None]
    qi = jnp.arange(tq)[None, :]
    diag = jnp.where(((kj // CHUNK) <= (qi // CHUNK))[..., None], bias_of(kj - qi), NEG_INF)
    prev = bias_of(kj - tq - qi)
    tabs = jnp.stack([jnp.transpose(diag, (2, 0, 1)), jnp.transpose(prev, (2, 0, 1))], axis=1)
    tabs = jnp.where(tabs > 0.5 * NEG_INF, tabs * LOG2E, NEG_INF)
    return jnp.concatenate([tabs, tabs], axis=3)


def _attn_kernel(qt_ref, k_ref, vt_ref, nb_ref, lq1_ref, lk1_ref, lq2_ref, lk2_ref, g_ref, o_ref, *scratch,
                 lambda_init):
    i = pl.program_id(1)
    tq = qt_ref.shape[2]
    dq, dv = DIFF_QK_DIM, DIFF_V_DIM
    heads = range(ATTN_GROUP)
    per = len(scratch) // ATTN_GROUP
    qq_s, sa_s, sb_s, p_s, m_s, a_s, acc_s = [[scratch[g * per + n] for g in heads] for n in range(per)]
    feat = lax.broadcasted_iota(jnp.int32, (dv, tq), 0)
    for g in heads:
        qt = qt_ref[0, g * dv:(g + 1) * dv, :]
        zero = jnp.zeros_like(qt)
        qq_s[g][...] = jnp.concatenate([jnp.where(feat < dq, qt, zero), jnp.where(feat >= dq, qt, zero)], axis=1)
        m_s[g][...] = jnp.full(m_s[g].shape, -jnp.inf, F32)
        acc_s[g][...] = jnp.zeros(acc_s[g].shape, F32)

    def scores(bufs, j, nblk):
        rows = nblk * tq
        for g in heads:
            kb = k_ref[pl.ds(pl.multiple_of(j * tq, tq), rows), g * dv:(g + 1) * dv]
            bufs[g][:rows, :] = jnp.dot(kb, qq_s[g][...], preferred_element_type=F32)

    def absorb(bufs, j, nblk, bias=None):
        rows = nblk * tq
        for g in heads:
            for c in range(2 * tq // 128):
                cs = slice(c * 128, (c + 1) * 128)
                s = bufs[g][:rows, cs]
                if bias is not None:
                    s = s + bias(g, cs)
                m_old = m_s[g][:, cs]
                m_new = jnp.maximum(m_old, jnp.max(s, axis=0, keepdims=True))
                m_s[g][:, cs] = m_new
                a_s[g][:, cs] = jnp.exp2(m_old - m_new)
                p_s[g][:rows, cs] = jnp.exp2((s - m_new).astype(BF16))
            vt = jnp.concatenate([vt_ref[j + b, g * dv:(g + 1) * dv, :] for b in range(nblk)], axis=1)
            vt = jnp.concatenate([vt, jnp.ones((ONES_ROWS, rows), BF16)], axis=0)
            acc_s[g][...] = (a_s[g][...] * acc_s[g][...]
                             + jnp.dot(vt, p_s[g][:rows, :], preferred_element_type=F32))

    scores(sa_s, i, 1)
    absorb(sa_s, i, 1, lambda g, cs: nb_ref[g, 0, :, cs])
    jp = jnp.maximum(i - 1, 0)
    first = jnp.where(i > 0, 0.0, NEG_INF)
    scores(sa_s, jp, 1)
    absorb(sa_s, jp, 1, lambda g, cs: nb_ref[g, 1, :, cs] + first)
    n_far = jnp.maximum(i - 1, 0)
    n_single = n_far % 2
    n_head = n_far % 4

    @pl.when(n_single == 1)
    def _():
        scores(sb_s, 0, 1)
        absorb(sb_s, 0, 1)

    @pl.when(n_head >= 2)
    def _():
        scores(sb_s, n_single, 2)
        absorb(sb_s, n_single, 2)

    scores(sa_s, n_head, 2)

    def quad(qd, c):
        j0 = n_head + 4 * qd
        scores(sb_s, j0 + 2, 2)
        absorb(sa_s, j0, 2)
        scores(sa_s, j0 + 4, 2)
        absorb(sb_s, j0 + 2, 2)
        return c

    lax.fori_loop(0, n_far // 4, quad, 0)
    lam = (jnp.exp(jnp.sum(lq1_ref[...] * lk1_ref[...], keepdims=True))
           - jnp.exp(jnp.sum(lq2_ref[...] * lk2_ref[...], keepdims=True)) + lambda_init)
    for g in heads:
        o = acc_s[g][:dv, :] / acc_s[g][dv:dv + 1, :]
        out = o[:, :tq] - lam * o[:, tq:]
        out = out * lax.rsqrt(jnp.mean(jnp.square(out), axis=0, keepdims=True) + EPS) * g_ref[...]
        o_ref[:, g * dv:(g + 1) * dv] = (out * (1.0 - lambda_init)).T.astype(o_ref.dtype)


def _diff_attention(qvt, kmat, nb, lq1, lk1, lq2, lk2, subln_g, lambda_init):
    nq, _, tq = qvt.shape
    assert nq >= 2
    L = kmat.shape[0]
    H, dv, G = DIFF_HEADS, DIFF_V_DIM, ATTN_GROUP
    vec = lambda a: a.reshape(1, -1).astype(F32)
    small = lambda n: pl.BlockSpec((1, n), lambda h, i: (0, 0))
    per_head = [pltpu.VMEM((dv, 2 * tq), BF16),
                pltpu.VMEM((2 * tq, 2 * tq), F32), pltpu.VMEM((2 * tq, 2 * tq), F32),
                pltpu.VMEM((2 * tq, 2 * tq), BF16),
                pltpu.VMEM((1, 2 * tq), F32), pltpu.VMEM((1, 2 * tq), F32),
                pltpu.VMEM((dv + ONES_ROWS, 2 * tq), F32)]
    return pl.pallas_call(
        functools.partial(_attn_kernel, lambda_init=lambda_init),
        grid=(H // G, nq),
        in_specs=[pl.BlockSpec((1, G * dv, tq), lambda h, i: (i, h, 0)),
                  pl.BlockSpec((L, G * dv), lambda h, i: (0, h)),
                  pl.BlockSpec((nq, G * dv, tq), lambda h, i: (0, H // G + h, 0)),
                  pl.BlockSpec((G, 2, tq, 2 * tq), lambda h, i: (h, 0, 0, 0)),
                  small(DIFF_QK_DIM), small(DIFF_QK_DIM), small(DIFF_QK_DIM), small(DIFF_QK_DIM),
                  pl.BlockSpec((dv, 1), lambda h, i: (0, 0))],
        out_specs=pl.BlockSpec((tq, G * dv), lambda h, i: (i, h)),
        out_shape=jax.ShapeDtypeStruct((L, H * dv), BF16),
        scratch_shapes=per_head * G,
        compiler_params=_cparams(("parallel", "arbitrary")),
        name="diff_attention",
    )(qvt, kmat, qvt, nb, vec(lq1), vec(lk1), vec(lq2), vec(lk2), subln_g.reshape(dv, 1).astype(F32))


def _outproj_kernel(abc_ref, d_ref, x_ref, wa_ref, wd_ref, g_ref, b_ref, rhi_ref, rlo_ref,
                    x1_ref, x1b_ref, x1p_ref, lg_ref, *, alpha):
    nt = (((1,), (1,)), ((), ()))
    tm = x_ref.shape[0]
    sub = min(tm, SUB_ROWS)
    for r0 in range(0, tm, sub):
        rows = pl.ds(r0, sub)
        mix = (jnp.dot(abc_ref[rows, :], wa_ref[...], preferred_element_type=F32)
               + jnp.dot(d_ref[rows, :], wd_ref[...], preferred_element_type=F32))
        x1 = _ln(alpha * x_ref[rows, :] + mix, g_ref[...], b_ref[...])
        x1_ref[rows, :] = x1
        hi = x1.astype(BF16)
        x1b_ref[rows, :] = hi
        _store_token_tiles(x1p_ref.at[pl.ds(r0 * ROW_TILE, sub * ROW_TILE), :], x1)
        lo = (x1 - hi.astype(F32)).astype(BF16)
        lg_ref[:, r0:r0 + sub] = (lax.dot_general(rhi_ref[...], hi, nt, preferred_element_type=F32)
                                  + lax.dot_general(rhi_ref[...], lo, nt, preferred_element_type=F32)
                                  + lax.dot_general(rlo_ref[...], hi, nt, preferred_element_type=F32))


def _outproj(abc, d_out, x, w_out, ln_g, ln_b, router_w, alpha, tm):
    L, D = x.shape
    E = router_w.shape[1]
    ka = abc.shape[1]
    kd = d_out.shape[1]
    rwt = router_w.astype(F32).T
    rhi = rwt.astype(BF16)
    rlo = (rwt - rhi.astype(F32)).astype(BF16)
    full = lambda shape: pl.BlockSpec(shape, lambda i: (0,) * len(shape))
    return pl.pallas_call(
        functools.partial(_outproj_kernel, alpha=alpha),
        grid=(L // tm,),
        in_specs=[pl.BlockSpec((tm, ka), lambda i: (i, 0)),
                  pl.BlockSpec((tm, kd), lambda i: (i, 0)),
                  pl.BlockSpec((tm, D), lambda i: (i, 0)),
                  pl.BlockSpec((ka, D), lambda i: (0, 0)),
                  pl.BlockSpec((kd, D), lambda i: (ka // kd, 0)),
                  full((1, D)), full((1, D)), full((E, D)), full((E, D))],
        out_specs=[pl.BlockSpec((tm, D), lambda i: (i, 0)),
                   pl.BlockSpec((tm, D), lambda i: (i, 0)),
                   pl.BlockSpec((tm * ROW_TILE, LANES), lambda i: (i, 0)),
                   pl.BlockSpec((E, tm), lambda i: (0, i))],
        out_shape=[jax.ShapeDtypeStruct((L, D), F32), jax.ShapeDtypeStruct((L, D), BF16),
                   jax.ShapeDtypeStruct((L * ROW_TILE, LANES), jnp.uint32), jax.ShapeDtypeStruct((E, L), F32)],
        compiler_params=_cparams(("parallel",)),
        name="outproj_ln1",
    )(abc, d_out, x, w_out, w_out, ln_g.reshape(1, D).astype(F32), ln_b.reshape(1, D).astype(F32), rhi, rlo)


def _router_kernel(lg_ref, bias_ref, tri_ref, e_ref, r_ref, w_ref, cnt_ref, carry_ref):
    i = pl.program_id(0)
    E, tn = lg_ref.shape
    ng = N_EXPERT_GROUPS
    gs_ = E // ng

    @pl.when(i == 0)
    def _():
        carry_ref[...] = jnp.zeros_like(carry_ref)

    s = jax.nn.sigmoid(lg_ref[...])
    sel = s + bias_ref[...]
    midx = lax.broadcasted_iota(jnp.int32, (gs_, tn), 0).astype(F32)
    rows, gscore = [], []
    for g in range(ng):
        rg = sel[g * gs_:(g + 1) * gs_, :]
        m1 = jnp.max(rg, axis=0, keepdims=True)
        first = jnp.min(jnp.where(rg == m1, midx, float(gs_)), axis=0, keepdims=True)
        m2 = jnp.max(jnp.where(midx == first, -jnp.inf, rg), axis=0, keepdims=True)
        rows.append(rg)
        gscore.append(m1 + m2)
    vals = []
    for g in range(ng):
        rank = jnp.zeros((1, tn), F32)
        for o in range(ng):
            if o != g:
                beats = (gscore[o] >= gscore[g]) if o < g else (gscore[o] > gscore[g])
                rank = rank + jnp.where(beats, 1.0, 0.0)
        vals.append(jnp.where(rank < TOPK_GROUPS, rows[g], -jnp.inf))
    val = jnp.concatenate(vals, axis=0)
    eidx = lax.broadcasted_iota(jnp.int32, val.shape, 0)
    erank = jnp.zeros(val.shape, F32)
    for e in range(E):
        other = val[e:e + 1, :]
        erank = erank + jnp.where(eidx > e, jnp.where(other >= val, 1.0, 0.0), jnp.where(other > val, 1.0, 0.0))
    chosen = erank < TOP_K
    wsel = jnp.where(chosen, s, 0.0)
    wn = wsel / (jnp.sum(wsel, axis=0, keepdims=True) + 1e-20) * ROUTED_SCALE
    chf = jnp.where(chosen, 1.0, 0.0)
    incl = jnp.dot(chf.astype(BF16), tri_ref[...], preferred_element_type=F32)
    base = carry_ref[...]
    pos = base + incl - chf
    carry_ref[...] = base + incl[:, tn - 1:tn]
    cnt_ref[...] = (base + incl[:, tn - 1:tn]).astype(jnp.int32)
    eidf = eidx.astype(F32)
    cand = jnp.where(chosen, eidf, float(E))
    for k in range(TOP_K):
        ek = jnp.min(cand, axis=0, keepdims=True)
        hit = cand == ek
        e_ref[k:k + 1, :] = ek.astype(jnp.int32)
        r_ref[k:k + 1, :] = jnp.sum(jnp.where(hit, pos, 0.0), axis=0, keepdims=True).astype(jnp.int32)
        w_ref[k:k + 1, :] = jnp.sum(jnp.where(hit, wn, 0.0), axis=0, keepdims=True)
        cand = jnp.where(hit, float(E), cand)


def _router(logits_t, router_bias, tn):
    E, L = logits_t.shape
    tri = (jnp.arange(tn)[:, None] <= jnp.arange(tn)[None, :]).astype(BF16)
    slot = lambda dt: jax.ShapeDtypeStruct((TOP_K, L), dt)
    return pl.pallas_call(
        _router_kernel,
        grid=(L // tn,),
        in_specs=[pl.BlockSpec((E, tn), lambda i: (0, i)),
                  pl.BlockSpec((E, 1), lambda i: (0, 0)),
                  pl.BlockSpec((tn, tn), lambda i: (0, 0))],
        out_specs=[pl.BlockSpec((TOP_K, tn), lambda i: (0, i)),
                   pl.BlockSpec((TOP_K, tn), lambda i: (0, i)),
                   pl.BlockSpec((TOP_K, tn), lambda i: (0, i)),
                   pl.BlockSpec((E, 1), lambda i: (0, 0))],
        out_shape=[slot(jnp.int32), slot(jnp.int32), slot(F32), jax.ShapeDtypeStruct((E, 1), jnp.int32)],
        scratch_shapes=[pltpu.VMEM((E, 1), F32)],
        compiler_params=_cparams(("arbitrary",)),
        name="router_topk",
    )(logits_t, router_bias.reshape(E, 1).astype(F32), tri)


def _dispatch_kernel(dest_ref, pad_ref, x_ref, xs_ref, zero_ref, sem, zsem):
    i = pl.program_id(0)
    tm = x_ref.shape[0] // ROW_TILE

    def tile(ref, t):
        return ref.at[pl.ds(pl.multiple_of(t * ROW_TILE, ROW_TILE), ROW_TILE), :]

    def row_copy(r, k):
        return pltpu.make_async_copy(tile(x_ref, r), tile(xs_ref, dest_ref[k, r]), sem)

    def zero_copy(dst):
        return pltpu.make_async_copy(zero_ref, tile(xs_ref, dst), zsem)

    @pl.when(i == 0)
    def _():
        zero_ref[...] = jnp.zeros_like(zero_ref)

        def per_expert(e, c):
            lo, hi = pad_ref[0, e], pad_ref[1, e]
            lax.fori_loop(lo, hi, lambda d, c2: (zero_copy(d).start(), c2)[1], 0)
            lax.fori_loop(lo, hi, lambda d, c2: (zero_copy(d).wait(), c2)[1], 0)
            return c
        lax.fori_loop(0, N_EXPERTS, per_expert, 0)

    def issue(r, c):
        for k in range(TOP_K):
            row_copy(r, k).start()
        return c
    lax.fori_loop(0, tm, issue, 0)

    def drain(r, c):
        for k in range(TOP_K):
            row_copy(r, k).wait()
        return c
    lax.fori_loop(0, tm, drain, 0)


def _dispatch(x1p, dest, pad_lo_hi, n_slots, tm):
    L = x1p.shape[0] // ROW_TILE
    return pl.pallas_call(
        _dispatch_kernel,
        grid=(L // tm,),
        in_specs=[pl.BlockSpec((TOP_K, tm), lambda i: (0, i), memory_space=pltpu.SMEM),
                  pl.BlockSpec(memory_space=pltpu.SMEM),
                  pl.BlockSpec((tm * ROW_TILE, LANES), lambda i: (i, 0))],
        out_specs=pl.BlockSpec(memory_space=pl.ANY),
        out_shape=jax.ShapeDtypeStruct((n_slots * ROW_TILE, LANES), x1p.dtype),
        scratch_shapes=[pltpu.VMEM((ROW_TILE, LANES), x1p.dtype), pltpu.SemaphoreType.DMA(()),
                        pltpu.SemaphoreType.DMA(())],
        compiler_params=_cparams(("arbitrary",)),
        name="moe_dispatch",
    )(dest, pad_lo_hi, x1p)


def _experts_kernel(be_ref, nb_ref, xs_ref, wgu_ref, wd_ref, y_ref, wgu_b, wd_b):
    b = pl.program_id(0)

    @pl.when((b == 0) | (be_ref[b] != be_ref[jnp.maximum(b - 1, 0)]))
    def _():
        wgu_b[...] = wgu_ref[0, 0].astype(BF16)
        wd_b[...] = wd_ref[0, 0].astype(BF16)

    @pl.when(b < nb_ref[0])
    def _():
        de = wd_b.shape[0]
        xa, xb = _load_token_tiles(xs_ref, xs_ref.shape[0] // ROW_TILE)
        x = jnp.concatenate([xa.astype(BF16), xb.astype(BF16)], axis=1)
        h = jnp.dot(x, wgu_b[...], preferred_element_type=F32)
        a = jax.nn.silu(h[:, :de]) * h[:, de:]
        _store_token_tiles(y_ref, jnp.dot(a.astype(BF16), wd_b[...], preferred_element_type=F32))


def _experts(xs, block_e, n_used, w_gu, w_down, layer, blk):
    n_slots = xs.shape[0] // ROW_TILE
    nblk = n_slots // blk
    rows = blk * ROW_TILE
    D = w_gu.shape[2]
    de2 = w_gu.shape[3]
    de = w_down.shape[2]
    last = lambda b, be, nb: jnp.minimum(b, nb[0] - 1)
    return pl.pallas_call(
        _experts_kernel,
        grid_spec=pltpu.PrefetchScalarGridSpec(
            num_scalar_prefetch=2,
            grid=(nblk,),
            in_specs=[pl.BlockSpec((rows, LANES), lambda b, be, nb: (last(b, be, nb), 0)),
                      pl.BlockSpec((1, 1, D, de2), lambda b, be, nb: (layer, be[b], 0, 0)),
                      pl.BlockSpec((1, 1, de, D), lambda b, be, nb: (layer, be[b], 0, 0))],
            out_specs=pl.BlockSpec((rows, LANES), lambda b, be, nb: (last(b, be, nb), 0)),
            scratch_shapes=[pltpu.VMEM((D, de2), BF16), pltpu.VMEM((de, D), BF16)]),
        out_shape=jax.ShapeDtypeStruct(xs.shape, jnp.uint32),
        compiler_params=_cparams(("arbitrary",)),
        name="moe_experts",
    )(block_e, n_used, xs, w_gu, w_down)


def _combine_kernel(dcur_ref, dnxt_ref, y_ref, w_ref, x1_ref, x1b_ref, sgu_ref, sdn_ref, g_ref, b_ref,
                    x2_ref, x2b_ref, buf, sem, *, alpha):
    i = pl.program_id(0)
    n = pl.num_programs(0)
    tm = x1_ref.shape[0]
    slot = i % 2

    def row_copy(d_ref, s, r, k):
        src = y_ref.at[pl.ds(pl.multiple_of(d_ref[k, r] * ROW_TILE, ROW_TILE), ROW_TILE), :]
        dst = buf.at[s, k, pl.ds(pl.multiple_of(r * ROW_TILE, ROW_TILE), ROW_TILE), :]
        return pltpu.make_async_copy(src, dst, sem.at[s])

    def issue(d_ref, s):
        def body(r, c):
            for k in range(TOP_K):
                row_copy(d_ref, s, r, k).start()
            return c
        lax.fori_loop(0, tm, body, 0)

    @pl.when(i == 0)
    def _():
        issue(dcur_ref, 0)

    @pl.when(i + 1 < n)
    def _():
        issue(dnxt_ref, 1 - slot)

    de = sdn_ref.shape[0]
    h = jnp.dot(x1b_ref[...], sgu_ref[...], preferred_element_type=F32)
    a = jax.nn.silu(h[:, :de]) * h[:, de:]
    ffn = jnp.dot(a.astype(BF16), sdn_ref[...], preferred_element_type=F32)

    def drain(r, c):
        for k in range(TOP_K):
            row_copy(dcur_ref, slot, r, k).wait()
        return c
    lax.fori_loop(0, tm, drain, 0)

    half = ffn.shape[1] // 2
    fa, fb = ffn[:, :half], ffn[:, half:]
    for k in range(TOP_K):
        ya, yb = _load_token_tiles(buf.at[slot, k], tm)
        w = w_ref[:, k:k + 1]
        fa = fa + ya * w
        fb = fb + yb * w
    ffn = jnp.concatenate([fa, fb], axis=1)
    x2 = _ln(alpha * x1_ref[...] + ffn, g_ref[...], b_ref[...])
    x2_ref[...] = x2
    x2b_ref[...] = x2.astype(BF16)


def _combine(dest, y, w_t, x1, x1b, sh_gu, sh_down, ln_g, ln_b, alpha, tm):
    L, D = x1.shape
    n = L // tm
    full = lambda shape: pl.BlockSpec(shape, lambda i: (0,) * len(shape))
    return pl.pallas_call(
        functools.partial(_combine_kernel, alpha=alpha),
        grid=(n,),
        in_specs=[pl.BlockSpec((TOP_K, tm), lambda i: (0, i), memory_space=pltpu.SMEM),
                  pl.BlockSpec((TOP_K, tm), lambda i: (0, jnp.minimum(i + 1, n - 1)), memory_space=pltpu.SMEM),
                  pl.BlockSpec(memory_space=pl.ANY),
                  pl.BlockSpec((tm, TOP_K), lambda i: (i, 0)),
                  pl.BlockSpec((tm, D), lambda i: (i, 0)),
                  pl.BlockSpec((tm, D), lambda i: (i, 0)),
                  full(sh_gu.shape), full(sh_down.shape), full((1, D)), full((1, D))],
        out_specs=[pl.BlockSpec((tm, D), lambda i: (i, 0)), pl.BlockSpec((tm, D), lambda i: (i, 0))],
        out_shape=[jax.ShapeDtypeStruct((L, D), F32), jax.ShapeDtypeStruct((L, D), BF16)],
        scratch_shapes=[pltpu.VMEM((2, TOP_K, tm * ROW_TILE, LANES), y.dtype), pltpu.SemaphoreType.DMA((2,))],
        compiler_params=_cparams(("arbitrary",)),
        name="moe_combine_ln2",
    )(dest, dest, y, w_t, x1, x1b, sh_gu, sh_down, ln_g.reshape(1, D).astype(F32), ln_b.reshape(1, D).astype(F32))


def _experts_fused_kernel(blk_ref, exp_ref, nv_ref,
                          src_ref, srcn_ref, dst_ref, x_hbm, wgu_ref, wd_ref, y_hbm,
                          xbuf, ybuf, wgu_b, wd_b, gsem, ssem):
    w = pl.program_id(0)
    nv = nv_ref[0]
    R = xbuf.shape[1] // ROW_TILE
    par = w % 2

    def tile(buf, r):
        return buf.at[pl.ds(r * ROW_TILE, ROW_TILE), :]

    def gather(s_ref, p):
        for r in range(R):
            pltpu.make_async_copy(x_hbm.at[s_ref[0, 0, r]], tile(xbuf.at[p], r), gsem.at[p]).start()

    def gather_wait(p):
        pltpu.make_async_copy(xbuf.at[1 - p], xbuf.at[p], gsem.at[p]).wait()

    def scatter_wait(p):
        pltpu.make_async_copy(ybuf.at[p], ybuf.at[1 - p], ssem.at[p]).wait()

    @pl.when(w == 0)
    def _():
        gather(src_ref, 0)

    @pl.when((w < nv) & ((w == 0) | (exp_ref[w] != exp_ref[jnp.maximum(w - 1, 0)])))
    def _():
        wgu_b[...] = wgu_ref[0, 0].astype(BF16)
        wd_b[...] = wd_ref[0, 0].astype(BF16)

    @pl.when((w >= 2) & (w < nv))
    def _():
        scatter_wait(par)

    @pl.when(w < nv)
    def _():
        gather_wait(par)
        gather(srcn_ref, 1 - par)
        de = wd_b.shape[0]
        xa, xb = _load_token_tiles(xbuf.at[par], R)
        x = jnp.concatenate([xa.astype(BF16), xb.astype(BF16)], axis=1)
        h = jnp.dot(x, wgu_b[...], preferred_element_type=F32)
        a = jax.nn.silu(h[:, :de]) * h[:, de:]
        _store_token_tiles(ybuf.at[par], jnp.dot(a.astype(BF16), wd_b[...], preferred_element_type=F32))
        for r in range(R):
            pltpu.make_async_copy(tile(ybuf.at[par], r), y_hbm.at[dst_ref[0, 0, r]], ssem.at[par]).start()

    @pl.when(w == nv - 1)
    def _():
        scatter_wait(par)
        gather_wait(1 - par)

    @pl.when((w == nv - 1) & (w >= 1))
    def _():
        scatter_wait(1 - par)


def _experts_fused(x1p, src, dst, items, w_gu, w_down, layer, n_out_tiles, blk):
    blk_w, exp_w, n_valid = items
    n_items = blk_w.shape[0]
    D, de2 = w_gu.shape[2], w_gu.shape[3]
    de = w_down.shape[2]
    cur = lambda w, nv: jnp.minimum(w, nv[0] - 1)
    return pl.pallas_call(
        _experts_fused_kernel,
        grid_spec=pltpu.PrefetchScalarGridSpec(
            num_scalar_prefetch=3,
            grid=(n_items,),
            in_specs=[pl.BlockSpec((1, 1, blk), lambda w, b, e, nv: (b[cur(w, nv)], 0, 0),
                                   memory_space=pltpu.SMEM),
                      pl.BlockSpec((1, 1, blk), lambda w, b, e, nv: (b[cur(w + 1, nv)], 0, 0),
                                   memory_space=pltpu.SMEM),
                      pl.BlockSpec((1, 1, blk), lambda w, b, e, nv: (cur(w, nv), 0, 0),
                                   memory_space=pltpu.SMEM),
                      pl.BlockSpec(memory_space=pl.ANY),
                      pl.BlockSpec((1, 1, D, de2), lambda w, b, e, nv: (layer, e[cur(w, nv)], 0, 0)),
                      pl.BlockSpec((1, 1, de, D), lambda w, b, e, nv: (layer, e[cur(w, nv)], 0, 0))],
            out_specs=pl.BlockSpec(memory_space=pl.ANY),
            scratch_shapes=[pltpu.VMEM((2, blk * ROW_TILE, LANES), jnp.uint32),
                            pltpu.VMEM((2, blk * ROW_TILE, LANES), jnp.uint32),
                            pltpu.VMEM((D, de2), BF16), pltpu.VMEM((de, D), BF16),
                            pltpu.SemaphoreType.DMA((2,)), pltpu.SemaphoreType.DMA((2,))]),
        out_shape=jax.ShapeDtypeStruct((n_out_tiles, ROW_TILE, LANES), jnp.uint32),
        compiler_params=_cparams(("arbitrary",)),
        name="moe_experts_fused",
    )(blk_w, exp_w, n_valid, src, src, dst, x1p.reshape(-1, ROW_TILE, LANES), w_gu, w_down).reshape(-1, LANES)


def _combine_stream_kernel(*refs, alpha):
    y_refs = refs[:TOP_K]
    w_ref, x1_ref, x1b_ref, sgu_ref, sdn_ref, g_ref, b_ref, x2_ref, x2b_ref = refs[TOP_K:]
    tm = x1_ref.shape[0]
    de = sdn_ref.shape[0]
    h = jnp.dot(x1b_ref[...], sgu_ref[...], preferred_element_type=F32)
    a = jax.nn.silu(h[:, :de]) * h[:, de:]
    ffn = jnp.dot(a.astype(BF16), sdn_ref[...], preferred_element_type=F32)
    half = ffn.shape[1] // 2
    fa, fb = ffn[:, :half], ffn[:, half:]
    for k in range(TOP_K):
        ya, yb = _load_token_tiles(y_refs[k], tm)
        w = w_ref[:, k:k + 1]
        fa = fa + ya * w
        fb = fb + yb * w
    ffn = jnp.concatenate([fa, fb], axis=1)
    x2 = _ln(alpha * x1_ref[...] + ffn, g_ref[...], b_ref[...])
    x2_ref[...] = x2
    x2b_ref[...] = x2.astype(BF16)


def _combine_stream(yk, w_t, x1, x1b, sh_gu, sh_down, ln_g, ln_b, alpha, tm):
    L, D = x1.shape
    n = L // tm
    full = lambda shape: pl.BlockSpec(shape, lambda i: (0,) * len(shape))
    y_specs = [pl.BlockSpec((tm * ROW_TILE, LANES), lambda i, k=k: (k * n + i, 0)) for k in range(TOP_K)]
    return pl.pallas_call(
        functools.partial(_combine_stream_kernel, alpha=alpha),
        grid=(n,),
        in_specs=y_specs + [pl.BlockSpec((tm, TOP_K), lambda i: (i, 0)),
                            pl.BlockSpec((tm, D), lambda i: (i, 0)),
                            pl.BlockSpec((tm, D), lambda i: (i, 0)),
                            full(sh_gu.shape), full(sh_down.shape), full((1, D)), full((1, D))],
        out_specs=[pl.BlockSpec((tm, D), lambda i: (i, 0)), pl.BlockSpec((tm, D), lambda i: (i, 0))],
        out_shape=[jax.ShapeDtypeStruct((L, D), F32), jax.ShapeDtypeStruct((L, D), BF16)],
        compiler_params=_cparams(("arbitrary",)),
        name="moe_combine_ln2",
    )(*([yk] * TOP_K), w_t, x1, x1b, sh_gu, sh_down, ln_g.reshape(1, D).astype(F32), ln_b.reshape(1, D).astype(F32))


def _moe_layer_fused(x1, x1b, x1p, logits_t, router_bias, w_gu, w_down, layer, sh_gu, sh_down, ln_g, ln_b, alpha,
                     router_tn, combine_tm):
    L, D = x1.shape
    E, K, R = N_EXPERTS, TOP_K, MOE_BLK
    A = K * L
    assert A % R == 0
    nblk = A // R
    e_k, _, w_k, counts = _router(logits_t, router_bias, router_tn)
    keys = (e_k * L + jnp.arange(L, dtype=jnp.int32)[None, :]) * K + jnp.arange(K, dtype=jnp.int32)[:, None]
    skeys = jnp.sort(keys.reshape(A))
    tok = (skeys // K) % L
    src = tok.reshape(nblk, 1, R)
    dst_sorted = ((skeys % K) * L + tok).reshape(nblk, R)
    ends = jnp.cumsum(counts.reshape(E))
    cuts = jnp.sort(jnp.concatenate([jnp.arange(nblk, dtype=jnp.int32) * R, (ends - counts.reshape(E))]))
    lo = cuts
    hi = jnp.concatenate([cuts[1:], jnp.full((1,), A, jnp.int32)])
    valid = hi > lo
    order = jnp.argsort(jnp.logical_not(valid), stable=True)
    lo, hi = lo[order].astype(jnp.int32), hi[order].astype(jnp.int32)
    n_valid = jnp.sum(valid).astype(jnp.int32).reshape(1)
    blk_w = jnp.minimum(lo // R, nblk - 1).astype(jnp.int32)
    exp_w = jnp.minimum(jnp.sum((ends[None, :] <= lo[:, None]).astype(jnp.int32), axis=1), E - 1).astype(jnp.int32)
    n_out_tiles = A + 2 * R
    n_items = blk_w.shape[0]
    onehot = (blk_w[:, None] == jnp.arange(nblk, dtype=jnp.int32)[None, :]).astype(F32)
    dst_rows = jnp.dot(onehot, dst_sorted.astype(F32), precision=lax.Precision.HIGHEST).astype(jnp.int32)
    pos = blk_w[:, None] * R + jnp.arange(R, dtype=jnp.int32)[None, :]
    spare = (A + (jnp.arange(n_items, dtype=jnp.int32)[:, None] % 2) * R
             + jnp.arange(R, dtype=jnp.int32)[None, :])
    dst = jnp.where((pos >= lo[:, None]) & (pos < hi[:, None]), dst_rows, spare).reshape(n_items, 1, R)
    yk = _experts_fused(x1p, src, dst, (blk_w, exp_w, n_valid), w_gu, w_down, layer, n_out_tiles, R)
    return _combine_stream(yk, w_k.T, x1, x1b, sh_gu, sh_down, ln_g, ln_b, alpha, combine_tm)


MOE_BLK = 256


def _moe_layer(x1, x1b, x1p, logits_t, router_bias, w_gu, w_down, layer, sh_gu, sh_down, ln_g, ln_b, alpha,
               router_tn, dispatch_tm, combine_tm):
    L, D = x1.shape
    E = N_EXPERTS
    e_k, r_k, w_k, counts = _router(logits_t, router_bias, router_tn)
    counts = counts.reshape(E)
    padded = (counts + MOE_BLK - 1) // MOE_BLK * MOE_BLK
    pends = jnp.cumsum(padded)
    pstarts = pends - padded
    dest = jnp.sum(jnp.where(e_k[..., None] == jnp.arange(E), pstarts, 0), axis=-1) + r_k
    nblk = -(-(L * TOP_K) // MOE_BLK) + E
    n_used = (pends[-1] // MOE_BLK).astype(jnp.int32)
    blocks = jnp.minimum(jnp.arange(nblk, dtype=jnp.int32), n_used - 1)
    block_e = jnp.sum((pends[None, :] <= (blocks * MOE_BLK)[:, None]).astype(jnp.int32), axis=1)
    block_e = jnp.minimum(block_e, E - 1).astype(jnp.int32)
    pad_lo_hi = jnp.stack([pstarts + counts, pends]).astype(jnp.int32)
    xs = _dispatch(x1p, dest, pad_lo_hi, nblk * MOE_BLK, dispatch_tm)
    y = _experts(xs, block_e, n_used.reshape(1), w_gu, w_down, layer, MOE_BLK)
    return _combine(dest, y, w_k.T, x1, x1b, sh_gu, sh_down, ln_g, ln_b, alpha, combine_tm)


def _pick(n, pref):
    t = min(n, pref)
    assert n % t == 0
    return t


def kernel(x, w_in, w_out, mix_norm_g, s5_lambda_re, s5_lambda_im, s5_log_dt, s5_b_re, s5_b_im, s5_c_re, s5_c_im, s5_d, s5_w_glu, conv_w, sgu_ln_g, sgu_ln_b, sgu_w, sgu_b, diff_lq1, diff_lk1, diff_lq2, diff_lk2, diff_subln_g, rel_bias, ln1_g, ln1_b, router_w, router_bias, moe_w_gu, moe_w_down, shared_w_gu, shared_w_down, ln2_g, ln2_b):
    Bt, L, D = x.shape
    assert Bt == 1
    depth = w_in.shape[0]
    alpha = (2 * depth) ** 0.25
    gw = GROUP_W
    tq = _pick(L, 256)
    nb = _attn_bias_tables(rel_bias, tq)
    xf = x.reshape(L, D)
    xb = xf.astype(BF16)
    for l in range(depth):
        w_in_b = w_in[l].astype(BF16)
        proj_a = _matmul(xb, w_in_b[:, :6 * gw], _pick(L, 1024), 512, F32)
        kmat = _matmul(xb, w_in_b[:, 7 * gw:8 * gw], _pick(L, 1024), 512, BF16)
        w_q = (w_in[l][:, 6 * gw:7 * gw] * (DIFF_QK_DIM ** -0.5 * LOG2E)).astype(BF16)
        w_qv_t = jnp.concatenate([w_q, w_in_b[:, 8 * gw:]], axis=1).T
        qvt = _matmul_nt(w_qv_t, xb, tq, BF16)
        s5_rt = _pick(L // S5_SUB, 256)
        tabs = _s5_tables(s5_lambda_re[l], s5_lambda_im[l], s5_log_dt[l], s5_b_re[l], s5_b_im[l],
                          s5_c_re[l], s5_c_im[l], s5_rt)
        ys5 = _s5_scan(proj_a, tabs, s5_rt)
        abc = _mixers(proj_a, ys5, s5_d[l], s5_w_glu[l], conv_w[l], sgu_ln_g[l], sgu_ln_b[l], sgu_w[l], sgu_b[l],
                      mix_norm_g[l], _pick(L, 512))
        lambda_init = 0.8 - 0.6 * math.exp(-0.3 * l)
        d_out = _diff_attention(qvt, kmat, nb, diff_lq1[l], diff_lk1[l], diff_lq2[l], diff_lk2[l],
                                diff_subln_g[l], lambda_init)
        x1, x1b, x1p, logits_t = _outproj(abc, d_out, xf, w_out[l].astype(BF16), ln1_g[l], ln1_b[l], router_w[l],
                                          alpha, _pick(L, 256))
        xf, xb = _moe_layer_fused(x1, x1b, x1p, logits_t, router_bias[l], moe_w_gu, moe_w_down, l,
                                  shared_w_gu[l].astype(BF16), shared_w_down[l].astype(BF16),
                                  ln2_g[l], ln2_b[l], alpha, _pick(L, 512), _pick(L, 256))
    return xf.reshape(Bt, L, D)
```

```python
import functools
import math

import jax
import jax.numpy as jnp
from jax import lax
from jax.experimental import pallas as pl
from jax.experimental.pallas import tpu as pltpu

F32 = jnp.float32
BF16 = jnp.bfloat16

GROUP_W = 512
CHUNK = 64
S5_GROUP_CH = 16
S5_GROUPS = 32
S5_STATE = 64
S5_SUB = 16
S5_KBLOCKS = 4
SGU_BLK = 128
SGU_HEADS = 4
DIFF_HEADS = 4
DIFF_QK_DIM = 64
DIFF_V_DIM = 128
NUM_BUCKETS = 32
MAX_DISTANCE = 128
N_EXPERTS = 64
TOP_K = 8
N_EXPERT_GROUPS = 8
TOPK_GROUPS = 4
ROUTED_SCALE = 2.5
EPS = 1e-5
NEG_INF = -1e30
LOG2E = math.log2(math.e)
ONES_ROWS = 16

VMEM_LIMIT = 56 * 1024 * 1024


def _cparams(sem):
    return pltpu.CompilerParams(dimension_semantics=sem, vmem_limit_bytes=VMEM_LIMIT)


def _rms(x, g):
    return x * lax.rsqrt(jnp.mean(jnp.square(x), -1, keepdims=True) + EPS) * g


def _pack_bf16_pairs(x):
    c = x.shape[1] // 2
    hi = lax.bitcast_convert_type(x[:, :c].astype(BF16).astype(F32), jnp.uint32)
    lo = lax.bitcast_convert_type(x[:, c:].astype(BF16).astype(F32), jnp.uint32)
    return hi | (lo >> 16)


def _unpack_bf16_pairs(u):
    hi = lax.bitcast_convert_type(u & jnp.uint32(0xFFFF0000), F32)
    lo = lax.bitcast_convert_type(u << 16, F32)
    return hi, lo


ROW_TILE = 8
LANES = 128
SUB_ROWS = 128
MOE_BLK = 256


def _store_token_tiles(ref, x):
    p = _pack_bf16_pairs(x)
    n = x.shape[0]
    for c in range(ROW_TILE):
        ref[pl.ds(c, n, stride=ROW_TILE), :] = p[:, c * LANES:(c + 1) * LANES]


def _load_token_tiles(ref, n):
    p = jnp.concatenate([ref[pl.ds(c, n, stride=ROW_TILE), :] for c in range(ROW_TILE)], axis=1)
    return _unpack_bf16_pairs(p)


def _ln(x, g, b):
    mu = jnp.mean(x, -1, keepdims=True)
    var = jnp.mean(jnp.square(x - mu), -1, keepdims=True)
    return (x - mu) * lax.rsqrt(var + EPS) * g + b


def _matmul_kernel(x_ref, w_ref, o_ref):
    o_ref[...] = jnp.dot(x_ref[...], w_ref[...], preferred_element_type=F32).astype(o_ref.dtype)


def _matmul(x, w, tm, tn, out_dtype):
    M, K = x.shape
    N = w.shape[1]
    return pl.pallas_call(
        _matmul_kernel,
        grid=(M // tm, N // tn),
        in_specs=[pl.BlockSpec((tm, K), lambda i, j: (i, 0)),
                  pl.BlockSpec((K, tn), lambda i, j: (0, j))],
        out_specs=pl.BlockSpec((tm, tn), lambda i, j: (i, j)),
        out_shape=jax.ShapeDtypeStruct((M, N), out_dtype),
        compiler_params=_cparams(("parallel", "arbitrary")),
        name="proj_matmul",
    )(x, w)


def _matmul_nt_kernel(w_ref, x_ref, o_ref):
    o_ref[0] = lax.dot_general(w_ref[...], x_ref[...], (((1,), (1,)), ((), ())),
                               preferred_element_type=F32).astype(o_ref.dtype)


def _matmul_nt(w_t, x, tm, out_dtype):
    M, K = x.shape
    N = w_t.shape[0]
    return pl.pallas_call(
        _matmul_nt_kernel,
        grid=(M // tm,),
        in_specs=[pl.BlockSpec((N, K), lambda i: (0, 0)),
                  pl.BlockSpec((tm, K), lambda i: (i, 0))],
        out_specs=pl.BlockSpec((1, N, tm), lambda i: (i, 0, 0)),
        out_shape=jax.ShapeDtypeStruct((M // tm, N, tm), out_dtype),
        compiler_params=_cparams(("parallel",)),
        name="proj_matmul_nt",
    )(w_t, x)


def _s5_tables(lam_re, lam_im, log_dt, b_re, b_im, c_re, c_im, n_rows):
    G, P, H, S = S5_GROUPS, S5_STATE, S5_GROUP_CH, S5_SUB
    hp = lax.Precision.HIGHEST
    dt = jnp.exp(log_dt.astype(F32))[:, None]
    lam = lax.complex(lam_re.astype(F32), lam_im.astype(F32))
    ldt = lam * dt
    lam_bar = jnp.exp(ldt)
    b_bar = ((lam_bar - 1.0) / lam)[..., None] * lax.complex(b_re.astype(F32), b_im.astype(F32))
    c = lax.complex(c_re.astype(F32), c_im.astype(F32))
    tau = jnp.arange(S + 1, dtype=F32)
    pows = jnp.exp(ldt[None] * tau[:, None, None])
    KB, GL = S5_KBLOCKS, S5_GROUPS // S5_KBLOCKS
    eye = jnp.eye(GL, dtype=bool)
    w1 = pows[:S][::-1][:, :, None, :] * jnp.transpose(b_bar, (0, 2, 1))[None]
    w1 = jnp.transpose(w1.reshape(S, KB, GL, H, P), (1, 0, 2, 3, 4))
    w1 = jnp.where(eye[None, None, :, None, :, None], w1[:, :, :, :, None, :], 0.0)
    w1 = w1.reshape(KB, S * GL * H, GL * P)
    w2 = jnp.transpose(c, (0, 2, 1))[:, :, None, :] * jnp.transpose(pows[1:], (1, 2, 0))[..., None]
    w2 = w2.reshape(KB, GL, P, S, H)
    w2 = jnp.where(eye[None, :, None, None, :, None], w2[:, :, :, :, None, :], 0.0)
    w2 = w2.reshape(KB, GL * P, S * GL * H)
    kc = jnp.real(jnp.einsum('ghp,tgp,gpi->tghi', c, pows[:S], b_bar, precision=hp))
    kc = jnp.transpose(kc.reshape(S, KB, GL, H, H), (1, 2, 4, 0, 3))
    kc = jnp.where(eye[None, :, None, None, :, None], kc[:, :, :, :, None, :], 0.0)
    kcat = kc.reshape(KB, GL * H, S * GL * H)

    nstep = max(1, (n_rows - 1).bit_length())
    kk = (S * (2 ** jnp.arange(nstep))).astype(F32)
    lp = jnp.exp(ldt[None] * kk[:, None, None])
    lp = jnp.transpose(lp.reshape(nstep, KB, GL * P), (1, 0, 2))
    lampow = jnp.stack([jnp.real(lp), jnp.imag(lp)], axis=2)
    return dict(
        kcat=kcat.astype(BF16),
        w1re=jnp.real(w1).astype(BF16), w1im=jnp.imag(w1).astype(BF16),
        w2re=jnp.real(w2).astype(BF16), w2im=(-jnp.imag(w2)).astype(BF16),
        lampow=lampow.astype(F32))


def _s5_kernel(u_ref, w1re_ref, w1im_ref, w2re_ref, w2im_ref, kcat_ref, lp_ref, o_ref,
               ucat_ref, yall_ref, carry_ref, *, nstep):
    t = pl.program_id(1)
    S = S5_SUB
    R = ucat_ref.shape[0]
    W = u_ref.shape[1]

    @pl.when(t == 0)
    def _():
        carry_ref[...] = jnp.zeros_like(carry_ref)

    for j in range(S):
        ucat_ref[:, j * W:(j + 1) * W] = u_ref[pl.ds(j, R, stride=S), :].astype(BF16)
    ucat = ucat_ref[...]
    xre = jnp.dot(ucat, w1re_ref[0], preferred_element_type=F32)
    xim = jnp.dot(ucat, w1im_ref[0], preferred_element_type=F32)
    row = lax.broadcasted_iota(jnp.int32, xre.shape, 0)
    cre, cim = carry_ref[0], carry_ref[1]
    lr, li = lp_ref[0, 0, 0:1, :], lp_ref[0, 0, 1:2, :]
    xre = xre + jnp.where(row == 0, lr * cre - li * cim, 0.0)
    xim = xim + jnp.where(row == 0, lr * cim + li * cre, 0.0)
    for k in range(nstep):
        sh = 1 << k
        pre = pltpu.roll(xre, sh, 0)
        pim = pltpu.roll(xim, sh, 0)
        lr, li = lp_ref[0, k, 0:1, :], lp_ref[0, k, 1:2, :]
        keep = row >= sh
        xre, xim = (xre + jnp.where(keep, lr * pre - li * pim, 0.0),
                    xim + jnp.where(keep, lr * pim + li * pre, 0.0))
    carry_ref[0] = xre[R - 1:R, :]
    carry_ref[1] = xim[R - 1:R, :]
    sre = jnp.where(row >= 1, pltpu.roll(xre, 1, 0), cre).astype(BF16)
    sim = jnp.where(row >= 1, pltpu.roll(xim, 1, 0), cim).astype(BF16)
    yall_ref[...] = (jnp.dot(sre, w2re_ref[0], preferred_element_type=F32)
                     + jnp.dot(sim, w2im_ref[0], preferred_element_type=F32))
    for j in range(S):
        yall_ref[:, j * W:] += jnp.dot(ucat_ref[:, j * W:(j + 1) * W], kcat_ref[0, :, :(S - j) * W],
                                       preferred_element_type=F32)
    for j in range(S):
        o_ref[pl.ds(j, R, stride=S), :] = yall_ref[:, j * W:(j + 1) * W]


def _s5_scan(proj_a, tabs, rt):
    L = proj_a.shape[0]
    S, KB = S5_SUB, S5_KBLOCKS
    W = GROUP_W // KB
    nstep = tabs['lampow'].shape[1]
    P2 = tabs['lampow'].shape[3]
    rows = rt * S
    kb3 = lambda a: pl.BlockSpec((1,) + a.shape[1:], lambda k, t: (k, 0, 0))
    return pl.pallas_call(
        functools.partial(_s5_kernel, nstep=nstep),
        grid=(KB, L // rows),
        in_specs=[pl.BlockSpec((rows, W), lambda k, t: (t, k)),
                  kb3(tabs['w1re']), kb3(tabs['w1im']), kb3(tabs['w2re']), kb3(tabs['w2im']), kb3(tabs['kcat']),
                  pl.BlockSpec((1, nstep, 2, P2), lambda k, t: (k, 0, 0, 0))],
        out_specs=pl.BlockSpec((rows, W), lambda k, t: (t, k)),
        out_shape=jax.ShapeDtypeStruct((L, GROUP_W), F32),
        scratch_shapes=[pltpu.VMEM((rt, S * W), BF16), pltpu.VMEM((rt, S * W), F32), pltpu.VMEM((2, 1, P2), F32)],
        compiler_params=_cparams(("parallel", "arbitrary")),
        name="s5_scan",
    )(proj_a, tabs['w1re'], tabs['w1im'], tabs['w2re'], tabs['w2im'], tabs['kcat'], tabs['lampow'])


def _mixers_kernel(s5u_ref, cb_ref, cc_ref, ch_ref, su_ref, sv_ref, cch_ref, chh_ref, ys_ref,
                   d_ref, wglu_ref, cw_ref, lng_ref, lnb_ref, ws_ref, bs_ref, g_ref, o_ref):
    i = pl.program_id(0)
    tm = o_ref.shape[0]
    gw = GROUP_W
    y = ys_ref[...] + d_ref[...] * s5u_ref[...]
    y = jax.nn.gelu(y)
    y = y * jax.nn.sigmoid(jnp.dot(y.astype(BF16), wglu_ref[...], preferred_element_type=F32))
    o_ref[:, 0:gw] = _rms(y, g_ref[0:1, :]).astype(o_ref.dtype)
    z = cc_ref[...] * ch_ref[...]
    zh = jnp.where(i > 0, cch_ref[...] * chh_ref[...], 0.0)
    row = lax.broadcasted_iota(jnp.int32, z.shape, 0)
    z1 = jnp.where(row == 0, zh[7:8, :], pltpu.roll(z, 1, 0))
    z2 = jnp.where(row == 0, zh[6:7, :], jnp.where(row == 1, zh[7:8, :], pltpu.roll(z, 2, 0)))
    conv = cw_ref[0:1, :] * z2 + cw_ref[1:2, :] * z1 + cw_ref[2:3, :] * z
    o_ref[:, gw:2 * gw] = _rms(cb_ref[...] * conv, g_ref[1:2, :]).astype(o_ref.dtype)
    uu = jax.nn.gelu(su_ref[...])
    vv = _ln(jax.nn.gelu(sv_ref[...]), lng_ref[...], lnb_ref[...]).astype(BF16)
    pi = lax.broadcasted_iota(jnp.int32, (SGU_BLK, SGU_BLK), 0)
    pj = lax.broadcasted_iota(jnp.int32, (SGU_BLK, SGU_BLK), 1)
    causal = (pj // CHUNK) <= (pi // CHUNK)
    hd = gw // SGU_HEADS
    ws = [jnp.where(causal, ws_ref[h], 0.0).astype(BF16) for h in range(SGU_HEADS)]
    blocks = []
    for n in range(tm // SGU_BLK):
        vb = vv[n * SGU_BLK:(n + 1) * SGU_BLK, :]
        blocks.append(jnp.concatenate(
            [jnp.dot(ws[h], vb[:, h * hd:(h + 1) * hd], preferred_element_type=F32) for h in range(SGU_HEADS)],
            axis=1) + bs_ref[...])
    mixed = jnp.concatenate(blocks, axis=0)
    o_ref[:, 2 * gw:3 * gw] = _rms(uu * mixed, g_ref[2:3, :]).astype(o_ref.dtype)


def _mixers(proj_a, ys5, s5_d, w_glu, conv_w, ln_g, ln_b, sgu_w, sgu_b, mix_g, tm):
    L = proj_a.shape[0]
    gw = GROUP_W
    hb = tm // 8
    col = lambda c: pl.BlockSpec((tm, gw), lambda i, c=c: (i, c))
    halo = lambda c: pl.BlockSpec((8, gw), lambda i, c=c: (jnp.maximum(i * hb - 1, 0), c))
    full = lambda a: pl.BlockSpec(a.shape, lambda i: (0,) * a.ndim)
    hd = gw // SGU_HEADS
    bs_full = jnp.repeat(sgu_b.astype(F32).T, hd, axis=1)
    consts = [s5_d.reshape(1, gw).astype(F32), w_glu.astype(BF16), conv_w.astype(F32),
              ln_g.reshape(1, gw).astype(F32), ln_b.reshape(1, gw).astype(F32), sgu_w.astype(F32),
              bs_full, mix_g.reshape(3, gw).astype(F32)]
    return pl.pallas_call(
        _mixers_kernel,
        grid=(L // tm,),
        in_specs=[col(0), col(1), col(2), col(3), col(4), col(5), halo(2), halo(3),
                  pl.BlockSpec((tm, gw), lambda i: (i, 0))] + [full(a) for a in consts],
        out_specs=pl.BlockSpec((tm, 3 * gw), lambda i: (i, 0)),
        out_shape=jax.ShapeDtypeStruct((L, 3 * gw), BF16),
        compiler_params=_cparams(("parallel",)),
        name="row_mixers",
    )(proj_a, proj_a, proj_a, proj_a, proj_a, proj_a, proj_a, proj_a, ys5, *consts)


def _t5_bucket(rel):
    half = NUM_BUCKETS // 2
    ret = jnp.where(rel > 0, half, 0)
    n = jnp.abs(rel)
    max_exact = half // 2
    large = max_exact + (jnp.log(jnp.maximum(n, 1).astype(F32) / max_exact)
                         / math.log(MAX_DISTANCE / max_exact) * (half - max_exact)).astype(jnp.int32)
    large = jnp.minimum(large, half - 1)
    return ret + jnp.where(n < max_exact, n, large)


def _attn_bias_tables(rel_bias, tq):
    assert tq >= MAX_DISTANCE
    rb = rel_bias.astype(F32)
    far = rb[NUM_BUCKETS // 2 - 1]
    buckets = jnp.arange(NUM_BUCKETS)[:, None]

    def bias_of(rel):
        onehot = _t5_bucket(rel)[:, :, None, None] == buckets
        return jnp.sum(jnp.where(onehot, rb, 0.0), axis=2) - far

    kj = jnp.arange(tq)[:, None]
    qi = jnp.arange(tq)[None, :]
    diag = jnp.where(((kj // CHUNK) <= (qi // CHUNK))[..., None], bias_of(kj - qi), NEG_INF)
    prev = bias_of(kj - tq - qi)
    tabs = jnp.stack([jnp.transpose(diag, (2, 0, 1)), jnp.transpose(prev, (2, 0, 1))], axis=1)
    tabs = jnp.where(tabs > 0.5 * NEG_INF, tabs * LOG2E, NEG_INF)
    return jnp.concatenate([tabs, tabs], axis=3)


def _attn_kernel(qt_ref, k_ref, vt_ref, nb_ref, lq1_ref, lk1_ref, lq2_ref, lk2_ref, g_ref, o_ref,
                 qq_s, sa_s, sb_s, p_s, m_s, a_s, acc_s, *, lambda_init):
    i = pl.program_id(1)
    tq = qt_ref.shape[2]
    dq = DIFF_QK_DIM
    qt = qt_ref[0]
    feat = lax.broadcasted_iota(jnp.int32, qt.shape, 0)
    zero = jnp.zeros_like(qt)
    qq = jnp.concatenate([jnp.where(feat < dq, qt, zero), jnp.where(feat >= dq, qt, zero)], axis=1)

    qq_s[...] = qq
    m_s[...] = jnp.full(m_s.shape, -jnp.inf, F32)
    acc_s[...] = jnp.zeros(acc_s.shape, F32)

    def scores(s_ref, j, nblk):
        rows = nblk * tq
        kb = k_ref[pl.ds(pl.multiple_of(j * tq, tq), rows), :]
        s_ref[:rows, :] = jnp.dot(kb, qq_s[...], preferred_element_type=F32)

    def absorb(s_ref, j, nblk, bias=None):
        rows = nblk * tq
        for c in range(2 * tq // 128):
            cs = slice(c * 128, (c + 1) * 128)
            s = s_ref[:rows, cs]
            if bias is not None:
                s = s + bias(cs)
            m_old = m_s[:, cs]
            m_new = jnp.maximum(m_old, jnp.max(s, axis=0, keepdims=True))
            m_s[:, cs] = m_new
            a_s[:, cs] = jnp.exp2(m_old - m_new)
            p_s[:rows, cs] = jnp.exp2((s - m_new).astype(BF16))
        vt = jnp.concatenate([vt_ref[j + b] for b in range(nblk)], axis=1)
        vt = jnp.concatenate([vt, jnp.ones((ONES_ROWS, rows), BF16)], axis=0)
        acc_s[...] = a_s[...] * acc_s[...] + jnp.dot(vt, p_s[:rows, :], preferred_element_type=F32)

    scores(sa_s, i, 1)
    absorb(sa_s, i, 1, lambda cs: nb_ref[0, 0, :, cs])
    jp = jnp.maximum(i - 1, 0)
    first = jnp.where(i > 0, 0.0, NEG_INF)
    scores(sa_s, jp, 1)
    absorb(sa_s, jp, 1, lambda cs: nb_ref[0, 1, :, cs] + first)
    n_far = jnp.maximum(i - 1, 0)
    n_single = n_far % 2
    n_head = n_far % 4

    @pl.when(n_single == 1)
    def _():
        scores(sb_s, 0, 1)
        absorb(sb_s, 0, 1)

    @pl.when(n_head >= 2)
    def _():
        scores(sb_s, n_single, 2)
        absorb(sb_s, n_single, 2)

    scores(sa_s, n_head, 2)

    def quad(qd, c):
        j0 = n_head + 4 * qd
        scores(sb_s, j0 + 2, 2)
        absorb(sa_s, j0, 2)
        scores(sa_s, j0 + 4, 2)
        absorb(sb_s, j0 + 2, 2)
        return c

    lax.fori_loop(0, n_far // 4, quad, 0)
    o = acc_s[:DIFF_V_DIM, :] / acc_s[DIFF_V_DIM:DIFF_V_DIM + 1, :]
    lam = (jnp.exp(jnp.sum(lq1_ref[...] * lk1_ref[...], keepdims=True))
           - jnp.exp(jnp.sum(lq2_ref[...] * lk2_ref[...], keepdims=True)) + lambda_init)
    out = o[:, :tq] - lam * o[:, tq:]
    out = out * lax.rsqrt(jnp.mean(jnp.square(out), axis=0, keepdims=True) + EPS) * g_ref[...]
    o_ref[...] = (out * (1.0 - lambda_init)).T.astype(o_ref.dtype)


def _diff_attention(qvt, kmat, nb, lq1, lk1, lq2, lk2, subln_g, lambda_init):
    nq, _, tq = qvt.shape
    assert nq >= 2
    L = kmat.shape[0]
    H, dv = DIFF_HEADS, DIFF_V_DIM
    vec = lambda a: a.reshape(1, -1).astype(F32)
    small = lambda n: pl.BlockSpec((1, n), lambda h, i: (0, 0))
    return pl.pallas_call(
        functools.partial(_attn_kernel, lambda_init=lambda_init),
        grid=(H, nq),
        in_specs=[pl.BlockSpec((1, dv, tq), lambda h, i: (i, h, 0)),
                  pl.BlockSpec((L, dv), lambda h, i: (0, h)),
                  pl.BlockSpec((nq, dv, tq), lambda h, i: (0, H + h, 0)),
                  pl.BlockSpec((1, 2, tq, 2 * tq), lambda h, i: (h, 0, 0, 0)),
                  small(DIFF_QK_DIM), small(DIFF_QK_DIM), small(DIFF_QK_DIM), small(DIFF_QK_DIM),
                  pl.BlockSpec((dv, 1), lambda h, i: (0, 0))],
        out_specs=pl.BlockSpec((tq, dv), lambda h, i: (i, h)),
        out_shape=jax.ShapeDtypeStruct((L, H * dv), BF16),
        scratch_shapes=[pltpu.VMEM((dv, 2 * tq), BF16),
                        pltpu.VMEM((2 * tq, 2 * tq), F32), pltpu.VMEM((2 * tq, 2 * tq), F32),
                        pltpu.VMEM((2 * tq, 2 * tq), BF16),
                        pltpu.VMEM((1, 2 * tq), F32), pltpu.VMEM((1, 2 * tq), F32),
                        pltpu.VMEM((dv + ONES_ROWS, 2 * tq), F32)],
        compiler_params=_cparams(("parallel", "arbitrary")),
        name="diff_attention",
    )(qvt, kmat, qvt, nb, vec(lq1), vec(lk1), vec(lq2), vec(lk2), subln_g.reshape(dv, 1).astype(F32))


def _outproj_kernel(abc_ref, d_ref, x_ref, wa_ref, wd_ref, g_ref, b_ref, rhi_ref, rlo_ref,
                    x1_ref, x1b_ref, x1p_ref, lg_ref, *, alpha):
    nt = (((1,), (1,)), ((), ()))
    tm = x_ref.shape[0]
    sub = min(tm, SUB_ROWS)
    for r0 in range(0, tm, sub):
        rows = pl.ds(r0, sub)
        mix = (jnp.dot(abc_ref[rows, :], wa_ref[...], preferred_element_type=F32)
               + jnp.dot(d_ref[rows, :], wd_ref[...], preferred_element_type=F32))
        x1 = _ln(alpha * x_ref[rows, :] + mix, g_ref[...], b_ref[...])
        x1_ref[rows, :] = x1
        hi = x1.astype(BF16)
        x1b_ref[rows, :] = hi
        _store_token_tiles(x1p_ref.at[pl.ds(r0 * ROW_TILE, sub * ROW_TILE), :], x1)
        lo = (x1 - hi.astype(F32)).astype(BF16)
        lg_ref[:, r0:r0 + sub] = (lax.dot_general(rhi_ref[...], hi, nt, preferred_element_type=F32)
                                  + lax.dot_general(rhi_ref[...], lo, nt, preferred_element_type=F32)
                                  + lax.dot_general(rlo_ref[...], hi, nt, preferred_element_type=F32))


def _outproj(abc, d_out, x, w_out, ln_g, ln_b, router_w, alpha, tm):
    L, D = x.shape
    E = router_w.shape[1]
    ka = abc.shape[1]
    kd = d_out.shape[1]
    rwt = router_w.astype(F32).T
    rhi = rwt.astype(BF16)
    rlo = (rwt - rhi.astype(F32)).astype(BF16)
    full = lambda shape: pl.BlockSpec(shape, lambda i: (0,) * len(shape))
    return pl.pallas_call(
        functools.partial(_outproj_kernel, alpha=alpha),
        grid=(L // tm,),
        in_specs=[pl.BlockSpec((tm, ka), lambda i: (i, 0)),
                  pl.BlockSpec((tm, kd), lambda i: (i, 0)),
                  pl.BlockSpec((tm, D), lambda i: (i, 0)),
                  pl.BlockSpec((ka, D), lambda i: (0, 0)),
                  pl.BlockSpec((kd, D), lambda i: (ka // kd, 0)),
                  full((1, D)), full((1, D)), full((E, D)), full((E, D))],
        out_specs=[pl.BlockSpec((tm, D), lambda i: (i, 0)),
                   pl.BlockSpec((tm, D), lambda i: (i, 0)),
                   pl.BlockSpec((tm * ROW_TILE, LANES), lambda i: (i, 0)),
                   pl.BlockSpec((E, tm), lambda i: (0, i))],
        out_shape=[jax.ShapeDtypeStruct((L, D), F32), jax.ShapeDtypeStruct((L, D), BF16),
                   jax.ShapeDtypeStruct((L * ROW_TILE, LANES), jnp.uint32), jax.ShapeDtypeStruct((E, L), F32)],
        compiler_params=_cparams(("parallel",)),
        name="outproj_ln1",
    )(abc, d_out, x, w_out, w_out, ln_g.reshape(1, D).astype(F32), ln_b.reshape(1, D).astype(F32), rhi, rlo)


def _router_kernel(lg_ref, bias_ref, e_ref, w_ref, cnt_ref, carry_ref):
    i = pl.program_id(0)
    E, tn = lg_ref.shape
    ng = N_EXPERT_GROUPS
    gs_ = E // ng

    @pl.when(i == 0)
    def _():
        carry_ref[...] = jnp.zeros_like(carry_ref)

    s = jax.nn.sigmoid(lg_ref[...])
    sel = s + bias_ref[...]
    midx = lax.broadcasted_iota(jnp.int32, (gs_, tn), 0).astype(F32)
    rows, gscore = [], []
    for g in range(ng):
        rg = sel[g * gs_:(g + 1) * gs_, :]
        m1 = jnp.max(rg, axis=0, keepdims=True)
        first = jnp.min(jnp.where(rg == m1, midx, float(gs_)), axis=0, keepdims=True)
        m2 = jnp.max(jnp.where(midx == first, -jnp.inf, rg), axis=0, keepdims=True)
        rows.append(rg)
        gscore.append(m1 + m2)
    vals = []
    for g in range(ng):
        rank = jnp.zeros((1, tn), F32)
        for o in range(ng):
            if o != g:
                beats = (gscore[o] >= gscore[g]) if o < g else (gscore[o] > gscore[g])
                rank = rank + jnp.where(beats, 1.0, 0.0)
        vals.append(jnp.where(rank < TOPK_GROUPS, rows[g], -jnp.inf))
    val = jnp.concatenate(vals, axis=0)
    eidx = lax.broadcasted_iota(jnp.int32, val.shape, 0)
    erank = jnp.zeros(val.shape, F32)
    for e in range(E):
        other = val[e:e + 1, :]
        erank = erank + jnp.where(eidx > e, jnp.where(other >= val, 1.0, 0.0), jnp.where(other > val, 1.0, 0.0))
    chosen = erank < TOP_K
    wsel = jnp.where(chosen, s, 0.0)
    wn = wsel / (jnp.sum(wsel, axis=0, keepdims=True) + 1e-20) * ROUTED_SCALE
    total = carry_ref[...] + jnp.sum(jnp.where(chosen, 1.0, 0.0), axis=1, keepdims=True)
    carry_ref[...] = total
    cnt_ref[...] = total.astype(jnp.int32)
    eidf = eidx.astype(F32)
    cand = jnp.where(chosen, eidf, float(E))
    for k in range(TOP_K):
        ek = jnp.min(cand, axis=0, keepdims=True)
        hit = cand == ek
        e_ref[k:k + 1, :] = ek.astype(jnp.int32)
        w_ref[k:k + 1, :] = jnp.sum(jnp.where(hit, wn, 0.0), axis=0, keepdims=True)
        cand = jnp.where(hit, float(E), cand)


def _router(logits_t, router_bias, tn):
    E, L = logits_t.shape
    slot = lambda dt: jax.ShapeDtypeStruct((TOP_K, L), dt)
    return pl.pallas_call(
        _router_kernel,
        grid=(L // tn,),
        in_specs=[pl.BlockSpec((E, tn), lambda i: (0, i)),
                  pl.BlockSpec((E, 1), lambda i: (0, 0))],
        out_specs=[pl.BlockSpec((TOP_K, tn), lambda i: (0, i)),
                   pl.BlockSpec((TOP_K, tn), lambda i: (0, i)),
                   pl.BlockSpec((E, 1), lambda i: (0, 0))],
        out_shape=[slot(jnp.int32), slot(F32), jax.ShapeDtypeStruct((E, 1), jnp.int32)],
        scratch_shapes=[pltpu.VMEM((E, 1), F32)],
        compiler_params=_cparams(("arbitrary",)),
        name="router_topk",
    )(logits_t, router_bias.reshape(E, 1).astype(F32))


def _experts_fused_kernel(blk_ref, exp_ref, nv_ref,
                          src_ref, srcn_ref, dst_ref, x_hbm, wgu_ref, wd_ref, y_hbm,
                          xbuf, ybuf, wgu_b, wd_b, gsem, ssem):
    w = pl.program_id(0)
    nv = nv_ref[0]
    R = xbuf.shape[1] // ROW_TILE
    par = w % 2

    def tile(buf, r):
        return buf.at[pl.ds(r * ROW_TILE, ROW_TILE), :]

    def gather(s_ref, p):
        for r in range(R):
            pltpu.make_async_copy(x_hbm.at[s_ref[0, 0, r]], tile(xbuf.at[p], r), gsem.at[p]).start()

    def gather_wait(p):
        pltpu.make_async_copy(xbuf.at[1 - p], xbuf.at[p], gsem.at[p]).wait()

    def scatter_wait(p):
        pltpu.make_async_copy(ybuf.at[p], ybuf.at[1 - p], ssem.at[p]).wait()

    @pl.when(w == 0)
    def _():
        gather(src_ref, 0)

    @pl.when((w < nv) & ((w == 0) | (exp_ref[w] != exp_ref[jnp.maximum(w - 1, 0)])))
    def _():
        wgu_b[...] = wgu_ref[0, 0].astype(BF16)
        wd_b[...] = wd_ref[0, 0].astype(BF16)

    @pl.when((w >= 2) & (w < nv))
    def _():
        scatter_wait(par)

    @pl.when(w < nv)
    def _():
        gather_wait(par)
        gather(srcn_ref, 1 - par)
        de = wd_b.shape[0]
        xa, xb = _load_token_tiles(xbuf.at[par], R)
        x = jnp.concatenate([xa.astype(BF16), xb.astype(BF16)], axis=1)
        h = jnp.dot(x, wgu_b[...], preferred_element_type=F32)
        a = jax.nn.silu(h[:, :de]) * h[:, de:]
        _store_token_tiles(ybuf.at[par], jnp.dot(a.astype(BF16), wd_b[...], preferred_element_type=F32))
        for r in range(R):
            pltpu.make_async_copy(tile(ybuf.at[par], r), y_hbm.at[dst_ref[0, 0, r]], ssem.at[par]).start()

    @pl.when(w == nv - 1)
    def _():
        scatter_wait(par)
        gather_wait(1 - par)

    @pl.when((w == nv - 1) & (w >= 1))
    def _():
        scatter_wait(1 - par)


def _experts_fused(x1p, src, dst, items, w_gu, w_down, layer, n_out_tiles, blk):
    blk_w, exp_w, n_valid = items
    n_items = blk_w.shape[0]
    D, de2 = w_gu.shape[2], w_gu.shape[3]
    de = w_down.shape[2]
    cur = lambda w, nv: jnp.minimum(w, nv[0] - 1)
    return pl.pallas_call(
        _experts_fused_kernel,
        grid_spec=pltpu.PrefetchScalarGridSpec(
            num_scalar_prefetch=3,
            grid=(n_items,),
            in_specs=[pl.BlockSpec((1, 1, blk), lambda w, b, e, nv: (b[cur(w, nv)], 0, 0),
                                   memory_space=pltpu.SMEM),
                      pl.BlockSpec((1, 1, blk), lambda w, b, e, nv: (b[cur(w + 1, nv)], 0, 0),
                                   memory_space=pltpu.SMEM),
                      pl.BlockSpec((1, 1, blk), lambda w, b, e, nv: (cur(w, nv), 0, 0),
                                   memory_space=pltpu.SMEM),
                      pl.BlockSpec(memory_space=pl.ANY),
                      pl.BlockSpec((1, 1, D, de2), lambda w, b, e, nv: (layer, e[cur(w, nv)], 0, 0)),
                      pl.BlockSpec((1, 1, de, D), lambda w, b, e, nv: (layer, e[cur(w, nv)], 0, 0))],
            out_specs=pl.BlockSpec(memory_space=pl.ANY),
            scratch_shapes=[pltpu.VMEM((2, blk * ROW_TILE, LANES), jnp.uint32),
                            pltpu.VMEM((2, blk * ROW_TILE, LANES), jnp.uint32),
                            pltpu.VMEM((D, de2), BF16), pltpu.VMEM((de, D), BF16),
                            pltpu.SemaphoreType.DMA((2,)), pltpu.SemaphoreType.DMA((2,))]),
        out_shape=jax.ShapeDtypeStruct((n_out_tiles, ROW_TILE, LANES), jnp.uint32),
        compiler_params=_cparams(("arbitrary",)),
        name="moe_experts_fused",
    )(blk_w, exp_w, n_valid, src, src, dst, x1p.reshape(-1, ROW_TILE, LANES), w_gu, w_down).reshape(-1, LANES)


def _combine_stream_kernel(*refs, alpha):
    y_refs = refs[:TOP_K]
    w_ref, x1_ref, x1b_ref, sgu_ref, sdn_ref, g_ref, b_ref, x2_ref, x2b_ref = refs[TOP_K:]
    tm = x1_ref.shape[0]
    de = sdn_ref.shape[0]
    h = jnp.dot(x1b_ref[...], sgu_ref[...], preferred_element_type=F32)
    a = jax.nn.silu(h[:, :de]) * h[:, de:]
    ffn = jnp.dot(a.astype(BF16), sdn_ref[...], preferred_element_type=F32)
    half = ffn.shape[1] // 2
    fa, fb = ffn[:, :half], ffn[:, half:]
    for k in range(TOP_K):
        ya, yb = _load_token_tiles(y_refs[k], tm)
        w = w_ref[:, k:k + 1]
        fa = fa + ya * w
        fb = fb + yb * w
    ffn = jnp.concatenate([fa, fb], axis=1)
    x2 = _ln(alpha * x1_ref[...] + ffn, g_ref[...], b_ref[...])
    x2_ref[...] = x2
    x2b_ref[...] = x2.astype(BF16)


def _combine_stream(yk, w_t, x1, x1b, sh_gu, sh_down, ln_g, ln_b, alpha, tm):
    L, D = x1.shape
    n = L // tm
    full = lambda shape: pl.BlockSpec(shape, lambda i: (0,) * len(shape))
    y_specs = [pl.BlockSpec((tm * ROW_TILE, LANES), lambda i, k=k: (k * n + i, 0)) for k in range(TOP_K)]
    return pl.pallas_call(
        functools.partial(_combine_stream_kernel, alpha=alpha),
        grid=(n,),
        in_specs=y_specs + [pl.BlockSpec((tm, TOP_K), lambda i: (i, 0)),
                            pl.BlockSpec((tm, D), lambda i: (i, 0)),
                            pl.BlockSpec((tm, D), lambda i: (i, 0)),
                            full(sh_gu.shape), full(sh_down.shape), full((1, D)), full((1, D))],
        out_specs=[pl.BlockSpec((tm, D), lambda i: (i, 0)), pl.BlockSpec((tm, D), lambda i: (i, 0))],
        out_shape=[jax.ShapeDtypeStruct((L, D), F32), jax.ShapeDtypeStruct((L, D), BF16)],
        compiler_params=_cparams(("arbitrary",)),
        name="moe_combine_ln2",
    )(*([yk] * TOP_K), w_t, x1, x1b, sh_gu, sh_down, ln_g.reshape(1, D).astype(F32), ln_b.reshape(1, D).astype(F32))


def _moe_layer(x1, x1b, x1p, logits_t, router_bias, w_gu, w_down, layer, sh_gu, sh_down, ln_g, ln_b, alpha,
               router_tn, combine_tm):
    L, D = x1.shape
    E, K, R = N_EXPERTS, TOP_K, MOE_BLK
    A = K * L
    assert A % R == 0
    nblk = A // R
    e_k, w_k, counts = _router(logits_t, router_bias, router_tn)
    keys = (e_k * L + jnp.arange(L, dtype=jnp.int32)[None, :]) * K + jnp.arange(K, dtype=jnp.int32)[:, None]
    skeys = jnp.sort(keys.reshape(A))
    tok = (skeys // K) % L
    src = tok.reshape(nblk, 1, R)
    dst_sorted = ((skeys % K) * L + tok).reshape(nblk, R)
    ends = jnp.cumsum(counts.reshape(E))
    cuts = jnp.sort(jnp.concatenate([jnp.arange(nblk, dtype=jnp.int32) * R, (ends - counts.reshape(E))]))
    lo = cuts
    hi = jnp.concatenate([cuts[1:], jnp.full((1,), A, jnp.int32)])
    valid = hi > lo
    order = jnp.argsort(jnp.logical_not(valid), stable=True)
    lo, hi = lo[order].astype(jnp.int32), hi[order].astype(jnp.int32)
    n_valid = jnp.sum(valid).astype(jnp.int32).reshape(1)
    blk_w = jnp.minimum(lo // R, nblk - 1).astype(jnp.int32)
    exp_w = jnp.minimum(jnp.sum((ends[None, :] <= lo[:, None]).astype(jnp.int32), axis=1), E - 1).astype(jnp.int32)
    n_out_tiles = A + 2 * R
    n_items = blk_w.shape[0]
    onehot = (blk_w[:, None] == jnp.arange(nblk, dtype=jnp.int32)[None, :]).astype(F32)
    dst_rows = jnp.dot(onehot, dst_sorted.astype(F32), precision=lax.Precision.HIGHEST).astype(jnp.int32)
    pos = blk_w[:, None] * R + jnp.arange(R, dtype=jnp.int32)[None, :]
    spare = (A + (jnp.arange(n_items, dtype=jnp.int32)[:, None] % 2) * R
             + jnp.arange(R, dtype=jnp.int32)[None, :])
    dst = jnp.where((pos >= lo[:, None]) & (pos < hi[:, None]), dst_rows, spare).reshape(n_items, 1, R)
    yk = _experts_fused(x1p, src, dst, (blk_w, exp_w, n_valid), w_gu, w_down, layer, n_out_tiles, R)
    return _combine_stream(yk, w_k.T, x1, x1b, sh_gu, sh_down, ln_g, ln_b, alpha, combine_tm)


def _pick(n, pref):
    t = min(n, pref)
    assert n % t == 0
    return t


def _tile_plan(L):
    return dict(
        proj_rows=_pick(L, 1024),
        attn_q=_pick(L, 256),
        s5_subchunks=_pick(L // S5_SUB, 256),
        mixers_rows=_pick(L, 512),
        outproj_rows=_pick(L, 256),
        router_tokens=_pick(L, 512),
        combine_rows=_pick(L, 256))


def kernel(x, w_in, w_out, mix_norm_g, s5_lambda_re, s5_lambda_im, s5_log_dt, s5_b_re, s5_b_im, s5_c_re, s5_c_im, s5_d, s5_w_glu, conv_w, sgu_ln_g, sgu_ln_b, sgu_w, sgu_b, diff_lq1, diff_lk1, diff_lq2, diff_lk2, diff_subln_g, rel_bias, ln1_g, ln1_b, router_w, router_bias, moe_w_gu, moe_w_down, shared_w_gu, shared_w_down, ln2_g, ln2_b):
    Bt, L, D = x.shape
    assert Bt == 1
    depth = w_in.shape[0]
    alpha = (2 * depth) ** 0.25
    gw = GROUP_W
    t = _tile_plan(L)
    nb = _attn_bias_tables(rel_bias, t['attn_q'])
    xf = x.reshape(L, D)
    xb = xf.astype(BF16)
    for l in range(depth):
        w_in_b = w_in[l].astype(BF16)
        proj_a = _matmul(xb, w_in_b[:, :6 * gw], t['proj_rows'], 3 * gw, F32)
        kmat = _matmul(xb, w_in_b[:, 7 * gw:8 * gw], t['proj_rows'], gw, BF16)
        w_q = (w_in[l][:, 6 * gw:7 * gw] * (DIFF_QK_DIM ** -0.5 * LOG2E)).astype(BF16)
        w_qv_t = jnp.concatenate([w_q, w_in_b[:, 8 * gw:]], axis=1).T
        qvt = _matmul_nt(w_qv_t, xb, t['attn_q'], BF16)
        tabs = _s5_tables(s5_lambda_re[l], s5_lambda_im[l], s5_log_dt[l], s5_b_re[l], s5_b_im[l],
                          s5_c_re[l], s5_c_im[l], t['s5_subchunks'])
        ys5 = _s5_scan(proj_a, tabs, t['s5_subchunks'])
        abc = _mixers(proj_a, ys5, s5_d[l], s5_w_glu[l], conv_w[l], sgu_ln_g[l], sgu_ln_b[l], sgu_w[l], sgu_b[l],
                      mix_norm_g[l], t['mixers_rows'])
        lambda_init = 0.8 - 0.6 * math.exp(-0.3 * l)
        d_out = _diff_attention(qvt, kmat, nb, diff_lq1[l], diff_lk1[l], diff_lq2[l], diff_lk2[l],
                                diff_subln_g[l], lambda_init)
        x1, x1b, x1p, logits_t = _outproj(abc, d_out, xf, w_out[l].astype(BF16), ln1_g[l], ln1_b[l], router_w[l],
                                          alpha, t['outproj_rows'])
        xf, xb = _moe_layer(x1, x1b, x1p, logits_t, router_bias[l], moe_w_gu, moe_w_down, l,
                            shared_w_gu[l].astype(BF16), shared_w_down[l].astype(BF16),
                            ln2_g[l], ln2_b[l], alpha, t['router_tokens'], t['combine_rows'])
    return xf.reshape(Bt, L, D)
```

```python
import functools
import math

import jax
import jax.numpy as jnp
from jax import lax
from jax.experimental import pallas as pl
from jax.experimental.pallas import tpu as pltpu

F32 = jnp.float32
BF16 = jnp.bfloat16

GROUP_W = 512
CHUNK = 64
S5_GROUP_CH = 16
S5_GROUPS = 32
S5_STATE = 64
S5_SUB = 16
S5_KBLOCKS = 4
SGU_BLK = 128
SGU_HEADS = 4
DIFF_HEADS = 4
DIFF_QK_DIM = 64
DIFF_V_DIM = 128
NUM_BUCKETS = 32
MAX_DISTANCE = 128
N_EXPERTS = 64
TOP_K = 8
N_EXPERT_GROUPS = 8
TOPK_GROUPS = 4
ROUTED_SCALE = 2.5
EPS = 1e-5
NEG_INF = -1e30
LOG2E = math.log2(math.e)
ONES_ROWS = 16

VMEM_LIMIT = 56 * 1024 * 1024


def _cparams(sem):
    return pltpu.CompilerParams(dimension_semantics=sem, vmem_limit_bytes=VMEM_LIMIT)


def _rms(x, g):
    return x * lax.rsqrt(jnp.mean(jnp.square(x), -1, keepdims=True) + EPS) * g


def _pack_bf16_pairs(x):
    c = x.shape[1] // 2
    hi = lax.bitcast_convert_type(x[:, :c].astype(BF16).astype(F32), jnp.uint32)
    lo = lax.bitcast_convert_type(x[:, c:].astype(BF16).astype(F32), jnp.uint32)
    return hi | (lo >> 16)


def _unpack_bf16_pairs(u):
    hi = lax.bitcast_convert_type(u & jnp.uint32(0xFFFF0000), F32)
    lo = lax.bitcast_convert_type(u << 16, F32)
    return hi, lo


ROW_TILE = 8
LANES = 128
SUB_ROWS = 128
MOE_BLK = 256


def _store_token_tiles(ref, x):
    p = _pack_bf16_pairs(x)
    n = x.shape[0]
    for c in range(ROW_TILE):
        ref[pl.ds(c, n, stride=ROW_TILE), :] = p[:, c * LANES:(c + 1) * LANES]


def _load_token_tiles(ref, n):
    p = jnp.concatenate([ref[pl.ds(c, n, stride=ROW_TILE), :] for c in range(ROW_TILE)], axis=1)
    return _unpack_bf16_pairs(p)


def _ln(x, g, b):
    mu = jnp.mean(x, -1, keepdims=True)
    var = jnp.mean(jnp.square(x - mu), -1, keepdims=True)
    return (x - mu) * lax.rsqrt(var + EPS) * g + b


def _matmul_kernel(x_ref, w_ref, o_ref):
    o_ref[...] = jnp.dot(x_ref[...], w_ref[...], preferred_element_type=F32).astype(o_ref.dtype)


def _matmul(x, w, tm, tn, out_dtype):
    M, K = x.shape
    N = w.shape[1]
    return pl.pallas_call(
        _matmul_kernel,
        grid=(M // tm, N // tn),
        in_specs=[pl.BlockSpec((tm, K), lambda i, j: (i, 0)),
                  pl.BlockSpec((K, tn), lambda i, j: (0, j))],
        out_specs=pl.BlockSpec((tm, tn), lambda i, j: (i, j)),
        out_shape=jax.ShapeDtypeStruct((M, N), out_dtype),
        compiler_params=_cparams(("parallel", "arbitrary")),
        name="proj_matmul",
    )(x, w)


def _matmul_nt_kernel(w_ref, x_ref, o_ref):
    o_ref[0] = lax.dot_general(w_ref[...], x_ref[...], (((1,), (1,)), ((), ())),
                               preferred_element_type=F32).astype(o_ref.dtype)


def _matmul_nt(w_t, x, tm, out_dtype):
    M, K = x.shape
    N = w_t.shape[0]
    return pl.pallas_call(
        _matmul_nt_kernel,
        grid=(M // tm,),
        in_specs=[pl.BlockSpec((N, K), lambda i: (0, 0)),
                  pl.BlockSpec((tm, K), lambda i: (i, 0))],
        out_specs=pl.BlockSpec((1, N, tm), lambda i: (i, 0, 0)),
        out_shape=jax.ShapeDtypeStruct((M // tm, N, tm), out_dtype),
        compiler_params=_cparams(("parallel",)),
        name="proj_matmul_nt",
    )(w_t, x)


def _s5_tables(lam_re, lam_im, log_dt, b_re, b_im, c_re, c_im, n_rows):
    G, P, H, S = S5_GROUPS, S5_STATE, S5_GROUP_CH, S5_SUB
    hp = lax.Precision.HIGHEST
    dt = jnp.exp(log_dt.astype(F32))[:, None]
    lam = lax.complex(lam_re.astype(F32), lam_im.astype(F32))
    ldt = lam * dt
    lam_bar = jnp.exp(ldt)
    b_bar = ((lam_bar - 1.0) / lam)[..., None] * lax.complex(b_re.astype(F32), b_im.astype(F32))
    c = lax.complex(c_re.astype(F32), c_im.astype(F32))
    tau = jnp.arange(S + 1, dtype=F32)
    pows = jnp.exp(ldt[None] * tau[:, None, None])
    KB, GL = S5_KBLOCKS, S5_GROUPS // S5_KBLOCKS
    eye = jnp.eye(GL, dtype=bool)
    w1 = pows[:S][::-1][:, :, None, :] * jnp.transpose(b_bar, (0, 2, 1))[None]
    w1 = jnp.transpose(w1.reshape(S, KB, GL, H, P), (1, 0, 2, 3, 4))
    w1 = jnp.where(eye[None, None, :, None, :, None], w1[:, :, :, :, None, :], 0.0)
    w1 = w1.reshape(KB, S * GL * H, GL * P)
    w2 = jnp.transpose(c, (0, 2, 1))[:, :, None, :] * jnp.transpose(pows[1:], (1, 2, 0))[..., None]
    w2 = w2.reshape(KB, GL, P, S, H)
    w2 = jnp.where(eye[None, :, None, None, :, None], w2[:, :, :, :, None, :], 0.0)
    w2 = w2.reshape(KB, GL * P, S * GL * H)
    kc = jnp.real(jnp.einsum('ghp,tgp,gpi->tghi', c, pows[:S], b_bar, precision=hp))
    kc = jnp.transpose(kc.reshape(S, KB, GL, H, H), (1, 2, 4, 0, 3))
    kc = jnp.where(eye[None, :, None, None, :, None], kc[:, :, :, :, None, :], 0.0)
    kcat = kc.reshape(KB, GL * H, S * GL * H)

    nstep = max(1, (n_rows - 1).bit_length())
    kk = (S * (2 ** jnp.arange(nstep))).astype(F32)
    lp = jnp.exp(ldt[None] * kk[:, None, None])
    lp = jnp.transpose(lp.reshape(nstep, KB, GL * P), (1, 0, 2))
    lampow = jnp.stack([jnp.real(lp), jnp.imag(lp)], axis=2)
    return dict(
        kcat=kcat.astype(BF16),
        w1re=jnp.real(w1).astype(BF16), w1im=jnp.imag(w1).astype(BF16),
        w2re=jnp.real(w2).astype(BF16), w2im=(-jnp.imag(w2)).astype(BF16),
        lampow=lampow.astype(F32))


def _s5_kernel(u_ref, w1re_ref, w1im_ref, w2re_ref, w2im_ref, kcat_ref, lp_ref, o_ref,
               ucat_ref, yall_ref, carry_ref, *, nstep):
    t = pl.program_id(1)
    S = S5_SUB
    R = ucat_ref.shape[0]
    W = u_ref.shape[1]

    @pl.when(t == 0)
    def _():
        carry_ref[...] = jnp.zeros_like(carry_ref)

    for j in range(S):
        ucat_ref[:, j * W:(j + 1) * W] = u_ref[pl.ds(j, R, stride=S), :].astype(BF16)
    ucat = ucat_ref[...]
    xre = jnp.dot(ucat, w1re_ref[0], preferred_element_type=F32)
    xim = jnp.dot(ucat, w1im_ref[0], preferred_element_type=F32)
    row = lax.broadcasted_iota(jnp.int32, xre.shape, 0)
    cre, cim = carry_ref[0], carry_ref[1]
    lr, li = lp_ref[0, 0, 0:1, :], lp_ref[0, 0, 1:2, :]
    xre = xre + jnp.where(row == 0, lr * cre - li * cim, 0.0)
    xim = xim + jnp.where(row == 0, lr * cim + li * cre, 0.0)
    for k in range(nstep):
        sh = 1 << k
        pre = pltpu.roll(xre, sh, 0)
        pim = pltpu.roll(xim, sh, 0)
        lr, li = lp_ref[0, k, 0:1, :], lp_ref[0, k, 1:2, :]
        keep = row >= sh
        xre, xim = (xre + jnp.where(keep, lr * pre - li * pim, 0.0),
                    xim + jnp.where(keep, lr * pim + li * pre, 0.0))
    carry_ref[0] = xre[R - 1:R, :]
    carry_ref[1] = xim[R - 1:R, :]
    sre = jnp.where(row >= 1, pltpu.roll(xre, 1, 0), cre).astype(BF16)
    sim = jnp.where(row >= 1, pltpu.roll(xim, 1, 0), cim).astype(BF16)
    yall_ref[...] = (jnp.dot(sre, w2re_ref[0], preferred_element_type=F32)
                     + jnp.dot(sim, w2im_ref[0], preferred_element_type=F32))
    for j in range(S):
        yall_ref[:, j * W:] += jnp.dot(ucat_ref[:, j * W:(j + 1) * W], kcat_ref[0, :, :(S - j) * W],
                                       preferred_element_type=F32)
    for j in range(S):
        o_ref[pl.ds(j, R, stride=S), :] = yall_ref[:, j * W:(j + 1) * W]


def _s5_scan(proj_a, tabs, rt):
    L = proj_a.shape[0]
    S, KB = S5_SUB, S5_KBLOCKS
    W = GROUP_W // KB
    nstep = tabs['lampow'].shape[1]
    P2 = tabs['lampow'].shape[3]
    rows = rt * S
    kb3 = lambda a: pl.BlockSpec((1,) + a.shape[1:], lambda k, t: (k, 0, 0))
    return pl.pallas_call(
        functools.partial(_s5_kernel, nstep=nstep),
        grid=(KB, L // rows),
        in_specs=[pl.BlockSpec((rows, W), lambda k, t: (t, k)),
                  kb3(tabs['w1re']), kb3(tabs['w1im']), kb3(tabs['w2re']), kb3(tabs['w2im']), kb3(tabs['kcat']),
                  pl.BlockSpec((1, nstep, 2, P2), lambda k, t: (k, 0, 0, 0))],
        out_specs=pl.BlockSpec((rows, W), lambda k, t: (t, k)),
        out_shape=jax.ShapeDtypeStruct((L, GROUP_W), F32),
        scratch_shapes=[pltpu.VMEM((rt, S * W), BF16), pltpu.VMEM((rt, S * W), F32), pltpu.VMEM((2, 1, P2), F32)],
        compiler_params=_cparams(("parallel", "arbitrary")),
        name="s5_scan",
    )(proj_a, tabs['w1re'], tabs['w1im'], tabs['w2re'], tabs['w2im'], tabs['kcat'], tabs['lampow'])


def _mixers_kernel(s5u_ref, cb_ref, cc_ref, ch_ref, su_ref, sv_ref, cch_ref, chh_ref, ys_ref,
                   d_ref, wglu_ref, cw_ref, lng_ref, lnb_ref, ws_ref, bs_ref, g_ref, o_ref):
    i = pl.program_id(0)
    tm = o_ref.shape[0]
    gw = GROUP_W
    y = ys_ref[...] + d_ref[...] * s5u_ref[...]
    y = jax.nn.gelu(y)
    y = y * jax.nn.sigmoid(jnp.dot(y.astype(BF16), wglu_ref[...], preferred_element_type=F32))
    o_ref[:, 0:gw] = _rms(y, g_ref[0:1, :]).astype(o_ref.dtype)
    z = cc_ref[...] * ch_ref[...]
    zh = jnp.where(i > 0, cch_ref[...] * chh_ref[...], 0.0)
    row = lax.broadcasted_iota(jnp.int32, z.shape, 0)
    z1 = jnp.where(row == 0, zh[7:8, :], pltpu.roll(z, 1, 0))
    z2 = jnp.where(row == 0, zh[6:7, :], jnp.where(row == 1, zh[7:8, :], pltpu.roll(z, 2, 0)))
    conv = cw_ref[0:1, :] * z2 + cw_ref[1:2, :] * z1 + cw_ref[2:3, :] * z
    o_ref[:, gw:2 * gw] = _rms(cb_ref[...] * conv, g_ref[1:2, :]).astype(o_ref.dtype)
    uu = jax.nn.gelu(su_ref[...])
    vv = _ln(jax.nn.gelu(sv_ref[...]), lng_ref[...], lnb_ref[...]).astype(BF16)
    pi = lax.broadcasted_iota(jnp.int32, (SGU_BLK, SGU_BLK), 0)
    pj = lax.broadcasted_iota(jnp.int32, (SGU_BLK, SGU_BLK), 1)
    causal = (pj // CHUNK) <= (pi // CHUNK)
    hd = gw // SGU_HEADS
    ws = [jnp.where(causal, ws_ref[h], 0.0).astype(BF16) for h in range(SGU_HEADS)]
    blocks = []
    for n in range(tm // SGU_BLK):
        vb = vv[n * SGU_BLK:(n + 1) * SGU_BLK, :]
        blocks.append(jnp.concatenate(
            [jnp.dot(ws[h], vb[:, h * hd:(h + 1) * hd], preferred_element_type=F32) for h in range(SGU_HEADS)],
            axis=1) + bs_ref[...])
    mixed = jnp.concatenate(blocks, axis=0)
    o_ref[:, 2 * gw:3 * gw] = _rms(uu * mixed, g_ref[2:3, :]).astype(o_ref.dtype)


def _mixers(proj_a, ys5, s5_d, w_glu, conv_w, ln_g, ln_b, sgu_w, sgu_b, mix_g, tm):
    L = proj_a.shape[0]
    gw = GROUP_W
    hb = tm // 8
    col = lambda c: pl.BlockSpec((tm, gw), lambda i, c=c: (i, c))
    halo = lambda c: pl.BlockSpec((8, gw), lambda i, c=c: (jnp.maximum(i * hb - 1, 0), c))
    full = lambda a: pl.BlockSpec(a.shape, lambda i: (0,) * a.ndim)
    hd = gw // SGU_HEADS
    bs_full = jnp.repeat(sgu_b.astype(F32).T, hd, axis=1)
    consts = [s5_d.reshape(1, gw).astype(F32), w_glu.astype(BF16), conv_w.astype(F32),
              ln_g.reshape(1, gw).astype(F32), ln_b.reshape(1, gw).astype(F32), sgu_w.astype(F32),
              bs_full, mix_g.reshape(3, gw).astype(F32)]
    return pl.pallas_call(
        _mixers_kernel,
        grid=(L // tm,),
        in_specs=[col(0), col(1), col(2), col(3), col(4), col(5), halo(2), halo(3),
                  pl.BlockSpec((tm, gw), lambda i: (i, 0))] + [full(a) for a in consts],
        out_specs=pl.BlockSpec((tm, 3 * gw), lambda i: (i, 0)),
        out_shape=jax.ShapeDtypeStruct((L, 3 * gw), BF16),
        compiler_params=_cparams(("parallel",)),
        name="row_mixers",
    )(proj_a, proj_a, proj_a, proj_a, proj_a, proj_a, proj_a, proj_a, ys5, *consts)


def _t5_bucket(rel):
    half = NUM_BUCKETS // 2
    ret = jnp.where(rel > 0, half, 0)
    n = jnp.abs(rel)
    max_exact = half // 2
    large = max_exact + (jnp.log(jnp.maximum(n, 1).astype(F32) / max_exact)
                         / math.log(MAX_DISTANCE / max_exact) * (half - max_exact)).astype(jnp.int32)
    large = jnp.minimum(large, half - 1)
    return ret + jnp.where(n < max_exact, n, large)


def _attn_bias_tables(rel_bias, tq):
    assert tq >= MAX_DISTANCE
    rb = rel_bias.astype(F32)
    far = rb[NUM_BUCKETS // 2 - 1]
    buckets = jnp.arange(NUM_BUCKETS)[:, None]

    def bias_of(rel):
        onehot = _t5_bucket(rel)[:, :, None, None] == buckets
        return jnp.sum(jnp.where(onehot, rb, 0.0), axis=2) - far

    kj = jnp.arange(tq)[:, None]
    qi = jnp.arange(tq)[None, :]
    diag = jnp.where(((kj // CHUNK) <= (qi // CHUNK))[..., None], bias_of(kj - qi), NEG_INF)
    prev = bias_of(kj - tq - qi)
    tabs = jnp.stack([jnp.transpose(diag, (2, 0, 1)), jnp.transpose(prev, (2, 0, 1))], axis=1)
    tabs = jnp.where(tabs > 0.5 * NEG_INF, tabs * LOG2E, NEG_INF)
    return jnp.concatenate([tabs, tabs], axis=3)


def _attn_kernel(qt_ref, k_ref, vt_ref, nb_ref, lq1_ref, lk1_ref, lq2_ref, lk2_ref, g_ref, o_ref,
                 qq_s, sa_s, sb_s, p_s, m_s, a_s, acc_s, *, lambda_init):
    i = pl.program_id(1)
    tq = qt_ref.shape[2]
    dq = DIFF_QK_DIM
    qt = qt_ref[0]
    feat = lax.broadcasted_iota(jnp.int32, qt.shape, 0)
    zero = jnp.zeros_like(qt)
    qq = jnp.concatenate([jnp.where(feat < dq, qt, zero), jnp.where(feat >= dq, qt, zero)], axis=1)

    qq_s[...] = qq
    m_s[...] = jnp.full(m_s.shape, -jnp.inf, F32)
    acc_s[...] = jnp.zeros(acc_s.shape, F32)

    def scores(s_ref, j, nblk):
        rows = nblk * tq
        kb = k_ref[pl.ds(pl.multiple_of(j * tq, tq), rows), :]
        s_ref[:rows, :] = jnp.dot(kb, qq_s[...], preferred_element_type=F32)

    def absorb(s_ref, j, nblk, bias=None):
        rows = nblk * tq
        vt = jnp.concatenate([vt_ref[j + b] for b in range(nblk)], axis=1)
        vt = jnp.concatenate([vt, jnp.ones((ONES_ROWS, rows), BF16)], axis=0)
        for br in range(2):
            for c in range(br * tq // 128, (br + 1) * tq // 128):
                cs = slice(c * 128, (c + 1) * 128)
                s = s_ref[:rows, cs]
                if bias is not None:
                    s = s + bias(cs)
                m_old = m_s[:, cs]
                m_new = jnp.maximum(m_old, jnp.max(s, axis=0, keepdims=True))
                m_s[:, cs] = m_new
                a_s[:, cs] = jnp.exp2(m_old - m_new)
                p_s[:rows, cs] = jnp.exp2((s - m_new).astype(BF16))
            bs = slice(br * tq, (br + 1) * tq)
            acc_s[:, bs] = a_s[:, bs] * acc_s[:, bs] + jnp.dot(vt, p_s[:rows, bs], preferred_element_type=F32)

    scores(sa_s, i, 1)
    absorb(sa_s, i, 1, lambda cs: nb_ref[0, 0, :, cs])
    jp = jnp.maximum(i - 1, 0)
    first = jnp.where(i > 0, 0.0, NEG_INF)
    scores(sa_s, jp, 1)
    absorb(sa_s, jp, 1, lambda cs: nb_ref[0, 1, :, cs] + first)
    n_far = jnp.maximum(i - 1, 0)
    n_single = n_far % 2
    n_head = n_far % 4

    @pl.when(n_single == 1)
    def _():
        scores(sb_s, 0, 1)
        absorb(sb_s, 0, 1)

    @pl.when(n_head >= 2)
    def _():
        scores(sb_s, n_single, 2)
        absorb(sb_s, n_single, 2)

    scores(sa_s, n_head, 2)

    def quad(qd, c):
        j0 = n_head + 4 * qd
        scores(sb_s, j0 + 2, 2)
        absorb(sa_s, j0, 2)
        scores(sa_s, j0 + 4, 2)
        absorb(sb_s, j0 + 2, 2)
        return c

    lax.fori_loop(0, n_far // 4, quad, 0)
    o = acc_s[:DIFF_V_DIM, :] / acc_s[DIFF_V_DIM:DIFF_V_DIM + 1, :]
    lam = (jnp.exp(jnp.sum(lq1_ref[...] * lk1_ref[...], keepdims=True))
           - jnp.exp(jnp.sum(lq2_ref[...] * lk2_ref[...], keepdims=True)) + lambda_init)
    out = o[:, :tq] - lam * o[:, tq:]
    out = out * lax.rsqrt(jnp.mean(jnp.square(out), axis=0, keepdims=True) + EPS) * g_ref[...]
    o_ref[...] = (out * (1.0 - lambda_init)).T.astype(o_ref.dtype)


def _diff_attention(qvt, kmat, nb, lq1, lk1, lq2, lk2, subln_g, lambda_init):
    nq, _, tq = qvt.shape
    assert nq >= 2
    L = kmat.shape[0]
    H, dv = DIFF_HEADS, DIFF_V_DIM
    vec = lambda a: a.reshape(1, -1).astype(F32)
    small = lambda n: pl.BlockSpec((1, n), lambda h, i: (0, 0))
    return pl.pallas_call(
        functools.partial(_attn_kernel, lambda_init=lambda_init),
        grid=(H, nq),
        in_specs=[pl.BlockSpec((1, dv, tq), lambda h, i: (i, h, 0)),
                  pl.BlockSpec((L, dv), lambda h, i: (0, h)),
                  pl.BlockSpec((nq, dv, tq), lambda h, i: (0, H + h, 0)),
                  pl.BlockSpec((1, 2, tq, 2 * tq), lambda h, i: (h, 0, 0, 0)),
                  small(DIFF_QK_DIM), small(DIFF_QK_DIM), small(DIFF_QK_DIM), small(DIFF_QK_DIM),
                  pl.BlockSpec((dv, 1), lambda h, i: (0, 0))],
        out_specs=pl.BlockSpec((tq, dv), lambda h, i: (i, h)),
        out_shape=jax.ShapeDtypeStruct((L, H * dv), BF16),
        scratch_shapes=[pltpu.VMEM((dv, 2 * tq), BF16),
                        pltpu.VMEM((2 * tq, 2 * tq), F32), pltpu.VMEM((2 * tq, 2 * tq), F32),
                        pltpu.VMEM((2 * tq, 2 * tq), BF16),
                        pltpu.VMEM((1, 2 * tq), F32), pltpu.VMEM((1, 2 * tq), F32),
                        pltpu.VMEM((dv + ONES_ROWS, 2 * tq), F32)],
        compiler_params=_cparams(("parallel", "arbitrary")),
        name="diff_attention",
    )(qvt, kmat, qvt, nb, vec(lq1), vec(lk1), vec(lq2), vec(lk2), subln_g.reshape(dv, 1).astype(F32))


def _outproj_kernel(abc_ref, d_ref, x_ref, wa_ref, wd_ref, g_ref, b_ref, rhi_ref, rlo_ref,
                    x1_ref, x1b_ref, x1p_ref, lg_ref, *, alpha):
    nt = (((1,), (1,)), ((), ()))
    tm = x_ref.shape[0]
    sub = min(tm, SUB_ROWS)
    for r0 in range(0, tm, sub):
        rows = pl.ds(r0, sub)
        mix = (jnp.dot(abc_ref[rows, :], wa_ref[...], preferred_element_type=F32)
               + jnp.dot(d_ref[rows, :], wd_ref[...], preferred_element_type=F32))
        x1 = _ln(alpha * x_ref[rows, :] + mix, g_ref[...], b_ref[...])
        x1_ref[rows, :] = x1
        hi = x1.astype(BF16)
        x1b_ref[rows, :] = hi
        _store_token_tiles(x1p_ref.at[pl.ds(r0 * ROW_TILE, sub * ROW_TILE), :], x1)
        lo = (x1 - hi.astype(F32)).astype(BF16)
        lg_ref[:, r0:r0 + sub] = (lax.dot_general(rhi_ref[...], hi, nt, preferred_element_type=F32)
                                  + lax.dot_general(rhi_ref[...], lo, nt, preferred_element_type=F32)
                                  + lax.dot_general(rlo_ref[...], hi, nt, preferred_element_type=F32))


def _outproj(abc, d_out, x, w_out, ln_g, ln_b, router_w, alpha, tm):
    L, D = x.shape
    E = router_w.shape[1]
    ka = abc.shape[1]
    kd = d_out.shape[1]
    rwt = router_w.astype(F32).T
    rhi = rwt.astype(BF16)
    rlo = (rwt - rhi.astype(F32)).astype(BF16)
    full = lambda shape: pl.BlockSpec(shape, lambda i: (0,) * len(shape))
    return pl.pallas_call(
        functools.partial(_outproj_kernel, alpha=alpha),
        grid=(L // tm,),
        in_specs=[pl.BlockSpec((tm, ka), lambda i: (i, 0)),
                  pl.BlockSpec((tm, kd), lambda i: (i, 0)),
                  pl.BlockSpec((tm, D), lambda i: (i, 0)),
                  pl.BlockSpec((ka, D), lambda i: (0, 0)),
                  pl.BlockSpec((kd, D), lambda i: (ka // kd, 0)),
                  full((1, D)), full((1, D)), full((E, D)), full((E, D))],
        out_specs=[pl.BlockSpec((tm, D), lambda i: (i, 0)),
                   pl.BlockSpec((tm, D), lambda i: (i, 0)),
                   pl.BlockSpec((tm * ROW_TILE, LANES), lambda i: (i, 0)),
                   pl.BlockSpec((E, tm), lambda i: (0, i))],
        out_shape=[jax.ShapeDtypeStruct((L, D), F32), jax.ShapeDtypeStruct((L, D), BF16),
                   jax.ShapeDtypeStruct((L * ROW_TILE, LANES), jnp.uint32), jax.ShapeDtypeStruct((E, L), F32)],
        compiler_params=_cparams(("parallel",)),
        name="outproj_ln1",
    )(abc, d_out, x, w_out, w_out, ln_g.reshape(1, D).astype(F32), ln_b.reshape(1, D).astype(F32), rhi, rlo)


def _router_kernel(lg_ref, bias_ref, e_ref, w_ref, cnt_ref, carry_ref):
    i = pl.program_id(0)
    E, tn = lg_ref.shape
    ng = N_EXPERT_GROUPS
    gs_ = E // ng

    @pl.when(i == 0)
    def _():
        carry_ref[...] = jnp.zeros_like(carry_ref)

    s = jax.nn.sigmoid(lg_ref[...])
    sel = s + bias_ref[...]
    midx = lax.broadcasted_iota(jnp.int32, (gs_, tn), 0).astype(F32)
    rows, gscore = [], []
    for g in range(ng):
        rg = sel[g * gs_:(g + 1) * gs_, :]
        m1 = jnp.max(rg, axis=0, keepdims=True)
        first = jnp.min(jnp.where(rg == m1, midx, float(gs_)), axis=0, keepdims=True)
        m2 = jnp.max(jnp.where(midx == first, -jnp.inf, rg), axis=0, keepdims=True)
        rows.append(rg)
        gscore.append(m1 + m2)
    vals = []
    for g in range(ng):
        rank = jnp.zeros((1, tn), F32)
        for o in range(ng):
            if o != g:
                beats = (gscore[o] >= gscore[g]) if o < g else (gscore[o] > gscore[g])
                rank = rank + jnp.where(beats, 1.0, 0.0)
        vals.append(jnp.where(rank < TOPK_GROUPS, rows[g], -jnp.inf))
    val = jnp.concatenate(vals, axis=0)
    eidx = lax.broadcasted_iota(jnp.int32, val.shape, 0)
    erank = jnp.zeros(val.shape, F32)
    for e in range(E):
        other = val[e:e + 1, :]
        erank = erank + jnp.where(eidx > e, jnp.where(other >= val, 1.0, 0.0), jnp.where(other > val, 1.0, 0.0))
    chosen = erank < TOP_K
    wsel = jnp.where(chosen, s, 0.0)
    wn = wsel / (jnp.sum(wsel, axis=0, keepdims=True) + 1e-20) * ROUTED_SCALE
    total = carry_ref[...] + jnp.sum(jnp.where(chosen, 1.0, 0.0), axis=1, keepdims=True)
    carry_ref[...] = total
    cnt_ref[...] = total.astype(jnp.int32)
    eidf = eidx.astype(F32)
    cand = jnp.where(chosen, eidf, float(E))
    for k in range(TOP_K):
        ek = jnp.min(cand, axis=0, keepdims=True)
        hit = cand == ek
        e_ref[k:k + 1, :] = ek.astype(jnp.int32)
        w_ref[k:k + 1, :] = jnp.sum(jnp.where(hit, wn, 0.0), axis=0, keepdims=True)
        cand = jnp.where(hit, float(E), cand)


def _router(logits_t, router_bias, tn):
    E, L = logits_t.shape
    slot = lambda dt: jax.ShapeDtypeStruct((TOP_K, L), dt)
    return pl.pallas_call(
        _router_kernel,
        grid=(L // tn,),
        in_specs=[pl.BlockSpec((E, tn), lambda i: (0, i)),
                  pl.BlockSpec((E, 1), lambda i: (0, 0))],
        out_specs=[pl.BlockSpec((TOP_K, tn), lambda i: (0, i)),
                   pl.BlockSpec((TOP_K, tn), lambda i: (0, i)),
                   pl.BlockSpec((E, 1), lambda i: (0, 0))],
        out_shape=[slot(jnp.int32), slot(F32), jax.ShapeDtypeStruct((E, 1), jnp.int32)],
        scratch_shapes=[pltpu.VMEM((E, 1), F32)],
        compiler_params=_cparams(("arbitrary",)),
        name="router_topk",
    )(logits_t, router_bias.reshape(E, 1).astype(F32))


def _experts_fused_kernel(blk_ref, exp_ref, nv_ref,
                          src_ref, srcn_ref, dst_ref, x_hbm, wgu_ref, wd_ref, y_hbm,
                          xbuf, ybuf, wgu_b, wd_b, gsem, ssem):
    w = pl.program_id(0)
    nv = nv_ref[0]
    R = xbuf.shape[1] // ROW_TILE
    par = w % 2

    def tile(buf, r):
        return buf.at[pl.ds(r * ROW_TILE, ROW_TILE), :]

    def gather(s_ref, p):
        for r in range(R):
            pltpu.make_async_copy(x_hbm.at[s_ref[0, 0, r]], tile(xbuf.at[p], r), gsem.at[p]).start()

    def gather_wait(p):
        pltpu.make_async_copy(xbuf.at[1 - p], xbuf.at[p], gsem.at[p]).wait()

    def scatter_wait(p):
        pltpu.make_async_copy(ybuf.at[p], ybuf.at[1 - p], ssem.at[p]).wait()

    @pl.when(w == 0)
    def _():
        gather(src_ref, 0)

    @pl.when((w < nv) & ((w == 0) | (exp_ref[w] != exp_ref[jnp.maximum(w - 1, 0)])))
    def _():
        wgu_b[...] = wgu_ref[0, 0].astype(BF16)
        wd_b[...] = wd_ref[0, 0].astype(BF16)

    @pl.when((w >= 2) & (w < nv))
    def _():
        scatter_wait(par)

    @pl.when(w < nv)
    def _():
        gather_wait(par)
        gather(srcn_ref, 1 - par)
        de = wd_b.shape[0]
        xa, xb = _load_token_tiles(xbuf.at[par], R)
        x = jnp.concatenate([xa.astype(BF16), xb.astype(BF16)], axis=1)
        h = jnp.dot(x, wgu_b[...], preferred_element_type=F32)
        a = jax.nn.silu(h[:, :de]) * h[:, de:]
        _store_token_tiles(ybuf.at[par], jnp.dot(a.astype(BF16), wd_b[...], preferred_element_type=F32))
        for r in range(R):
            pltpu.make_async_copy(tile(ybuf.at[par], r), y_hbm.at[dst_ref[0, 0, r]], ssem.at[par]).start()

    @pl.when(w == nv - 1)
    def _():
        scatter_wait(par)
        gather_wait(1 - par)

    @pl.when((w == nv - 1) & (w >= 1))
    def _():
        scatter_wait(1 - par)


def _experts_fused(x1p, src, dst, items, w_gu, w_down, layer, n_out_tiles, blk):
    blk_w, exp_w, n_valid = items
    n_items = blk_w.shape[0]
    D, de2 = w_gu.shape[2], w_gu.shape[3]
    de = w_down.shape[2]
    cur = lambda w, nv: jnp.minimum(w, nv[0] - 1)
    return pl.pallas_call(
        _experts_fused_kernel,
        grid_spec=pltpu.PrefetchScalarGridSpec(
            num_scalar_prefetch=3,
            grid=(n_items,),
            in_specs=[pl.BlockSpec((1, 1, blk), lambda w, b, e, nv: (b[cur(w, nv)], 0, 0),
                                   memory_space=pltpu.SMEM),
                      pl.BlockSpec((1, 1, blk), lambda w, b, e, nv: (b[cur(w + 1, nv)], 0, 0),
                                   memory_space=pltpu.SMEM),
                      pl.BlockSpec((1, 1, blk), lambda w, b, e, nv: (cur(w, nv), 0, 0),
                                   memory_space=pltpu.SMEM),
                      pl.BlockSpec(memory_space=pl.ANY),
                      pl.BlockSpec((1, 1, D, de2), lambda w, b, e, nv: (layer, e[cur(w, nv)], 0, 0)),
                      pl.BlockSpec((1, 1, de, D), lambda w, b, e, nv: (layer, e[cur(w, nv)], 0, 0))],
            out_specs=pl.BlockSpec(memory_space=pl.ANY),
            scratch_shapes=[pltpu.VMEM((2, blk * ROW_TILE, LANES), jnp.uint32),
                            pltpu.VMEM((2, blk * ROW_TILE, LANES), jnp.uint32),
                            pltpu.VMEM((D, de2), BF16), pltpu.VMEM((de, D), BF16),
                            pltpu.SemaphoreType.DMA((2,)), pltpu.SemaphoreType.DMA((2,))]),
        out_shape=jax.ShapeDtypeStruct((n_out_tiles, ROW_TILE, LANES), jnp.uint32),
        compiler_params=_cparams(("arbitrary",)),
        name="moe_experts_fused",
    )(blk_w, exp_w, n_valid, src, src, dst, x1p.reshape(-1, ROW_TILE, LANES), w_gu, w_down).reshape(-1, LANES)


def _combine_stream_kernel(*refs, alpha):
    y_refs = refs[:TOP_K]
    w_ref, x1_ref, x1b_ref, sgu_ref, sdn_ref, g_ref, b_ref, x2_ref, x2b_ref = refs[TOP_K:]
    tm = x1_ref.shape[0]
    de = sdn_ref.shape[0]
    h = jnp.dot(x1b_ref[...], sgu_ref[...], preferred_element_type=F32)
    a = jax.nn.silu(h[:, :de]) * h[:, de:]
    ffn = jnp.dot(a.astype(BF16), sdn_ref[...], preferred_element_type=F32)
    half = ffn.shape[1] // 2
    fa, fb = ffn[:, :half], ffn[:, half:]
    for k in range(TOP_K):
        ya, yb = _load_token_tiles(y_refs[k], tm)
        w = w_ref[:, k:k + 1]
        fa = fa + ya * w
        fb = fb + yb * w
    ffn = jnp.concatenate([fa, fb], axis=1)
    x2 = _ln(alpha * x1_ref[...] + ffn, g_ref[...], b_ref[...])
    x2_ref[...] = x2
    x2b_ref[...] = x2.astype(BF16)


def _combine_stream(yk, w_t, x1, x1b, sh_gu, sh_down, ln_g, ln_b, alpha, tm):
    L, D = x1.shape
    n = L // tm
    full = lambda shape: pl.BlockSpec(shape, lambda i: (0,) * len(shape))
    y_specs = [pl.BlockSpec((tm * ROW_TILE, LANES), lambda i, k=k: (k * n + i, 0)) for k in range(TOP_K)]
    return pl.pallas_call(
        functools.partial(_combine_stream_kernel, alpha=alpha),
        grid=(n,),
        in_specs=y_specs + [pl.BlockSpec((tm, TOP_K), lambda i: (i, 0)),
                            pl.BlockSpec((tm, D), lambda i: (i, 0)),
                            pl.BlockSpec((tm, D), lambda i: (i, 0)),
                            full(sh_gu.shape), full(sh_down.shape), full((1, D)), full((1, D))],
        out_specs=[pl.BlockSpec((tm, D), lambda i: (i, 0)), pl.BlockSpec((tm, D), lambda i: (i, 0))],
        out_shape=[jax.ShapeDtypeStruct((L, D), F32), jax.ShapeDtypeStruct((L, D), BF16)],
        compiler_params=_cparams(("arbitrary",)),
        name="moe_combine_ln2",
    )(*([yk] * TOP_K), w_t, x1, x1b, sh_gu, sh_down, ln_g.reshape(1, D).astype(F32), ln_b.reshape(1, D).astype(F32))


def _moe_layer(x1, x1b, x1p, logits_t, router_bias, w_gu, w_down, layer, sh_gu, sh_down, ln_g, ln_b, alpha,
               router_tn, combine_tm):
    L, D = x1.shape
    E, K, R = N_EXPERTS, TOP_K, MOE_BLK
    A = K * L
    assert A % R == 0
    nblk = A // R
    e_k, w_k, counts = _router(logits_t, router_bias, router_tn)
    keys = (e_k * L + jnp.arange(L, dtype=jnp.int32)[None, :]) * K + jnp.arange(K, dtype=jnp.int32)[:, None]
    skeys = jnp.sort(keys.reshape(A))
    tok = (skeys // K) % L
    src = tok.reshape(nblk, 1, R)
    dst_sorted = ((skeys % K) * L + tok).reshape(nblk, R)
    ends = jnp.cumsum(counts.reshape(E))
    cuts = jnp.sort(jnp.concatenate([jnp.arange(nblk, dtype=jnp.int32) * R, (ends - counts.reshape(E))]))
    lo = cuts
    hi = jnp.concatenate([cuts[1:], jnp.full((1,), A, jnp.int32)])
    valid = hi > lo
    order = jnp.argsort(jnp.logical_not(valid), stable=True)
    lo, hi = lo[order].astype(jnp.int32), hi[order].astype(jnp.int32)
    n_valid = jnp.sum(valid).astype(jnp.int32).reshape(1)
    blk_w = jnp.minimum(lo // R, nblk - 1).astype(jnp.int32)
    exp_w = jnp.minimum(jnp.sum((ends[None, :] <= lo[:, None]).astype(jnp.int32), axis=1), E - 1).astype(jnp.int32)
    n_out_tiles = A + 2 * R
    n_items = blk_w.shape[0]
    onehot = (blk_w[:, None] == jnp.arange(nblk, dtype=jnp.int32)[None, :]).astype(F32)
    dst_rows = jnp.dot(onehot, dst_sorted.astype(F32), precision=lax.Precision.HIGHEST).astype(jnp.int32)
    pos = blk_w[:, None] * R + jnp.arange(R, dtype=jnp.int32)[None, :]
    spare = (A + (jnp.arange(n_items, dtype=jnp.int32)[:, None] % 2) * R
             + jnp.arange(R, dtype=jnp.int32)[None, :])
    dst = jnp.where((pos >= lo[:, None]) & (pos < hi[:, None]), dst_rows, spare).reshape(n_items, 1, R)
    yk = _experts_fused(x1p, src, dst, (blk_w, exp_w, n_valid), w_gu, w_down, layer, n_out_tiles, R)
    return _combine_stream(yk, w_k.T, x1, x1b, sh_gu, sh_down, ln_g, ln_b, alpha, combine_tm)


def _pick(n, pref):
    t = min(n, pref)
    assert n % t == 0
    return t


def _tile_plan(L):
    return dict(
        proj_rows=_pick(L, 1024),
        attn_q=_pick(L, 256),
        s5_subchunks=_pick(L // S5_SUB, 256),
        mixers_rows=_pick(L, 512),
        outproj_rows=_pick(L, 256),
        router_tokens=_pick(L, 512),
        combine_rows=_pick(L, 256))


def kernel(x, w_in, w_out, mix_norm_g, s5_lambda_re, s5_lambda_im, s5_log_dt, s5_b_re, s5_b_im, s5_c_re, s5_c_im, s5_d, s5_w_glu, conv_w, sgu_ln_g, sgu_ln_b, sgu_w, sgu_b, diff_lq1, diff_lk1, diff_lq2, diff_lk2, diff_subln_g, rel_bias, ln1_g, ln1_b, router_w, router_bias, moe_w_gu, moe_w_down, shared_w_gu, shared_w_down, ln2_g, ln2_b):
    Bt, L, D = x.shape
    assert Bt == 1
    depth = w_in.shape[0]
    alpha = (2 * depth) ** 0.25
    gw = GROUP_W
    t = _tile_plan(L)
    nb = _attn_bias_tables(rel_bias, t['attn_q'])
    xf = x.reshape(L, D)
    xb = xf.astype(BF16)
    for l in range(depth):
        w_in_b = w_in[l].astype(BF16)
        proj_a = _matmul(xb, w_in_b[:, :6 * gw], t['proj_rows'], 3 * gw, F32)
        kmat = _matmul(xb, w_in_b[:, 7 * gw:8 * gw], t['proj_rows'], gw, BF16)
        w_q = (w_in[l][:, 6 * gw:7 * gw] * (DIFF_QK_DIM ** -0.5 * LOG2E)).astype(BF16)
        w_qv_t = jnp.concatenate([w_q, w_in_b[:, 8 * gw:]], axis=1).T
        qvt = _matmul_nt(w_qv_t, xb, t['attn_q'], BF16)
        tabs = _s5_tables(s5_lambda_re[l], s5_lambda_im[l], s5_log_dt[l], s5_b_re[l], s5_b_im[l],
                          s5_c_re[l], s5_c_im[l], t['s5_subchunks'])
        ys5 = _s5_scan(proj_a, tabs, t['s5_subchunks'])
        abc = _mixers(proj_a, ys5, s5_d[l], s5_w_glu[l], conv_w[l], sgu_ln_g[l], sgu_ln_b[l], sgu_w[l], sgu_b[l],
                      mix_norm_g[l], t['mixers_rows'])
        lambda_init = 0.8 - 0.6 * math.exp(-0.3 * l)
        d_out = _diff_attention(qvt, kmat, nb, diff_lq1[l], diff_lk1[l], diff_lq2[l], diff_lk2[l],
                                diff_subln_g[l], lambda_init)
        x1, x1b, x1p, logits_t = _outproj(abc, d_out, xf, w_out[l].astype(BF16), ln1_g[l], ln1_b[l], router_w[l],
                                          alpha, t['outproj_rows'])
        xf, xb = _moe_layer(x1, x1b, x1p, logits_t, router_bias[l], moe_w_gu, moe_w_down, l,
                            shared_w_gu[l].astype(BF16), shared_w_down[l].astype(BF16),
                            ln2_g[l], ln2_b[l], alpha, t['router_tokens'], t['combine_rows'])
    return xf.reshape(Bt, L, D)
```

```python
import functools
import math

import jax
import jax.numpy as jnp
from jax import lax
from jax.experimental import pallas as pl
from jax.experimental.pallas import tpu as pltpu

F32 = jnp.float32
BF16 = jnp.bfloat16

GROUP_W = 512
CHUNK = 64
S5_GROUP_CH = 16
S5_GROUPS = 32
S5_STATE = 64
S5_SUB = 8
S5_KBLOCKS = 4
SGU_BLK = 128
SGU_HEADS = 4
DIFF_HEADS = 4
DIFF_QK_DIM = 64
DIFF_V_DIM = 128
NUM_BUCKETS = 32
MAX_DISTANCE = 128
N_EXPERTS = 64
TOP_K = 8
N_EXPERT_GROUPS = 8
TOPK_GROUPS = 4
ROUTED_SCALE = 2.5
EPS = 1e-5
NEG_INF = -1e30
LOG2E = math.log2(math.e)
ONES_ROWS = 16

VMEM_LIMIT = 56 * 1024 * 1024


def _cparams(sem):
    return pltpu.CompilerParams(dimension_semantics=sem, vmem_limit_bytes=VMEM_LIMIT)


def _rms(x, g):
    return x * lax.rsqrt(jnp.mean(jnp.square(x), -1, keepdims=True) + EPS) * g


def _pack_bf16_pairs(x):
    c = x.shape[1] // 2
    hi = lax.bitcast_convert_type(x[:, :c].astype(BF16).astype(F32), jnp.uint32)
    lo = lax.bitcast_convert_type(x[:, c:].astype(BF16).astype(F32), jnp.uint32)
    return hi | (lo >> 16)


def _unpack_bf16_pairs(u):
    hi = lax.bitcast_convert_type(u & jnp.uint32(0xFFFF0000), F32)
    lo = lax.bitcast_convert_type(u << 16, F32)
    return hi, lo


ROW_TILE = 8
LANES = 128
SUB_ROWS = 128
MOE_BLK = 256


def _store_token_tiles(ref, x):
    p = _pack_bf16_pairs(x)
    n = x.shape[0]
    for c in range(ROW_TILE):
        ref[pl.ds(c, n, stride=ROW_TILE), :] = p[:, c * LANES:(c + 1) * LANES]


def _load_token_tiles(ref, n):
    p = jnp.concatenate([ref[pl.ds(c, n, stride=ROW_TILE), :] for c in range(ROW_TILE)], axis=1)
    return _unpack_bf16_pairs(p)


def _ln(x, g, b):
    mu = jnp.mean(x, -1, keepdims=True)
    var = jnp.mean(jnp.square(x - mu), -1, keepdims=True)
    return (x - mu) * lax.rsqrt(var + EPS) * g + b


def _matmul_kernel(x_ref, w_ref, o_ref):
    o_ref[...] = jnp.dot(x_ref[...], w_ref[...], preferred_element_type=F32).astype(o_ref.dtype)


def _matmul(x, w, tm, tn, out_dtype):
    M, K = x.shape
    N = w.shape[1]
    return pl.pallas_call(
        _matmul_kernel,
        grid=(M // tm, N // tn),
        in_specs=[pl.BlockSpec((tm, K), lambda i, j: (i, 0)),
                  pl.BlockSpec((K, tn), lambda i, j: (0, j))],
        out_specs=pl.BlockSpec((tm, tn), lambda i, j: (i, j)),
        out_shape=jax.ShapeDtypeStruct((M, N), out_dtype),
        compiler_params=_cparams(("parallel", "arbitrary")),
        name="proj_matmul",
    )(x, w)


def _matmul_nt_kernel(w_ref, x_ref, o_ref):
    o_ref[0] = lax.dot_general(w_ref[...], x_ref[...], (((1,), (1,)), ((), ())),
                               preferred_element_type=F32).astype(o_ref.dtype)


def _matmul_nt(w_t, x, tm, out_dtype):
    M, K = x.shape
    N = w_t.shape[0]
    return pl.pallas_call(
        _matmul_nt_kernel,
        grid=(M // tm,),
        in_specs=[pl.BlockSpec((N, K), lambda i: (0, 0)),
                  pl.BlockSpec((tm, K), lambda i: (i, 0))],
        out_specs=pl.BlockSpec((1, N, tm), lambda i: (i, 0, 0)),
        out_shape=jax.ShapeDtypeStruct((M // tm, N, tm), out_dtype),
        compiler_params=_cparams(("parallel",)),
        name="proj_matmul_nt",
    )(w_t, x)


def _s5_tables(lam_re, lam_im, log_dt, b_re, b_im, c_re, c_im, n_rows):
    G, P, H, S = S5_GROUPS, S5_STATE, S5_GROUP_CH, S5_SUB
    hp = lax.Precision.HIGHEST
    dt = jnp.exp(log_dt.astype(F32))[:, None]
    lam = lax.complex(lam_re.astype(F32), lam_im.astype(F32))
    ldt = lam * dt
    lam_bar = jnp.exp(ldt)
    b_bar = ((lam_bar - 1.0) / lam)[..., None] * lax.complex(b_re.astype(F32), b_im.astype(F32))
    c = lax.complex(c_re.astype(F32), c_im.astype(F32))
    tau = jnp.arange(S + 1, dtype=F32)
    pows = jnp.exp(ldt[None] * tau[:, None, None])
    KB, GL = S5_KBLOCKS, S5_GROUPS // S5_KBLOCKS
    eye = jnp.eye(GL, dtype=bool)
    w1 = pows[:S][::-1][:, :, None, :] * jnp.transpose(b_bar, (0, 2, 1))[None]
    w1 = jnp.transpose(w1.reshape(S, KB, GL, H, P), (1, 0, 2, 3, 4))
    w1 = jnp.where(eye[None, None, :, None, :, None], w1[:, :, :, :, None, :], 0.0)
    w1 = w1.reshape(KB, S * GL * H, GL * P)
    w2 = jnp.transpose(c, (0, 2, 1))[:, :, None, :] * jnp.transpose(pows[1:], (1, 2, 0))[..., None]
    w2 = w2.reshape(KB, GL, P, S, H)
    w2 = jnp.where(eye[None, :, None, None, :, None], w2[:, :, :, :, None, :], 0.0)
    w2 = w2.reshape(KB, GL * P, S * GL * H)
    kc = jnp.real(jnp.einsum('ghp,tgp,gpi->tghi', c, pows[:S], b_bar, precision=hp))
    kc = jnp.transpose(kc.reshape(S, KB, GL, H, H), (1, 2, 4, 0, 3))
    kc = jnp.where(eye[None, :, None, None, :, None], kc[:, :, :, :, None, :], 0.0)
    kcat = kc.reshape(KB, GL * H, S * GL * H)

    nstep = max(1, (n_rows - 1).bit_length())
    kk = (S * (2 ** jnp.arange(nstep))).astype(F32)
    lp = jnp.exp(ldt[None] * kk[:, None, None])
    lp = jnp.transpose(lp.reshape(nstep, KB, GL * P), (1, 0, 2))
    lampow = jnp.stack([jnp.real(lp), jnp.imag(lp)], axis=2)
    return dict(
        kcat=kcat.astype(BF16),
        w1re=jnp.real(w1).astype(BF16), w1im=jnp.imag(w1).astype(BF16),
        w2re=jnp.real(w2).astype(BF16), w2im=(-jnp.imag(w2)).astype(BF16),
        lampow=lampow.astype(F32))


def _s5_kernel(u_ref, w1re_ref, w1im_ref, w2re_ref, w2im_ref, kcat_ref, lp_ref, o_ref,
               ucat_ref, yall_ref, carry_ref, *, nstep):
    t = pl.program_id(1)
    S = S5_SUB
    R = ucat_ref.shape[0]
    W = u_ref.shape[1]

    @pl.when(t == 0)
    def _():
        carry_ref[...] = jnp.zeros_like(carry_ref)

    for j in range(S):
        ucat_ref[:, j * W:(j + 1) * W] = u_ref[pl.ds(j, R, stride=S), :].astype(BF16)
    ucat = ucat_ref[...]
    xre = jnp.dot(ucat, w1re_ref[0], preferred_element_type=F32)
    xim = jnp.dot(ucat, w1im_ref[0], preferred_element_type=F32)
    row = lax.broadcasted_iota(jnp.int32, xre.shape, 0)
    cre, cim = carry_ref[0], carry_ref[1]
    lr, li = lp_ref[0, 0, 0:1, :], lp_ref[0, 0, 1:2, :]
    xre = xre + jnp.where(row == 0, lr * cre - li * cim, 0.0)
    xim = xim + jnp.where(row == 0, lr * cim + li * cre, 0.0)
    for k in range(nstep):
        sh = 1 << k
        pre = pltpu.roll(xre, sh, 0)
        pim = pltpu.roll(xim, sh, 0)
        lr, li = lp_ref[0, k, 0:1, :], lp_ref[0, k, 1:2, :]
        keep = row >= sh
        xre, xim = (xre + jnp.where(keep, lr * pre - li * pim, 0.0),
                    xim + jnp.where(keep, lr * pim + li * pre, 0.0))
    carry_ref[0] = xre[R - 1:R, :]
    carry_ref[1] = xim[R - 1:R, :]
    sre = jnp.where(row >= 1, pltpu.roll(xre, 1, 0), cre).astype(BF16)
    sim = jnp.where(row >= 1, pltpu.roll(xim, 1, 0), cim).astype(BF16)
    yall_ref[...] = (jnp.dot(sre, w2re_ref[0], preferred_element_type=F32)
                     + jnp.dot(sim, w2im_ref[0], preferred_element_type=F32))
    for j in range(S):
        yall_ref[:, j * W:] += jnp.dot(ucat_ref[:, j * W:(j + 1) * W], kcat_ref[0, :, :(S - j) * W],
                                       preferred_element_type=F32)
    for j in range(S):
        o_ref[pl.ds(j, R, stride=S), :] = yall_ref[:, j * W:(j + 1) * W]


def _s5_scan(proj_a, tabs, rt):
    L = proj_a.shape[0]
    S, KB = S5_SUB, S5_KBLOCKS
    W = GROUP_W // KB
    nstep = tabs['lampow'].shape[1]
    P2 = tabs['lampow'].shape[3]
    rows = rt * S
    kb3 = lambda a: pl.BlockSpec((1,) + a.shape[1:], lambda k, t: (k, 0, 0))
    return pl.pallas_call(
        functools.partial(_s5_kernel, nstep=nstep),
        grid=(KB, L // rows),
        in_specs=[pl.BlockSpec((rows, W), lambda k, t: (t, k)),
                  kb3(tabs['w1re']), kb3(tabs['w1im']), kb3(tabs['w2re']), kb3(tabs['w2im']), kb3(tabs['kcat']),
                  pl.BlockSpec((1, nstep, 2, P2), lambda k, t: (k, 0, 0, 0))],
        out_specs=pl.BlockSpec((rows, W), lambda k, t: (t, k)),
        out_shape=jax.ShapeDtypeStruct((L, GROUP_W), F32),
        scratch_shapes=[pltpu.VMEM((rt, S * W), BF16), pltpu.VMEM((rt, S * W), F32), pltpu.VMEM((2, 1, P2), F32)],
        compiler_params=_cparams(("parallel", "arbitrary")),
        name="s5_scan",
    )(proj_a, tabs['w1re'], tabs['w1im'], tabs['w2re'], tabs['w2im'], tabs['kcat'], tabs['lampow'])


def _mixers_kernel(s5u_ref, cb_ref, cc_ref, ch_ref, su_ref, sv_ref, cch_ref, chh_ref, ys_ref,
                   d_ref, wglu_ref, cw_ref, lng_ref, lnb_ref, ws_ref, bs_ref, g_ref, o_ref):
    i = pl.program_id(0)
    tm = o_ref.shape[0]
    gw = GROUP_W
    y = ys_ref[...] + d_ref[...] * s5u_ref[...]
    y = jax.nn.gelu(y)
    y = y * jax.nn.sigmoid(jnp.dot(y.astype(BF16), wglu_ref[...], preferred_element_type=F32))
    o_ref[:, 0:gw] = _rms(y, g_ref[0:1, :]).astype(o_ref.dtype)
    z = cc_ref[...] * ch_ref[...]
    zh = jnp.where(i > 0, cch_ref[...] * chh_ref[...], 0.0)
    row = lax.broadcasted_iota(jnp.int32, z.shape, 0)
    z1 = jnp.where(row == 0, zh[7:8, :], pltpu.roll(z, 1, 0))
    z2 = jnp.where(row == 0, zh[6:7, :], jnp.where(row == 1, zh[7:8, :], pltpu.roll(z, 2, 0)))
    conv = cw_ref[0:1, :] * z2 + cw_ref[1:2, :] * z1 + cw_ref[2:3, :] * z
    o_ref[:, gw:2 * gw] = _rms(cb_ref[...] * conv, g_ref[1:2, :]).astype(o_ref.dtype)
    uu = jax.nn.gelu(su_ref[...])
    vv = _ln(jax.nn.gelu(sv_ref[...]), lng_ref[...], lnb_ref[...]).astype(BF16)
    pi = lax.broadcasted_iota(jnp.int32, (SGU_BLK, SGU_BLK), 0)
    pj = lax.broadcasted_iota(jnp.int32, (SGU_BLK, SGU_BLK), 1)
    causal = (pj // CHUNK) <= (pi // CHUNK)
    hd = gw // SGU_HEADS
    ws = [jnp.where(causal, ws_ref[h], 0.0).astype(BF16) for h in range(SGU_HEADS)]
    blocks = []
    for n in range(tm // SGU_BLK):
        vb = vv[n * SGU_BLK:(n + 1) * SGU_BLK, :]
        blocks.append(jnp.concatenate(
            [jnp.dot(ws[h], vb[:, h * hd:(h + 1) * hd], preferred_element_type=F32) for h in range(SGU_HEADS)],
            axis=1) + bs_ref[...])
    mixed = jnp.concatenate(blocks, axis=0)
    o_ref[:, 2 * gw:3 * gw] = _rms(uu * mixed, g_ref[2:3, :]).astype(o_ref.dtype)


def _mixers(proj_a, ys5, s5_d, w_glu, conv_w, ln_g, ln_b, sgu_w, sgu_b, mix_g, tm):
    L = proj_a.shape[0]
    gw = GROUP_W
    hb = tm // 8
    col = lambda c: pl.BlockSpec((tm, gw), lambda i, c=c: (i, c))
    halo = lambda c: pl.BlockSpec((8, gw), lambda i, c=c: (jnp.maximum(i * hb - 1, 0), c))
    full = lambda a: pl.BlockSpec(a.shape, lambda i: (0,) * a.ndim)
    hd = gw // SGU_HEADS
    bs_full = jnp.repeat(sgu_b.astype(F32).T, hd, axis=1)
    consts = [s5_d.reshape(1, gw).astype(F32), w_glu.astype(BF16), conv_w.astype(F32),
              ln_g.reshape(1, gw).astype(F32), ln_b.reshape(1, gw).astype(F32), sgu_w.astype(F32),
              bs_full, mix_g.reshape(3, gw).astype(F32)]
    return pl.pallas_call(
        _mixers_kernel,
        grid=(L // tm,),
        in_specs=[col(0), col(1), col(2), col(3), col(4), col(5), halo(2), halo(3),
                  pl.BlockSpec((tm, gw), lambda i: (i, 0))] + [full(a) for a in consts],
        out_specs=pl.BlockSpec((tm, 3 * gw), lambda i: (i, 0)),
        out_shape=jax.ShapeDtypeStruct((L, 3 * gw), BF16),
        compiler_params=_cparams(("parallel",)),
        name="row_mixers",
    )(proj_a, proj_a, proj_a, proj_a, proj_a, proj_a, proj_a, proj_a, ys5, *consts)


def _t5_bucket(rel):
    half = NUM_BUCKETS // 2
    ret = jnp.where(rel > 0, half, 0)
    n = jnp.abs(rel)
    max_exact = half // 2
    large = max_exact + (jnp.log(jnp.maximum(n, 1).astype(F32) / max_exact)
                         / math.log(MAX_DISTANCE / max_exact) * (half - max_exact)).astype(jnp.int32)
    large = jnp.minimum(large, half - 1)
    return ret + jnp.where(n < max_exact, n, large)


def _attn_bias_tables(rel_bias, tq):
    assert tq >= MAX_DISTANCE
    rb = rel_bias.astype(F32)
    far = rb[NUM_BUCKETS // 2 - 1]
    buckets = jnp.arange(NUM_BUCKETS)[:, None]

    def bias_of(rel):
        onehot = _t5_bucket(rel)[:, :, None, None] == buckets
        return jnp.sum(jnp.where(onehot, rb, 0.0), axis=2) - far

    kj = jnp.arange(tq)[:, None]
    qi = jnp.arange(tq)[None, :]
    diag = jnp.where(((kj // CHUNK) <= (qi // CHUNK))[..., None], bias_of(kj - qi), NEG_INF)
    prev = bias_of(kj - tq - qi)
    tabs = jnp.stack([jnp.transpose(diag, (2, 0, 1)), jnp.transpose(prev, (2, 0, 1))], axis=1)
    tabs = jnp.where(tabs > 0.5 * NEG_INF, tabs * LOG2E, NEG_INF)
    return jnp.concatenate([tabs, tabs], axis=3)


def _attn_kernel(qt_ref, k_ref, vt_ref, nb_ref, lq1_ref, lk1_ref, lq2_ref, lk2_ref, g_ref, o_ref,
                 qq_s, sa_s, sb_s, p_s, m_s, a_s, acc_s, *, lambda_init):
    i = pl.program_id(1)
    tq = qt_ref.shape[2]
    dq = DIFF_QK_DIM
    qt = qt_ref[0]
    feat = lax.broadcasted_iota(jnp.int32, qt.shape, 0)
    zero = jnp.zeros_like(qt)
    qq = jnp.concatenate([jnp.where(feat < dq, qt, zero), jnp.where(feat >= dq, qt, zero)], axis=1)

    qq_s[...] = qq
    m_s[...] = jnp.full(m_s.shape, -jnp.inf, F32)
    acc_s[...] = jnp.zeros(acc_s.shape, F32)

    def scores(s_ref, j, nblk):
        rows = nblk * tq
        kb = k_ref[pl.ds(pl.multiple_of(j * tq, tq), rows), :]
        s_ref[:rows, :] = jnp.dot(kb, qq_s[...], preferred_element_type=F32)

    def absorb(s_ref, j, nblk, bias=None):
        rows = nblk * tq
        for c in range(2 * tq // 128):
            cs = slice(c * 128, (c + 1) * 128)
            s = s_ref[:rows, cs]
            if bias is not None:
                s = s + bias(cs)
            m_old = m_s[:, cs]
            m_new = jnp.maximum(m_old, jnp.max(s, axis=0, keepdims=True))
            m_s[:, cs] = m_new
            a_s[:, cs] = jnp.exp2(m_old - m_new)
            p_s[:rows, cs] = jnp.exp2((s - m_new).astype(BF16))
        vt = jnp.concatenate([vt_ref[j + b] for b in range(nblk)], axis=1)
        vt = jnp.concatenate([vt, jnp.ones((ONES_ROWS, rows), BF16)], axis=0)
        acc_s[...] = a_s[...] * acc_s[...] + jnp.dot(vt, p_s[:rows, :], preferred_element_type=F32)

    scores(sa_s, i, 1)
    absorb(sa_s, i, 1, lambda cs: nb_ref[0, 0, :, cs])
    jp = jnp.maximum(i - 1, 0)
    first = jnp.where(i > 0, 0.0, NEG_INF)
    scores(sa_s, jp, 1)
    absorb(sa_s, jp, 1, lambda cs: nb_ref[0, 1, :, cs] + first)
    n_far = jnp.maximum(i - 1, 0)
    n_single = n_far % 2
    n_head = n_far % 4

    @pl.when(n_single == 1)
    def _():
        scores(sb_s, 0, 1)
        absorb(sb_s, 0, 1)

    @pl.when(n_head >= 2)
    def _():
        scores(sb_s, n_single, 2)
        absorb(sb_s, n_single, 2)

    scores(sa_s, n_head, 2)

    def quad(qd, c):
        j0 = n_head + 4 * qd
        scores(sb_s, j0 + 2, 2)
        absorb(sa_s, j0, 2)
        scores(sa_s, j0 + 4, 2)
        absorb(sb_s, j0 + 2, 2)
        return c

    lax.fori_loop(0, n_far // 4, quad, 0)
    o = acc_s[:DIFF_V_DIM, :] / acc_s[DIFF_V_DIM:DIFF_V_DIM + 1, :]
    lam = (jnp.exp(jnp.sum(lq1_ref[...] * lk1_ref[...], keepdims=True))
           - jnp.exp(jnp.sum(lq2_ref[...] * lk2_ref[...], keepdims=True)) + lambda_init)
    out = o[:, :tq] - lam * o[:, tq:]
    out = out * lax.rsqrt(jnp.mean(jnp.square(out), axis=0, keepdims=True) + EPS) * g_ref[...]
    o_ref[...] = (out * (1.0 - lambda_init)).T.astype(o_ref.dtype)


def _diff_attention(qvt, kmat, nb, lq1, lk1, lq2, lk2, subln_g, lambda_init):
    nq, _, tq = qvt.shape
    assert nq >= 2
    L = kmat.shape[0]
    H, dv = DIFF_HEADS, DIFF_V_DIM
    vec = lambda a: a.reshape(1, -1).astype(F32)
    small = lambda n: pl.BlockSpec((1, n), lambda h, i: (0, 0))
    return pl.pallas_call(
        functools.partial(_attn_kernel, lambda_init=lambda_init),
        grid=(H, nq),
        in_specs=[pl.BlockSpec((1, dv, tq), lambda h, i: (i, h, 0)),
                  pl.BlockSpec((L, dv), lambda h, i: (0, h)),
                  pl.BlockSpec((nq, dv, tq), lambda h, i: (0, H + h, 0)),
                  pl.BlockSpec((1, 2, tq, 2 * tq), lambda h, i: (h, 0, 0, 0)),
                  small(DIFF_QK_DIM), small(DIFF_QK_DIM), small(DIFF_QK_DIM), small(DIFF_QK_DIM),
                  pl.BlockSpec((dv, 1), lambda h, i: (0, 0))],
        out_specs=pl.BlockSpec((tq, dv), lambda h, i: (i, h)),
        out_shape=jax.ShapeDtypeStruct((L, H * dv), BF16),
        scratch_shapes=[pltpu.VMEM((dv, 2 * tq), BF16),
                        pltpu.VMEM((2 * tq, 2 * tq), F32), pltpu.VMEM((2 * tq, 2 * tq), F32),
                        pltpu.VMEM((2 * tq, 2 * tq), BF16),
                        pltpu.VMEM((1, 2 * tq), F32), pltpu.VMEM((1, 2 * tq), F32),
                        pltpu.VMEM((dv + ONES_ROWS, 2 * tq), F32)],
        compiler_params=_cparams(("parallel", "arbitrary")),
        name="diff_attention",
    )(qvt, kmat, qvt, nb, vec(lq1), vec(lk1), vec(lq2), vec(lk2), subln_g.reshape(dv, 1).astype(F32))


def _outproj_kernel(abc_ref, d_ref, x_ref, wa_ref, wd_ref, g_ref, b_ref, rhi_ref, rlo_ref,
                    x1_ref, x1b_ref, x1p_ref, lg_ref, *, alpha):
    nt = (((1,), (1,)), ((), ()))
    tm = x_ref.shape[0]
    sub = min(tm, SUB_ROWS)
    for r0 in range(0, tm, sub):
        rows = pl.ds(r0, sub)
        mix = (jnp.dot(abc_ref[rows, :], wa_ref[...], preferred_element_type=F32)
               + jnp.dot(d_ref[rows, :], wd_ref[...], preferred_element_type=F32))
        x1 = _ln(alpha * x_ref[rows, :] + mix, g_ref[...], b_ref[...])
        x1_ref[rows, :] = x1
        hi = x1.astype(BF16)
        x1b_ref[rows, :] = hi
        _store_token_tiles(x1p_ref.at[pl.ds(r0 * ROW_TILE, sub * ROW_TILE), :], x1)
        lo = (x1 - hi.astype(F32)).astype(BF16)
        lg_ref[:, r0:r0 + sub] = (lax.dot_general(rhi_ref[...], hi, nt, preferred_element_type=F32)
                                  + lax.dot_general(rhi_ref[...], lo, nt, preferred_element_type=F32)
                                  + lax.dot_general(rlo_ref[...], hi, nt, preferred_element_type=F32))


def _outproj(abc, d_out, x, w_out, ln_g, ln_b, router_w, alpha, tm):
    L, D = x.shape
    E = router_w.shape[1]
    ka = abc.shape[1]
    kd = d_out.shape[1]
    rwt = router_w.astype(F32).T
    rhi = rwt.astype(BF16)
    rlo = (rwt - rhi.astype(F32)).astype(BF16)
    full = lambda shape: pl.BlockSpec(shape, lambda i: (0,) * len(shape))
    return pl.pallas_call(
        functools.partial(_outproj_kernel, alpha=alpha),
        grid=(L // tm,),
        in_specs=[pl.BlockSpec((tm, ka), lambda i: (i, 0)),
                  pl.BlockSpec((tm, kd), lambda i: (i, 0)),
                  pl.BlockSpec((tm, D), lambda i: (i, 0)),
                  pl.BlockSpec((ka, D), lambda i: (0, 0)),
                  pl.BlockSpec((kd, D), lambda i: (ka // kd, 0)),
                  full((1, D)), full((1, D)), full((E, D)), full((E, D))],
        out_specs=[pl.BlockSpec((tm, D), lambda i: (i, 0)),
                   pl.BlockSpec((tm, D), lambda i: (i, 0)),
                   pl.BlockSpec((tm * ROW_TILE, LANES), lambda i: (i, 0)),
                   pl.BlockSpec((E, tm), lambda i: (0, i))],
        out_shape=[jax.ShapeDtypeStruct((L, D), F32), jax.ShapeDtypeStruct((L, D), BF16),
                   jax.ShapeDtypeStruct((L * ROW_TILE, LANES), jnp.uint32), jax.ShapeDtypeStruct((E, L), F32)],
        compiler_params=_cparams(("parallel",)),
        name="outproj_ln1",
    )(abc, d_out, x, w_out, w_out, ln_g.reshape(1, D).astype(F32), ln_b.reshape(1, D).astype(F32), rhi, rlo)


def _router_kernel(lg_ref, bias_ref, e_ref, w_ref, cnt_ref, carry_ref):
    i = pl.program_id(0)
    E, tn = lg_ref.shape
    ng = N_EXPERT_GROUPS
    gs_ = E // ng

    @pl.when(i == 0)
    def _():
        carry_ref[...] = jnp.zeros_like(carry_ref)

    s = jax.nn.sigmoid(lg_ref[...])
    sel = s + bias_ref[...]
    midx = lax.broadcasted_iota(jnp.int32, (gs_, tn), 0).astype(F32)
    rows, gscore = [], []
    for g in range(ng):
        rg = sel[g * gs_:(g + 1) * gs_, :]
        m1 = jnp.max(rg, axis=0, keepdims=True)
        first = jnp.min(jnp.where(rg == m1, midx, float(gs_)), axis=0, keepdims=True)
        m2 = jnp.max(jnp.where(midx == first, -jnp.inf, rg), axis=0, keepdims=True)
        rows.append(rg)
        gscore.append(m1 + m2)
    vals = []
    for g in range(ng):
        rank = jnp.zeros((1, tn), F32)
        for o in range(ng):
            if o != g:
                beats = (gscore[o] >= gscore[g]) if o < g else (gscore[o] > gscore[g])
                rank = rank + jnp.where(beats, 1.0, 0.0)
        vals.append(jnp.where(rank < TOPK_GROUPS, rows[g], -jnp.inf))
    val = jnp.concatenate(vals, axis=0)
    eidx = lax.broadcasted_iota(jnp.int32, val.shape, 0)
    erank = jnp.zeros(val.shape, F32)
    for e in range(E):
        other = val[e:e + 1, :]
        erank = erank + jnp.where(eidx > e, jnp.where(other >= val, 1.0, 0.0), jnp.where(other > val, 1.0, 0.0))
    chosen = erank < TOP_K
    wsel = jnp.where(chosen, s, 0.0)
    wn = wsel / (jnp.sum(wsel, axis=0, keepdims=True) + 1e-20) * ROUTED_SCALE
    total = carry_ref[...] + jnp.sum(jnp.where(chosen, 1.0, 0.0), axis=1, keepdims=True)
    carry_ref[...] = total
    cnt_ref[...] = total.astype(jnp.int32)
    eidf = eidx.astype(F32)
    cand = jnp.where(chosen, eidf, float(E))
    for k in range(TOP_K):
        ek = jnp.min(cand, axis=0, keepdims=True)
        hit = cand == ek
        e_ref[k:k + 1, :] = ek.astype(jnp.int32)
        w_ref[k:k + 1, :] = jnp.sum(jnp.where(hit, wn, 0.0), axis=0, keepdims=True)
        cand = jnp.where(hit, float(E), cand)


def _router(logits_t, router_bias, tn):
    E, L = logits_t.shape
    slot = lambda dt: jax.ShapeDtypeStruct((TOP_K, L), dt)
    return pl.pallas_call(
        _router_kernel,
        grid=(L // tn,),
        in_specs=[pl.BlockSpec((E, tn), lambda i: (0, i)),
                  pl.BlockSpec((E, 1), lambda i: (0, 0))],
        out_specs=[pl.BlockSpec((TOP_K, tn), lambda i: (0, i)),
                   pl.BlockSpec((TOP_K, tn), lambda i: (0, i)),
                   pl.BlockSpec((E, 1), lambda i: (0, 0))],
        out_shape=[slot(jnp.int32), slot(F32), jax.ShapeDtypeStruct((E, 1), jnp.int32)],
        scratch_shapes=[pltpu.VMEM((E, 1), F32)],
        compiler_params=_cparams(("arbitrary",)),
        name="router_topk",
    )(logits_t, router_bias.reshape(E, 1).astype(F32))


def _experts_fused_kernel(blk_ref, exp_ref, nv_ref,
                          src_ref, srcn_ref, dst_ref, x_hbm, wgu_ref, wd_ref, y_hbm,
                          xbuf, ybuf, wgu_b, wd_b, gsem, ssem):
    w = pl.program_id(0)
    nv = nv_ref[0]
    R = xbuf.shape[1] // ROW_TILE
    par = w % 2

    def tile(buf, r):
        return buf.at[pl.ds(r * ROW_TILE, ROW_TILE), :]

    def gather(s_ref, p):
        for r in range(R):
            pltpu.make_async_copy(x_hbm.at[s_ref[0, 0, r]], tile(xbuf.at[p], r), gsem.at[p]).start()

    def gather_wait(p):
        pltpu.make_async_copy(xbuf.at[1 - p], xbuf.at[p], gsem.at[p]).wait()

    def scatter_wait(p):
        pltpu.make_async_copy(ybuf.at[p], ybuf.at[1 - p], ssem.at[p]).wait()

    @pl.when(w == 0)
    def _():
        gather(src_ref, 0)

    @pl.when((w < nv) & ((w == 0) | (exp_ref[w] != exp_ref[jnp.maximum(w - 1, 0)])))
    def _():
        wgu_b[...] = wgu_ref[0, 0].astype(BF16)
        wd_b[...] = wd_ref[0, 0].astype(BF16)

    @pl.when((w >= 2) & (w < nv))
    def _():
        scatter_wait(par)

    @pl.when(w < nv)
    def _():
        gather_wait(par)
        gather(srcn_ref, 1 - par)
        de = wd_b.shape[0]
        xa, xb = _load_token_tiles(xbuf.at[par], R)
        x = jnp.concatenate([xa.astype(BF16), xb.astype(BF16)], axis=1)
        h = jnp.dot(x, wgu_b[...], preferred_element_type=F32)
        a = jax.nn.silu(h[:, :de]) * h[:, de:]
        _store_token_tiles(ybuf.at[par], jnp.dot(a.astype(BF16), wd_b[...], preferred_element_type=F32))
        for r in range(R):
            pltpu.make_async_copy(tile(ybuf.at[par], r), y_hbm.at[dst_ref[0, 0, r]], ssem.at[par]).start()

    @pl.when(w == nv - 1)
    def _():
        scatter_wait(par)
        gather_wait(1 - par)

    @pl.when((w == nv - 1) & (w >= 1))
    def _():
        scatter_wait(1 - par)


def _experts_fused(x1p, src, dst, items, w_gu, w_down, layer, n_out_tiles, blk):
    blk_w, exp_w, n_valid = items
    n_items = blk_w.shape[0]
    D, de2 = w_gu.shape[2], w_gu.shape[3]
    de = w_down.shape[2]
    cur = lambda w, nv: jnp.minimum(w, nv[0] - 1)
    return pl.pallas_call(
        _experts_fused_kernel,
        grid_spec=pltpu.PrefetchScalarGridSpec(
            num_scalar_prefetch=3,
            grid=(n_items,),
            in_specs=[pl.BlockSpec((1, 1, blk), lambda w, b, e, nv: (b[cur(w, nv)], 0, 0),
                                   memory_space=pltpu.SMEM),
                      pl.BlockSpec((1, 1, blk), lambda w, b, e, nv: (b[cur(w + 1, nv)], 0, 0),
                                   memory_space=pltpu.SMEM),
                      pl.BlockSpec((1, 1, blk), lambda w, b, e, nv: (cur(w, nv), 0, 0),
                                   memory_space=pltpu.SMEM),
                      pl.BlockSpec(memory_space=pl.ANY),
                      pl.BlockSpec((1, 1, D, de2), lambda w, b, e, nv: (layer, e[cur(w, nv)], 0, 0)),
                      pl.BlockSpec((1, 1, de, D), lambda w, b, e, nv: (layer, e[cur(w, nv)], 0, 0))],
            out_specs=pl.BlockSpec(memory_space=pl.ANY),
            scratch_shapes=[pltpu.VMEM((2, blk * ROW_TILE, LANES), jnp.uint32),
                            pltpu.VMEM((2, blk * ROW_TILE, LANES), jnp.uint32),
                            pltpu.VMEM((D, de2), BF16), pltpu.VMEM((de, D), BF16),
                            pltpu.SemaphoreType.DMA((2,)), pltpu.SemaphoreType.DMA((2,))]),
        out_shape=jax.ShapeDtypeStruct((n_out_tiles, ROW_TILE, LANES), jnp.uint32),
        compiler_params=_cparams(("arbitrary",)),
        name="moe_experts_fused",
    )(blk_w, exp_w, n_valid, src, src, dst, x1p.reshape(-1, ROW_TILE, LANES), w_gu, w_down).reshape(-1, LANES)


def _combine_stream_kernel(*refs, alpha):
    y_refs = refs[:TOP_K]
    w_ref, x1_ref, x1b_ref, sgu_ref, sdn_ref, g_ref, b_ref, x2_ref, x2b_ref = refs[TOP_K:]
    tm = x1_ref.shape[0]
    de = sdn_ref.shape[0]
    h = jnp.dot(x1b_ref[...], sgu_ref[...], preferred_element_type=F32)
    a = jax.nn.silu(h[:, :de]) * h[:, de:]
    ffn = jnp.dot(a.astype(BF16), sdn_ref[...], preferred_element_type=F32)
    half = ffn.shape[1] // 2
    fa, fb = ffn[:, :half], ffn[:, half:]
    for k in range(TOP_K):
        ya, yb = _load_token_tiles(y_refs[k], tm)
        w = w_ref[:, k:k + 1]
        fa = fa + ya * w
        fb = fb + yb * w
    ffn = jnp.concatenate([fa, fb], axis=1)
    x2 = _ln(alpha * x1_ref[...] + ffn, g_ref[...], b_ref[...])
    x2_ref[...] = x2
    x2b_ref[...] = x2.astype(BF16)


def _combine_stream(yk, w_t, x1, x1b, sh_gu, sh_down, ln_g, ln_b, alpha, tm):
    L, D = x1.shape
    n = L // tm
    full = lambda shape: pl.BlockSpec(shape, lambda i: (0,) * len(shape))
    y_specs = [pl.BlockSpec((tm * ROW_TILE, LANES), lambda i, k=k: (k * n + i, 0)) for k in range(TOP_K)]
    return pl.pallas_call(
        functools.partial(_combine_stream_kernel, alpha=alpha),
        grid=(n,),
        in_specs=y_specs + [pl.BlockSpec((tm, TOP_K), lambda i: (i, 0)),
                            pl.BlockSpec((tm, D), lambda i: (i, 0)),
                            pl.BlockSpec((tm, D), lambda i: (i, 0)),
                            full(sh_gu.shape), full(sh_down.shape), full((1, D)), full((1, D))],
        out_specs=[pl.BlockSpec((tm, D), lambda i: (i, 0)), pl.BlockSpec((tm, D), lambda i: (i, 0))],
        out_shape=[jax.ShapeDtypeStruct((L, D), F32), jax.ShapeDtypeStruct((L, D), BF16)],
        compiler_params=_cparams(("arbitrary",)),
        name="moe_combine_ln2",
    )(*([yk] * TOP_K), w_t, x1, x1b, sh_gu, sh_down, ln_g.reshape(1, D).astype(F32), ln_b.reshape(1, D).astype(F32))


def _moe_layer(x1, x1b, x1p, logits_t, router_bias, w_gu, w_down, layer, sh_gu, sh_down, ln_g, ln_b, alpha,
               router_tn, combine_tm):
    L, D = x1.shape
    E, K, R = N_EXPERTS, TOP_K, MOE_BLK
    A = K * L
    assert A % R == 0
    nblk = A // R
    e_k, w_k, counts = _router(logits_t, router_bias, router_tn)
    keys = (e_k * L + jnp.arange(L, dtype=jnp.int32)[None, :]) * K + jnp.arange(K, dtype=jnp.int32)[:, None]
    skeys = jnp.sort(keys.reshape(A))
    tok = (skeys // K) % L
    src = tok.reshape(nblk, 1, R)
    dst_sorted = ((skeys % K) * L + tok).reshape(nblk, R)
    ends = jnp.cumsum(counts.reshape(E))
    cuts = jnp.sort(jnp.concatenate([jnp.arange(nblk, dtype=jnp.int32) * R, (ends - counts.reshape(E))]))
    lo = cuts
    hi = jnp.concatenate([cuts[1:], jnp.full((1,), A, jnp.int32)])
    valid = hi > lo
    order = jnp.argsort(jnp.logical_not(valid), stable=True)
    lo, hi = lo[order].astype(jnp.int32), hi[order].astype(jnp.int32)
    n_valid = jnp.sum(valid).astype(jnp.int32).reshape(1)
    blk_w = jnp.minimum(lo // R, nblk - 1).astype(jnp.int32)
    exp_w = jnp.minimum(jnp.sum((ends[None, :] <= lo[:, None]).astype(jnp.int32), axis=1), E - 1).astype(jnp.int32)
    n_out_tiles = A + 2 * R
    n_items = blk_w.shape[0]
    onehot = (blk_w[:, None] == jnp.arange(nblk, dtype=jnp.int32)[None, :]).astype(F32)
    dst_rows = jnp.dot(onehot, dst_sorted.astype(F32), precision=lax.Precision.HIGHEST).astype(jnp.int32)
    pos = blk_w[:, None] * R + jnp.arange(R, dtype=jnp.int32)[None, :]
    spare = (A + (jnp.arange(n_items, dtype=jnp.int32)[:, None] % 2) * R
             + jnp.arange(R, dtype=jnp.int32)[None, :])
    dst = jnp.where((pos >= lo[:, None]) & (pos < hi[:, None]), dst_rows, spare).reshape(n_items, 1, R)
    yk = _experts_fused(x1p, src, dst, (blk_w, exp_w, n_valid), w_gu, w_down, layer, n_out_tiles, R)
    return _combine_stream(yk, w_k.T, x1, x1b, sh_gu, sh_down, ln_g, ln_b, alpha, combine_tm)


def _pick(n, pref):
    t = min(n, pref)
    assert n % t == 0
    return t


def _tile_plan(L):
    return dict(
        proj_rows=_pick(L, 1024),
        attn_q=_pick(L, 256),
        s5_subchunks=_pick(L // S5_SUB, 256),
        mixers_rows=_pick(L, 512),
        outproj_rows=_pick(L, 256),
        router_tokens=_pick(L, 512),
        combine_rows=_pick(L, 256))


def kernel(x, w_in, w_out, mix_norm_g, s5_lambda_re, s5_lambda_im, s5_log_dt, s5_b_re, s5_b_im, s5_c_re, s5_c_im, s5_d, s5_w_glu, conv_w, sgu_ln_g, sgu_ln_b, sgu_w, sgu_b, diff_lq1, diff_lk1, diff_lq2, diff_lk2, diff_subln_g, rel_bias, ln1_g, ln1_b, router_w, router_bias, moe_w_gu, moe_w_down, shared_w_gu, shared_w_down, ln2_g, ln2_b):
    Bt, L, D = x.shape
    assert Bt == 1
    depth = w_in.shape[0]
    alpha = (2 * depth) ** 0.25
    gw = GROUP_W
    t = _tile_plan(L)
    nb = _attn_bias_tables(rel_bias, t['attn_q'])
    xf = x.reshape(L, D)
    xb = xf.astype(BF16)
    for l in range(depth):
        w_in_b = w_in[l].astype(BF16)
        proj_a = _matmul(xb, w_in_b[:, :6 * gw], t['proj_rows'], 3 * gw, F32)
        kmat = _matmul(xb, w_in_b[:, 7 * gw:8 * gw], t['proj_rows'], gw, BF16)
        w_q = (w_in[l][:, 6 * gw:7 * gw] * (DIFF_QK_DIM ** -0.5 * LOG2E)).astype(BF16)
        w_qv_t = jnp.concatenate([w_q, w_in_b[:, 8 * gw:]], axis=1).T
        qvt = _matmul_nt(w_qv_t, xb, t['attn_q'], BF16)
        tabs = _s5_tables(s5_lambda_re[l], s5_lambda_im[l], s5_log_dt[l], s5_b_re[l], s5_b_im[l],
                          s5_c_re[l], s5_c_im[l], t['s5_subchunks'])
        ys5 = _s5_scan(proj_a, tabs, t['s5_subchunks'])
        abc = _mixers(proj_a, ys5, s5_d[l], s5_w_glu[l], conv_w[l], sgu_ln_g[l], sgu_ln_b[l], sgu_w[l], sgu_b[l],
                      mix_norm_g[l], t['mixers_rows'])
        lambda_init = 0.8 - 0.6 * math.exp(-0.3 * l)
        d_out = _diff_attention(qvt, kmat, nb, diff_lq1[l], diff_lk1[l], diff_lq2[l], diff_lk2[l],
                                diff_subln_g[l], lambda_init)
        x1, x1b, x1p, logits_t = _outproj(abc, d_out, xf, w_out[l].astype(BF16), ln1_g[l], ln1_b[l], router_w[l],
                                          alpha, t['outproj_rows'])
        xf, xb = _moe_layer(x1, x1b, x1p, logits_t, router_bias[l], moe_w_gu, moe_w_down, l,
                            shared_w_gu[l].astype(BF16), shared_w_down[l].astype(BF16),
                            ln2_g[l], ln2_b[l], alpha, t['router_tokens'], t['combine_rows'])
    return xf.reshape(Bt, L, D)
```

```python
import functools
import math

import jax
import jax.numpy as jnp
from jax import lax
from jax.experimental import pallas as pl
from jax.experimental.pallas import tpu as pltpu

F32 = jnp.float32
BF16 = jnp.bfloat16

GROUP_W = 512
CHUNK = 64
S5_GROUP_CH = 16
S5_GROUPS = 32
S5_STATE = 64
S5_SUB = 4
S5_KBLOCKS = 4
SGU_BLK = 128
SGU_HEADS = 4
DIFF_HEADS = 4
DIFF_QK_DIM = 64
DIFF_V_DIM = 128
NUM_BUCKETS = 32
MAX_DISTANCE = 128
N_EXPERTS = 64
TOP_K = 8
N_EXPERT_GROUPS = 8
TOPK_GROUPS = 4
ROUTED_SCALE = 2.5
EPS = 1e-5
NEG_INF = -1e30
LOG2E = math.log2(math.e)
ONES_ROWS = 16

VMEM_LIMIT = 56 * 1024 * 1024


def _cparams(sem):
    return pltpu.CompilerParams(dimension_semantics=sem, vmem_limit_bytes=VMEM_LIMIT)


def _rms(x, g):
    return x * lax.rsqrt(jnp.mean(jnp.square(x), -1, keepdims=True) + EPS) * g


def _pack_bf16_pairs(x):
    c = x.shape[1] // 2
    hi = lax.bitcast_convert_type(x[:, :c].astype(BF16).astype(F32), jnp.uint32)
    lo = lax.bitcast_convert_type(x[:, c:].astype(BF16).astype(F32), jnp.uint32)
    return hi | (lo >> 16)


def _unpack_bf16_pairs(u):
    hi = lax.bitcast_convert_type(u & jnp.uint32(0xFFFF0000), F32)
    lo = lax.bitcast_convert_type(u << 16, F32)
    return hi, lo


ROW_TILE = 8
LANES = 128
SUB_ROWS = 128
MOE_BLK = 256


def _store_token_tiles(ref, x):
    p = _pack_bf16_pairs(x)
    n = x.shape[0]
    for c in range(ROW_TILE):
        ref[pl.ds(c, n, stride=ROW_TILE), :] = p[:, c * LANES:(c + 1) * LANES]


def _load_token_tiles(ref, n):
    p = jnp.concatenate([ref[pl.ds(c, n, stride=ROW_TILE), :] for c in range(ROW_TILE)], axis=1)
    return _unpack_bf16_pairs(p)


def _ln(x, g, b):
    mu = jnp.mean(x, -1, keepdims=True)
    var = jnp.mean(jnp.square(x - mu), -1, keepdims=True)
    return (x - mu) * lax.rsqrt(var + EPS) * g + b


def _matmul_kernel(x_ref, w_ref, o_ref):
    o_ref[...] = jnp.dot(x_ref[...], w_ref[...], preferred_element_type=F32).astype(o_ref.dtype)


def _matmul(x, w, tm, tn, out_dtype):
    M, K = x.shape
    N = w.shape[1]
    return pl.pallas_call(
        _matmul_kernel,
        grid=(M // tm, N // tn),
        in_specs=[pl.BlockSpec((tm, K), lambda i, j: (i, 0)),
                  pl.BlockSpec((K, tn), lambda i, j: (0, j))],
        out_specs=pl.BlockSpec((tm, tn), lambda i, j: (i, j)),
        out_shape=jax.ShapeDtypeStruct((M, N), out_dtype),
        compiler_params=_cparams(("parallel", "arbitrary")),
        name="proj_matmul",
    )(x, w)


def _matmul_nt_kernel(w_ref, x_ref, o_ref):
    o_ref[0] = lax.dot_general(w_ref[...], x_ref[...], (((1,), (1,)), ((), ())),
                               preferred_element_type=F32).astype(o_ref.dtype)


def _matmul_nt(w_t, x, tm, out_dtype):
    M, K = x.shape
    N = w_t.shape[0]
    return pl.pallas_call(
        _matmul_nt_kernel,
        grid=(M // tm,),
        in_specs=[pl.BlockSpec((N, K), lambda i: (0, 0)),
                  pl.BlockSpec((tm, K), lambda i: (i, 0))],
        out_specs=pl.BlockSpec((1, N, tm), lambda i: (i, 0, 0)),
        out_shape=jax.ShapeDtypeStruct((M // tm, N, tm), out_dtype),
        compiler_params=_cparams(("parallel",)),
        name="proj_matmul_nt",
    )(w_t, x)


def _s5_tables(lam_re, lam_im, log_dt, b_re, b_im, c_re, c_im, n_rows):
    G, P, H, S = S5_GROUPS, S5_STATE, S5_GROUP_CH, S5_SUB
    hp = lax.Precision.HIGHEST
    dt = jnp.exp(log_dt.astype(F32))[:, None]
    lam = lax.complex(lam_re.astype(F32), lam_im.astype(F32))
    ldt = lam * dt
    lam_bar = jnp.exp(ldt)
    b_bar = ((lam_bar - 1.0) / lam)[..., None] * lax.complex(b_re.astype(F32), b_im.astype(F32))
    c = lax.complex(c_re.astype(F32), c_im.astype(F32))
    tau = jnp.arange(S + 1, dtype=F32)
    pows = jnp.exp(ldt[None] * tau[:, None, None])
    KB, GL = S5_KBLOCKS, S5_GROUPS // S5_KBLOCKS
    eye = jnp.eye(GL, dtype=bool)
    w1 = pows[:S][::-1][:, :, None, :] * jnp.transpose(b_bar, (0, 2, 1))[None]
    w1 = jnp.transpose(w1.reshape(S, KB, GL, H, P), (1, 0, 2, 3, 4))
    w1 = jnp.where(eye[None, None, :, None, :, None], w1[:, :, :, :, None, :], 0.0)
    w1 = w1.reshape(KB, S * GL * H, GL * P)
    w2 = jnp.transpose(c, (0, 2, 1))[:, :, None, :] * jnp.transpose(pows[1:], (1, 2, 0))[..., None]
    w2 = w2.reshape(KB, GL, P, S, H)
    w2 = jnp.where(eye[None, :, None, None, :, None], w2[:, :, :, :, None, :], 0.0)
    w2 = w2.reshape(KB, GL * P, S * GL * H)
    kc = jnp.real(jnp.einsum('ghp,tgp,gpi->tghi', c, pows[:S], b_bar, precision=hp))
    kc = jnp.transpose(kc.reshape(S, KB, GL, H, H), (1, 2, 4, 0, 3))
    kc = jnp.where(eye[None, :, None, None, :, None], kc[:, :, :, :, None, :], 0.0)
    kcat = kc.reshape(KB, GL * H, S * GL * H)

    nstep = max(1, (n_rows - 1).bit_length())
    kk = (S * (2 ** jnp.arange(nstep))).astype(F32)
    lp = jnp.exp(ldt[None] * kk[:, None, None])
    lp = jnp.transpose(lp.reshape(nstep, KB, GL * P), (1, 0, 2))
    lampow = jnp.stack([jnp.real(lp), jnp.imag(lp)], axis=2)
    return dict(
        kcat=kcat.astype(BF16),
        w1re=jnp.real(w1).astype(BF16), w1im=jnp.imag(w1).astype(BF16),
        w2re=jnp.real(w2).astype(BF16), w2im=(-jnp.imag(w2)).astype(BF16),
        lampow=lampow.astype(F32))


def _s5_kernel(u_ref, w1re_ref, w1im_ref, w2re_ref, w2im_ref, kcat_ref, lp_ref, o_ref,
               ucat_ref, yall_ref, carry_ref, *, nstep):
    t = pl.program_id(1)
    S = S5_SUB
    R = ucat_ref.shape[0]
    W = u_ref.shape[1]

    @pl.when(t == 0)
    def _():
        carry_ref[...] = jnp.zeros_like(carry_ref)

    for j in range(S):
        ucat_ref[:, j * W:(j + 1) * W] = u_ref[pl.ds(j, R, stride=S), :].astype(BF16)
    ucat = ucat_ref[...]
    xre = jnp.dot(ucat, w1re_ref[0], preferred_element_type=F32)
    xim = jnp.dot(ucat, w1im_ref[0], preferred_element_type=F32)
    row = lax.broadcasted_iota(jnp.int32, xre.shape, 0)
    cre, cim = carry_ref[0], carry_ref[1]
    lr, li = lp_ref[0, 0, 0:1, :], lp_ref[0, 0, 1:2, :]
    xre = xre + jnp.where(row == 0, lr * cre - li * cim, 0.0)
    xim = xim + jnp.where(row == 0, lr * cim + li * cre, 0.0)
    for k in range(nstep):
        sh = 1 << k
        pre = pltpu.roll(xre, sh, 0)
        pim = pltpu.roll(xim, sh, 0)
        lr, li = lp_ref[0, k, 0:1, :], lp_ref[0, k, 1:2, :]
        keep = row >= sh
        xre, xim = (xre + jnp.where(keep, lr * pre - li * pim, 0.0),
                    xim + jnp.where(keep, lr * pim + li * pre, 0.0))
    carry_ref[0] = xre[R - 1:R, :]
    carry_ref[1] = xim[R - 1:R, :]
    sre = jnp.where(row >= 1, pltpu.roll(xre, 1, 0), cre).astype(BF16)
    sim = jnp.where(row >= 1, pltpu.roll(xim, 1, 0), cim).astype(BF16)
    yall_ref[...] = (jnp.dot(sre, w2re_ref[0], preferred_element_type=F32)
                     + jnp.dot(sim, w2im_ref[0], preferred_element_type=F32))
    for j in range(S):
        yall_ref[:, j * W:] += jnp.dot(ucat_ref[:, j * W:(j + 1) * W], kcat_ref[0, :, :(S - j) * W],
                                       preferred_element_type=F32)
    for j in range(S):
        o_ref[pl.ds(j, R, stride=S), :] = yall_ref[:, j * W:(j + 1) * W]


def _s5_scan(proj_a, tabs, rt):
    L = proj_a.shape[0]
    S, KB = S5_SUB, S5_KBLOCKS
    W = GROUP_W // KB
    nstep = tabs['lampow'].shape[1]
    P2 = tabs['lampow'].shape[3]
    rows = rt * S
    kb3 = lambda a: pl.BlockSpec((1,) + a.shape[1:], lambda k, t: (k, 0, 0))
    return pl.pallas_call(
        functools.partial(_s5_kernel, nstep=nstep),
        grid=(KB, L // rows),
        in_specs=[pl.BlockSpec((rows, W), lambda k, t: (t, k)),
                  kb3(tabs['w1re']), kb3(tabs['w1im']), kb3(tabs['w2re']), kb3(tabs['w2im']), kb3(tabs['kcat']),
                  pl.BlockSpec((1, nstep, 2, P2), lambda k, t: (k, 0, 0, 0))],
        out_specs=pl.BlockSpec((rows, W), lambda k, t: (t, k)),
        out_shape=jax.ShapeDtypeStruct((L, GROUP_W), F32),
        scratch_shapes=[pltpu.VMEM((rt, S * W), BF16), pltpu.VMEM((rt, S * W), F32), pltpu.VMEM((2, 1, P2), F32)],
        compiler_params=_cparams(("parallel", "arbitrary")),
        name="s5_scan",
    )(proj_a, tabs['w1re'], tabs['w1im'], tabs['w2re'], tabs['w2im'], tabs['kcat'], tabs['lampow'])


def _mixers_kernel(s5u_ref, cb_ref, cc_ref, ch_ref, su_ref, sv_ref, cch_ref, chh_ref, ys_ref,
                   d_ref, wglu_ref, cw_ref, lng_ref, lnb_ref, ws_ref, bs_ref, g_ref, o_ref):
    i = pl.program_id(0)
    tm = o_ref.shape[0]
    gw = GROUP_W
    y = ys_ref[...] + d_ref[...] * s5u_ref[...]
    y = jax.nn.gelu(y)
    y = y * jax.nn.sigmoid(jnp.dot(y.astype(BF16), wglu_ref[...], preferred_element_type=F32))
    o_ref[:, 0:gw] = _rms(y, g_ref[0:1, :]).astype(o_ref.dtype)
    z = cc_ref[...] * ch_ref[...]
    zh = jnp.where(i > 0, cch_ref[...] * chh_ref[...], 0.0)
    row = lax.broadcasted_iota(jnp.int32, z.shape, 0)
    z1 = jnp.where(row == 0, zh[7:8, :], pltpu.roll(z, 1, 0))
    z2 = jnp.where(row == 0, zh[6:7, :], jnp.where(row == 1, zh[7:8, :], pltpu.roll(z, 2, 0)))
    conv = cw_ref[0:1, :] * z2 + cw_ref[1:2, :] * z1 + cw_ref[2:3, :] * z
    o_ref[:, gw:2 * gw] = _rms(cb_ref[...] * conv, g_ref[1:2, :]).astype(o_ref.dtype)
    uu = jax.nn.gelu(su_ref[...])
    vv = _ln(jax.nn.gelu(sv_ref[...]), lng_ref[...], lnb_ref[...]).astype(BF16)
    pi = lax.broadcasted_iota(jnp.int32, (SGU_BLK, SGU_BLK), 0)
    pj = lax.broadcasted_iota(jnp.int32, (SGU_BLK, SGU_BLK), 1)
    causal = (pj // CHUNK) <= (pi // CHUNK)
    hd = gw // SGU_HEADS
    ws = [jnp.where(causal, ws_ref[h], 0.0).astype(BF16) for h in range(SGU_HEADS)]
    blocks = []
    for n in range(tm // SGU_BLK):
        vb = vv[n * SGU_BLK:(n + 1) * SGU_BLK, :]
        blocks.append(jnp.concatenate(
            [jnp.dot(ws[h], vb[:, h * hd:(h + 1) * hd], preferred_element_type=F32) for h in range(SGU_HEADS)],
            axis=1) + bs_ref[...])
    mixed = jnp.concatenate(blocks, axis=0)
    o_ref[:, 2 * gw:3 * gw] = _rms(uu * mixed, g_ref[2:3, :]).astype(o_ref.dtype)


def _mixers(proj_a, ys5, s5_d, w_glu, conv_w, ln_g, ln_b, sgu_w, sgu_b, mix_g, tm):
    L = proj_a.shape[0]
    gw = GROUP_W
    hb = tm // 8
    col = lambda c: pl.BlockSpec((tm, gw), lambda i, c=c: (i, c))
    halo = lambda c: pl.BlockSpec((8, gw), lambda i, c=c: (jnp.maximum(i * hb - 1, 0), c))
    full = lambda a: pl.BlockSpec(a.shape, lambda i: (0,) * a.ndim)
    hd = gw // SGU_HEADS
    bs_full = jnp.repeat(sgu_b.astype(F32).T, hd, axis=1)
    consts = [s5_d.reshape(1, gw).astype(F32), w_glu.astype(BF16), conv_w.astype(F32),
              ln_g.reshape(1, gw).astype(F32), ln_b.reshape(1, gw).astype(F32), sgu_w.astype(F32),
              bs_full, mix_g.reshape(3, gw).astype(F32)]
    return pl.pallas_call(
        _mixers_kernel,
        grid=(L // tm,),
        in_specs=[col(0), col(1), col(2), col(3), col(4), col(5), halo(2), halo(3),
                  pl.BlockSpec((tm, gw), lambda i: (i, 0))] + [full(a) for a in consts],
        out_specs=pl.BlockSpec((tm, 3 * gw), lambda i: (i, 0)),
        out_shape=jax.ShapeDtypeStruct((L, 3 * gw), BF16),
        compiler_params=_cparams(("parallel",)),
        name="row_mixers",
    )(proj_a, proj_a, proj_a, proj_a, proj_a, proj_a, proj_a, proj_a, ys5, *consts)


def _t5_bucket(rel):
    half = NUM_BUCKETS // 2
    ret = jnp.where(rel > 0, half, 0)
    n = jnp.abs(rel)
    max_exact = half // 2
    large = max_exact + (jnp.log(jnp.maximum(n, 1).astype(F32) / max_exact)
                         / math.log(MAX_DISTANCE / max_exact) * (half - max_exact)).astype(jnp.int32)
    large = jnp.minimum(large, half - 1)
    return ret + jnp.where(n < max_exact, n, large)


def _attn_bias_tables(rel_bias, tq):
    assert tq >= MAX_DISTANCE
    rb = rel_bias.astype(F32)
    far = rb[NUM_BUCKETS // 2 - 1]
    buckets = jnp.arange(NUM_BUCKETS)[:, None]

    def bias_of(rel):
        onehot = _t5_bucket(rel)[:, :, None, None] == buckets
        return jnp.sum(jnp.where(onehot, rb, 0.0), axis=2) - far

    kj = jnp.arange(tq)[:, None]
    qi = jnp.arange(tq)[None, :]
    diag = jnp.where(((kj // CHUNK) <= (qi // CHUNK))[..., None], bias_of(kj - qi), NEG_INF)
    prev = bias_of(kj - tq - qi)
    tabs = jnp.stack([jnp.transpose(diag, (2, 0, 1)), jnp.transpose(prev, (2, 0, 1))], axis=1)
    tabs = jnp.where(tabs > 0.5 * NEG_INF, tabs * LOG2E, NEG_INF)
    return jnp.concatenate([tabs, tabs], axis=3)


def _attn_kernel(qt_ref, k_ref, vt_ref, nb_ref, lq1_ref, lk1_ref, lq2_ref, lk2_ref, g_ref, o_ref,
                 qq_s, sa_s, sb_s, p_s, m_s, a_s, acc_s, *, lambda_init):
    i = pl.program_id(1)
    tq = qt_ref.shape[2]
    dq = DIFF_QK_DIM
    qt = qt_ref[0]
    feat = lax.broadcasted_iota(jnp.int32, qt.shape, 0)
    zero = jnp.zeros_like(qt)
    qq = jnp.concatenate([jnp.where(feat < dq, qt, zero), jnp.where(feat >= dq, qt, zero)], axis=1)

    qq_s[...] = qq
    m_s[...] = jnp.full(m_s.shape, -jnp.inf, F32)
    acc_s[...] = jnp.zeros(acc_s.shape, F32)

    def scores(s_ref, j, nblk):
        rows = nblk * tq
        kb = k_ref[pl.ds(pl.multiple_of(j * tq, tq), rows), :]
        s_ref[:rows, :] = jnp.dot(kb, qq_s[...], preferred_element_type=F32)

    def absorb(s_ref, j, nblk, bias=None):
        rows = nblk * tq
        for c in range(2 * tq // 128):
            cs = slice(c * 128, (c + 1) * 128)
            s = s_ref[:rows, cs]
            if bias is not None:
                s = s + bias(cs)
            m_old = m_s[:, cs]
            m_new = jnp.maximum(m_old, jnp.max(s, axis=0, keepdims=True))
            m_s[:, cs] = m_new
            a_s[:, cs] = jnp.exp2(m_old - m_new)
            p_s[:rows, cs] = jnp.exp2((s - m_new).astype(BF16))
        vt = jnp.concatenate([vt_ref[j + b] for b in range(nblk)], axis=1)
        vt = jnp.concatenate([vt, jnp.ones((ONES_ROWS, rows), BF16)], axis=0)
        acc_s[...] = a_s[...] * acc_s[...] + jnp.dot(vt, p_s[:rows, :], preferred_element_type=F32)

    scores(sa_s, i, 1)
    absorb(sa_s, i, 1, lambda cs: nb_ref[0, 0, :, cs])
    jp = jnp.maximum(i - 1, 0)
    first = jnp.where(i > 0, 0.0, NEG_INF)
    scores(sa_s, jp, 1)
    absorb(sa_s, jp, 1, lambda cs: nb_ref[0, 1, :, cs] + first)
    n_far = jnp.maximum(i - 1, 0)
    n_single = n_far % 2
    n_head = n_far % 4

    @pl.when(n_single == 1)
    def _():
        scores(sb_s, 0, 1)
        absorb(sb_s, 0, 1)

    @pl.when(n_head >= 2)
    def _():
        scores(sb_s, n_single, 2)
        absorb(sb_s, n_single, 2)

    scores(sa_s, n_head, 2)

    def quad(qd, c):
        j0 = n_head + 4 * qd
        scores(sb_s, j0 + 2, 2)
        absorb(sa_s, j0, 2)
        scores(sa_s, j0 + 4, 2)
        absorb(sb_s, j0 + 2, 2)
        return c

    lax.fori_loop(0, n_far // 4, quad, 0)
    o = acc_s[:DIFF_V_DIM, :] / acc_s[DIFF_V_DIM:DIFF_V_DIM + 1, :]
    lam = (jnp.exp(jnp.sum(lq1_ref[...] * lk1_ref[...], keepdims=True))
           - jnp.exp(jnp.sum(lq2_ref[...] * lk2_ref[...], keepdims=True)) + lambda_init)
    out = o[:, :tq] - lam * o[:, tq:]
    out = out * lax.rsqrt(jnp.mean(jnp.square(out), axis=0, keepdims=True) + EPS) * g_ref[...]
    o_ref[...] = (out * (1.0 - lambda_init)).T.astype(o_ref.dtype)


def _diff_attention(qvt, kmat, nb, lq1, lk1, lq2, lk2, subln_g, lambda_init):
    nq, _, tq = qvt.shape
    assert nq >= 2
    L = kmat.shape[0]
    H, dv = DIFF_HEADS, DIFF_V_DIM
    vec = lambda a: a.reshape(1, -1).astype(F32)
    small = lambda n: pl.BlockSpec((1, n), lambda h, i: (0, 0))
    return pl.pallas_call(
        functools.partial(_attn_kernel, lambda_init=lambda_init),
        grid=(H, nq),
        in_specs=[pl.BlockSpec((1, dv, tq), lambda h, i: (i, h, 0)),
                  pl.BlockSpec((L, dv), lambda h, i: (0, h)),
                  pl.BlockSpec((nq, dv, tq), lambda h, i: (0, H + h, 0)),
                  pl.BlockSpec((1, 2, tq, 2 * tq), lambda h, i: (h, 0, 0, 0)),
                  small(DIFF_QK_DIM), small(DIFF_QK_DIM), small(DIFF_QK_DIM), small(DIFF_QK_DIM),
                  pl.BlockSpec((dv, 1), lambda h, i: (0, 0))],
        out_specs=pl.BlockSpec((tq, dv), lambda h, i: (i, h)),
        out_shape=jax.ShapeDtypeStruct((L, H * dv), BF16),
        scratch_shapes=[pltpu.VMEM((dv, 2 * tq), BF16),
                        pltpu.VMEM((2 * tq, 2 * tq), F32), pltpu.VMEM((2 * tq, 2 * tq), F32),
                        pltpu.VMEM((2 * tq, 2 * tq), BF16),
                        pltpu.VMEM((1, 2 * tq), F32), pltpu.VMEM((1, 2 * tq), F32),
                        pltpu.VMEM((dv + ONES_ROWS, 2 * tq), F32)],
        compiler_params=_cparams(("parallel", "arbitrary")),
        name="diff_attention",
    )(qvt, kmat, qvt, nb, vec(lq1), vec(lk1), vec(lq2), vec(lk2), subln_g.reshape(dv, 1).astype(F32))


def _outproj_kernel(abc_ref, d_ref, x_ref, wa_ref, wd_ref, g_ref, b_ref, rhi_ref, rlo_ref,
                    x1_ref, x1b_ref, x1p_ref, lg_ref, *, alpha):
    nt = (((1,), (1,)), ((), ()))
    tm = x_ref.shape[0]
    sub = min(tm, SUB_ROWS)
    for r0 in range(0, tm, sub):
        rows = pl.ds(r0, sub)
        mix = (jnp.dot(abc_ref[rows, :], wa_ref[...], preferred_element_type=F32)
               + jnp.dot(d_ref[rows, :], wd_ref[...], preferred_element_type=F32))
        x1 = _ln(alpha * x_ref[rows, :] + mix, g_ref[...], b_ref[...])
        x1_ref[rows, :] = x1
        hi = x1.astype(BF16)
        x1b_ref[rows, :] = hi
        _store_token_tiles(x1p_ref.at[pl.ds(r0 * ROW_TILE, sub * ROW_TILE), :], x1)
        lo = (x1 - hi.astype(F32)).astype(BF16)
        lg_ref[:, r0:r0 + sub] = (lax.dot_general(rhi_ref[...], hi, nt, preferred_element_type=F32)
                                  + lax.dot_general(rhi_ref[...], lo, nt, preferred_element_type=F32)
                                  + lax.dot_general(rlo_ref[...], hi, nt, preferred_element_type=F32))


def _outproj(abc, d_out, x, w_out, ln_g, ln_b, router_w, alpha, tm):
    L, D = x.shape
    E = router_w.shape[1]
    ka = abc.shape[1]
    kd = d_out.shape[1]
    rwt = router_w.astype(F32).T
    rhi = rwt.astype(BF16)
    rlo = (rwt - rhi.astype(F32)).astype(BF16)
    full = lambda shape: pl.BlockSpec(shape, lambda i: (0,) * len(shape))
    return pl.pallas_call(
        functools.partial(_outproj_kernel, alpha=alpha),
        grid=(L // tm,),
        in_specs=[pl.BlockSpec((tm, ka), lambda i: (i, 0)),
                  pl.BlockSpec((tm, kd), lambda i: (i, 0)),
                  pl.BlockSpec((tm, D), lambda i: (i, 0)),
                  pl.BlockSpec((ka, D), lambda i: (0, 0)),
                  pl.BlockSpec((kd, D), lambda i: (ka // kd, 0)),
                  full((1, D)), full((1, D)), full((E, D)), full((E, D))],
        out_specs=[pl.BlockSpec((tm, D), lambda i: (i, 0)),
                   pl.BlockSpec((tm, D), lambda i: (i, 0)),
                   pl.BlockSpec((tm * ROW_TILE, LANES), lambda i: (i, 0)),
                   pl.BlockSpec((E, tm), lambda i: (0, i))],
        out_shape=[jax.ShapeDtypeStruct((L, D), F32), jax.ShapeDtypeStruct((L, D), BF16),
                   jax.ShapeDtypeStruct((L * ROW_TILE, LANES), jnp.uint32), jax.ShapeDtypeStruct((E, L), F32)],
        compiler_params=_cparams(("parallel",)),
        name="outproj_ln1",
    )(abc, d_out, x, w_out, w_out, ln_g.reshape(1, D).astype(F32), ln_b.reshape(1, D).astype(F32), rhi, rlo)


def _router_kernel(lg_ref, bias_ref, e_ref, w_ref, cnt_ref, carry_ref):
    i = pl.program_id(0)
    E, tn = lg_ref.shape
    ng = N_EXPERT_GROUPS
    gs_ = E // ng

    @pl.when(i == 0)
    def _():
        carry_ref[...] = jnp.zeros_like(carry_ref)

    s = jax.nn.sigmoid(lg_ref[...])
    sel = s + bias_ref[...]
    midx = lax.broadcasted_iota(jnp.int32, (gs_, tn), 0).astype(F32)
    rows, gscore = [], []
    for g in range(ng):
        rg = sel[g * gs_:(g + 1) * gs_, :]
        m1 = jnp.max(rg, axis=0, keepdims=True)
        first = jnp.min(jnp.where(rg == m1, midx, float(gs_)), axis=0, keepdims=True)
        m2 = jnp.max(jnp.where(midx == first, -jnp.inf, rg), axis=0, keepdims=True)
        rows.append(rg)
        gscore.append(m1 + m2)
    vals = []
    for g in range(ng):
        rank = jnp.zeros((1, tn), F32)
        for o in range(ng):
            if o != g:
                beats = (gscore[o] >= gscore[g]) if o < g else (gscore[o] > gscore[g])
                rank = rank + jnp.where(beats, 1.0, 0.0)
        vals.append(jnp.where(rank < TOPK_GROUPS, rows[g], -jnp.inf))
    val = jnp.concatenate(vals, axis=0)
    eidx = lax.broadcasted_iota(jnp.int32, val.shape, 0)
    erank = jnp.zeros(val.shape, F32)
    for e in range(E):
        other = val[e:e + 1, :]
        erank = erank + jnp.where(eidx > e, jnp.where(other >= val, 1.0, 0.0), jnp.where(other > val, 1.0, 0.0))
    chosen = erank < TOP_K
    wsel = jnp.where(chosen, s, 0.0)
    wn = wsel / (jnp.sum(wsel, axis=0, keepdims=True) + 1e-20) * ROUTED_SCALE
    total = carry_ref[...] + jnp.sum(jnp.where(chosen, 1.0, 0.0), axis=1, keepdims=True)
    carry_ref[...] = total
    cnt_ref[...] = total.astype(jnp.int32)
    eidf = eidx.astype(F32)
    cand = jnp.where(chosen, eidf, float(E))
    for k in range(TOP_K):
        ek = jnp.min(cand, axis=0, keepdims=True)
        hit = cand == ek
        e_ref[k:k + 1, :] = ek.astype(jnp.int32)
        w_ref[k:k + 1, :] = jnp.sum(jnp.where(hit, wn, 0.0), axis=0, keepdims=True)
        cand = jnp.where(hit, float(E), cand)


def _router(logits_t, router_bias, tn):
    E, L = logits_t.shape
    slot = lambda dt: jax.ShapeDtypeStruct((TOP_K, L), dt)
    return pl.pallas_call(
        _router_kernel,
        grid=(L // tn,),
        in_specs=[pl.BlockSpec((E, tn), lambda i: (0, i)),
                  pl.BlockSpec((E, 1), lambda i: (0, 0))],
        out_specs=[pl.BlockSpec((TOP_K, tn), lambda i: (0, i)),
                   pl.BlockSpec((TOP_K, tn), lambda i: (0, i)),
                   pl.BlockSpec((E, 1), lambda i: (0, 0))],
        out_shape=[slot(jnp.int32), slot(F32), jax.ShapeDtypeStruct((E, 1), jnp.int32)],
        scratch_shapes=[pltpu.VMEM((E, 1), F32)],
        compiler_params=_cparams(("arbitrary",)),
        name="router_topk",
    )(logits_t, router_bias.reshape(E, 1).astype(F32))


def _experts_fused_kernel(blk_ref, exp_ref, nv_ref,
                          src_ref, srcn_ref, dst_ref, x_hbm, wgu_ref, wd_ref, y_hbm,
                          xbuf, ybuf, wgu_b, wd_b, gsem, ssem):
    w = pl.program_id(0)
    nv = nv_ref[0]
    R = xbuf.shape[1] // ROW_TILE
    par = w % 2

    def tile(buf, r):
        return buf.at[pl.ds(r * ROW_TILE, ROW_TILE), :]

    def gather(s_ref, p):
        for r in range(R):
            pltpu.make_async_copy(x_hbm.at[s_ref[0, 0, r]], tile(xbuf.at[p], r), gsem.at[p]).start()

    def gather_wait(p):
        pltpu.make_async_copy(xbuf.at[1 - p], xbuf.at[p], gsem.at[p]).wait()

    def scatter_wait(p):
        pltpu.make_async_copy(ybuf.at[p], ybuf.at[1 - p], ssem.at[p]).wait()

    @pl.when(w == 0)
    def _():
        gather(src_ref, 0)

    @pl.when((w < nv) & ((w == 0) | (exp_ref[w] != exp_ref[jnp.maximum(w - 1, 0)])))
    def _():
        wgu_b[...] = wgu_ref[0, 0].astype(BF16)
        wd_b[...] = wd_ref[0, 0].astype(BF16)

    @pl.when((w >= 2) & (w < nv))
    def _():
        scatter_wait(par)

    @pl.when(w < nv)
    def _():
        gather_wait(par)
        gather(srcn_ref, 1 - par)
        de = wd_b.shape[0]
        xa, xb = _load_token_tiles(xbuf.at[par], R)
        x = jnp.concatenate([xa.astype(BF16), xb.astype(BF16)], axis=1)
        h = jnp.dot(x, wgu_b[...], preferred_element_type=F32)
        a = jax.nn.silu(h[:, :de]) * h[:, de:]
        _store_token_tiles(ybuf.at[par], jnp.dot(a.astype(BF16), wd_b[...], preferred_element_type=F32))
        for r in range(R):
            pltpu.make_async_copy(tile(ybuf.at[par], r), y_hbm.at[dst_ref[0, 0, r]], ssem.at[par]).start()

    @pl.when(w == nv - 1)
    def _():
        scatter_wait(par)
        gather_wait(1 - par)

    @pl.when((w == nv - 1) & (w >= 1))
    def _():
        scatter_wait(1 - par)


def _experts_fused(x1p, src, dst, items, w_gu, w_down, layer, n_out_tiles, blk):
    blk_w, exp_w, n_valid = items
    n_items = blk_w.shape[0]
    D, de2 = w_gu.shape[2], w_gu.shape[3]
    de = w_down.shape[2]
    cur = lambda w, nv: jnp.minimum(w, nv[0] - 1)
    return pl.pallas_call(
        _experts_fused_kernel,
        grid_spec=pltpu.PrefetchScalarGridSpec(
            num_scalar_prefetch=3,
            grid=(n_items,),
            in_specs=[pl.BlockSpec((1, 1, blk), lambda w, b, e, nv: (b[cur(w, nv)], 0, 0),
                                   memory_space=pltpu.SMEM),
                      pl.BlockSpec((1, 1, blk), lambda w, b, e, nv: (b[cur(w + 1, nv)], 0, 0),
                                   memory_space=pltpu.SMEM),
                      pl.BlockSpec((1, 1, blk), lambda w, b, e, nv: (cur(w, nv), 0, 0),
                                   memory_space=pltpu.SMEM),
                      pl.BlockSpec(memory_space=pl.ANY),
                      pl.BlockSpec((1, 1, D, de2), lambda w, b, e, nv: (layer, e[cur(w, nv)], 0, 0)),
                      pl.BlockSpec((1, 1, de, D), lambda w, b, e, nv: (layer, e[cur(w, nv)], 0, 0))],
            out_specs=pl.BlockSpec(memory_space=pl.ANY),
            scratch_shapes=[pltpu.VMEM((2, blk * ROW_TILE, LANES), jnp.uint32),
                            pltpu.VMEM((2, blk * ROW_TILE, LANES), jnp.uint32),
                            pltpu.VMEM((D, de2), BF16), pltpu.VMEM((de, D), BF16),
                            pltpu.SemaphoreType.DMA((2,)), pltpu.SemaphoreType.DMA((2,))]),
        out_shape=jax.ShapeDtypeStruct((n_out_tiles, ROW_TILE, LANES), jnp.uint32),
        compiler_params=_cparams(("arbitrary",)),
        name="moe_experts_fused",
    )(blk_w, exp_w, n_valid, src, src, dst, x1p.reshape(-1, ROW_TILE, LANES), w_gu, w_down).reshape(-1, LANES)


def _combine_stream_kernel(*refs, alpha):
    y_refs = refs[:TOP_K]
    w_ref, x1_ref, x1b_ref, sgu_ref, sdn_ref, g_ref, b_ref, x2_ref, x2b_ref = refs[TOP_K:]
    tm = x1_ref.shape[0]
    de = sdn_ref.shape[0]
    h = jnp.dot(x1b_ref[...], sgu_ref[...], preferred_element_type=F32)
    a = jax.nn.silu(h[:, :de]) * h[:, de:]
    ffn = jnp.dot(a.astype(BF16), sdn_ref[...], preferred_element_type=F32)
    half = ffn.shape[1] // 2
    fa, fb = ffn[:, :half], ffn[:, half:]
    for k in range(TOP_K):
        ya, yb = _load_token_tiles(y_refs[k], tm)
        w = w_ref[:, k:k + 1]
        fa = fa + ya * w
        fb = fb + yb * w
    ffn = jnp.concatenate([fa, fb], axis=1)
    x2 = _ln(alpha * x1_ref[...] + ffn, g_ref[...], b_ref[...])
    x2_ref[...] = x2
    x2b_ref[...] = x2.astype(BF16)


def _combine_stream(yk, w_t, x1, x1b, sh_gu, sh_down, ln_g, ln_b, alpha, tm):
    L, D = x1.shape
    n = L // tm
    full = lambda shape: pl.BlockSpec(shape, lambda i: (0,) * len(shape))
    y_specs = [pl.BlockSpec((tm * ROW_TILE, LANES), lambda i, k=k: (k * n + i, 0)) for k in range(TOP_K)]
    return pl.pallas_call(
        functools.partial(_combine_stream_kernel, alpha=alpha),
        grid=(n,),
        in_specs=y_specs + [pl.BlockSpec((tm, TOP_K), lambda i: (i, 0)),
                            pl.BlockSpec((tm, D), lambda i: (i, 0)),
                            pl.BlockSpec((tm, D), lambda i: (i, 0)),
                            full(sh_gu.shape), full(sh_down.shape), full((1, D)), full((1, D))],
        out_specs=[pl.BlockSpec((tm, D), lambda i: (i, 0)), pl.BlockSpec((tm, D), lambda i: (i, 0))],
        out_shape=[jax.ShapeDtypeStruct((L, D), F32), jax.ShapeDtypeStruct((L, D), BF16)],
        compiler_params=_cparams(("arbitrary",)),
        name="moe_combine_ln2",
    )(*([yk] * TOP_K), w_t, x1, x1b, sh_gu, sh_down, ln_g.reshape(1, D).astype(F32), ln_b.reshape(1, D).astype(F32))


def _moe_layer(x1, x1b, x1p, logits_t, router_bias, w_gu, w_down, layer, sh_gu, sh_down, ln_g, ln_b, alpha,
               router_tn, combine_tm):
    L, D = x1.shape
    E, K, R = N_EXPERTS, TOP_K, MOE_BLK
    A = K * L
    assert A % R == 0
    nblk = A // R
    e_k, w_k, counts = _router(logits_t, router_bias, router_tn)
    keys = (e_k * L + jnp.arange(L, dtype=jnp.int32)[None, :]) * K + jnp.arange(K, dtype=jnp.int32)[:, None]
    skeys = jnp.sort(keys.reshape(A))
    tok = (skeys // K) % L
    src = tok.reshape(nblk, 1, R)
    dst_sorted = ((skeys % K) * L + tok).reshape(nblk, R)
    ends = jnp.cumsum(counts.reshape(E))
    cuts = jnp.sort(jnp.concatenate([jnp.arange(nblk, dtype=jnp.int32) * R, (ends - counts.reshape(E))]))
    lo = cuts
    hi = jnp.concatenate([cuts[1:], jnp.full((1,), A, jnp.int32)])
    valid = hi > lo
    order = jnp.argsort(jnp.logical_not(valid), stable=True)
    lo, hi = lo[order].astype(jnp.int32), hi[order].astype(jnp.int32)
    n_valid = jnp.sum(valid).astype(jnp.int32).reshape(1)
    blk_w = jnp.minimum(lo // R, nblk - 1).astype(jnp.int32)
    exp_w = jnp.minimum(jnp.sum((ends[None, :] <= lo[:, None]).astype(jnp.int32), axis=1), E - 1).astype(jnp.int32)
    n_out_tiles = A + 2 * R
    n_items = blk_w.shape[0]
    onehot = (blk_w[:, None] == jnp.arange(nblk, dtype=jnp.int32)[None, :]).astype(F32)
    dst_rows = jnp.dot(onehot, dst_sorted.astype(F32), precision=lax.Precision.HIGHEST).astype(jnp.int32)
    pos = blk_w[:, None] * R + jnp.arange(R, dtype=jnp.int32)[None, :]
    spare = (A + (jnp.arange(n_items, dtype=jnp.int32)[:, None] % 2) * R
             + jnp.arange(R, dtype=jnp.int32)[None, :])
    dst = jnp.where((pos >= lo[:, None]) & (pos < hi[:, None]), dst_rows, spare).reshape(n_items, 1, R)
    yk = _experts_fused(x1p, src, dst, (blk_w, exp_w, n_valid), w_gu, w_down, layer, n_out_tiles, R)
    return _combine_stream(yk, w_k.T, x1, x1b, sh_gu, sh_down, ln_g, ln_b, alpha, combine_tm)


def _pick(n, pref):
    t = min(n, pref)
    assert n % t == 0
    return t


def _tile_plan(L):
    return dict(
        proj_rows=_pick(L, 1024),
        attn_q=_pick(L, 256),
        s5_subchunks=_pick(L // S5_SUB, 256),
        mixers_rows=_pick(L, 512),
        outproj_rows=_pick(L, 256),
        router_tokens=_pick(L, 512),
        combine_rows=_pick(L, 256))


def kernel(x, w_in, w_out, mix_norm_g, s5_lambda_re, s5_lambda_im, s5_log_dt, s5_b_re, s5_b_im, s5_c_re, s5_c_im, s5_d, s5_w_glu, conv_w, sgu_ln_g, sgu_ln_b, sgu_w, sgu_b, diff_lq1, diff_lk1, diff_lq2, diff_lk2, diff_subln_g, rel_bias, ln1_g, ln1_b, router_w, router_bias, moe_w_gu, moe_w_down, shared_w_gu, shared_w_down, ln2_g, ln2_b):
    Bt, L, D = x.shape
    assert Bt == 1
    depth = w_in.shape[0]
    alpha = (2 * depth) ** 0.25
    gw = GROUP_W
    t = _tile_plan(L)
    nb = _attn_bias_tables(rel_bias, t['attn_q'])
    xf = x.reshape(L, D)
    xb = xf.astype(BF16)
    for l in range(depth):
        w_in_b = w_in[l].astype(BF16)
        proj_a = _matmul(xb, w_in_b[:, :6 * gw], t['proj_rows'], 3 * gw, F32)
        kmat = _matmul(xb, w_in_b[:, 7 * gw:8 * gw], t['proj_rows'], gw, BF16)
        w_q = (w_in[l][:, 6 * gw:7 * gw] * (DIFF_QK_DIM ** -0.5 * LOG2E)).astype(BF16)
        w_qv_t = jnp.concatenate([w_q, w_in_b[:, 8 * gw:]], axis=1).T
        qvt = _matmul_nt(w_qv_t, xb, t['attn_q'], BF16)
        tabs = _s5_tables(s5_lambda_re[l], s5_lambda_im[l], s5_log_dt[l], s5_b_re[l], s5_b_im[l],
                          s5_c_re[l], s5_c_im[l], t['s5_subchunks'])
        ys5 = _s5_scan(proj_a, tabs, t['s5_subchunks'])
        abc = _mixers(proj_a, ys5, s5_d[l], s5_w_glu[l], conv_w[l], sgu_ln_g[l], sgu_ln_b[l], sgu_w[l], sgu_b[l],
                      mix_norm_g[l], t['mixers_rows'])
        lambda_init = 0.8 - 0.6 * math.exp(-0.3 * l)
        d_out = _diff_attention(qvt, kmat, nb, diff_lq1[l], diff_lk1[l], diff_lq2[l], diff_lk2[l],
                                diff_subln_g[l], lambda_init)
        x1, x1b, x1p, logits_t = _outproj(abc, d_out, xf, w_out[l].astype(BF16), ln1_g[l], ln1_b[l], router_w[l],
                                          alpha, t['outproj_rows'])
        xf, xb = _moe_layer(x1, x1b, x1p, logits_t, router_bias[l], moe_w_gu, moe_w_down, l,
                            shared_w_gu[l].astype(BF16), shared_w_down[l].astype(BF16),
                            ln2_g[l], ln2_b[l], alpha, t['router_tokens'], t['combine_rows'])
    return xf.reshape(Bt, L, D)
```
